```python
import math
import jax, jax.numpy as jnp
from jax import lax
import numpy as np

D_MODEL = 2048
BATCH = 2
SEQ = 8192
DEPTH = 1

RET_HEADS = 8
RET_DK = 128
RET_DV = 128
RET_CHUNK = 128
ROPE_BASE = 10000.0
SWA_HEADS = 16
SWA_KV_HEADS = 4
SWA_HD = 64
SWA_WINDOW = 128
SWA_BLOCK = 128
REL_BUCKETS = 32
REL_MAX_DIST = 128
MEM_LEN = 256
XA_HEADS = 4
XA_HD = D_MODEL // XA_HEADS
D_FF = 5632
CONV_W = 3
LN_EPS = 1e-5
DN_ALPHA = (2 * DEPTH) ** 0.25
DN_BETA = (8 * DEPTH) ** -0.25

RET_W = RET_HEADS * RET_DV
SWA_W = SWA_HEADS * SWA_HD
MIX_W = RET_W + SWA_W
SWA_KV_W = SWA_KV_HEADS * SWA_HD
SPLIT_SIZES = (RET_HEADS * RET_DK, RET_HEADS * RET_DK, RET_W, RET_W, SWA_W, SWA_KV_W, SWA_KV_W)
IN_W = sum(SPLIT_SIZES)

kernel_name = "hybrid_retention_swa_sink_deepnorm_layer"


def layer_norm(x, g, b):
    xf = x.astype(jnp.float32)
    mu = jnp.mean(xf, axis=-1, keepdims=True)
    var = jnp.mean(jnp.square(xf - mu), axis=-1, keepdims=True)
    y = (xf - mu) * lax.rsqrt(var + LN_EPS)
    return (y * g.astype(jnp.float32) + b.astype(jnp.float32)).astype(x.dtype)


def rotate(x, pos):
    half = x.shape[-1] // 2
    inv = 1.0 / (ROPE_BASE ** (jnp.arange(half, dtype=jnp.float32) / half))
    ang = pos.astype(jnp.float32)[:, None] * inv[None, :]
    cos = jnp.cos(ang)[None, :, None, :]
    sin = jnp.sin(ang)[None, :, None, :]
    x1, x2 = x[..., :half], x[..., half:]
    return jnp.concatenate([x1 * cos - x2 * sin, x1 * sin + x2 * cos], axis=-1)


def retention(q, k, v):
    B, S, H, dk = q.shape
    dv = v.shape[-1]
    C = RET_CHUNK
    NC = S // C
    log_gamma = jnp.log1p(-jnp.exp2(-5.0 - jnp.arange(H, dtype=jnp.float32)))
    idx = jnp.arange(C, dtype=jnp.float32)
    diff = idx[:, None] - idx[None, :]
    dmat = jnp.where(diff[None] >= 0,
                     jnp.exp(log_gamma[:, None, None] * jnp.maximum(diff, 0.0)[None]),
                     0.0)
    qc = q.reshape(B, NC, C, H, dk)
    kc = k.reshape(B, NC, C, H, dk)
    vc = v.reshape(B, NC, C, H, dv)
    scores = jnp.einsum('bnihd,bnjhd->bnhij', qc, kc) * dmat
    o_intra = jnp.einsum('bnhij,bnjhe->bnihe', scores, vc)
    k_decay = jnp.exp(log_gamma[:, None] * (C - 1 - idx)[None, :])
    kv = jnp.einsum('bnjhd,hj,bnjhe->bnhde', kc, k_decay, vc)
    chunk_decay = jnp.exp(log_gamma * C)[:, None, None]

    def step(state, kv_n):
        return state * chunk_decay + kv_n, state

    _, prev = lax.scan(step, jnp.zeros((B, H, dk, dv), jnp.float32), jnp.moveaxis(kv, 1, 0))
    prev = jnp.moveaxis(prev, 0, 1)
    q_decay = jnp.exp(log_gamma[:, None] * (idx + 1.0)[None, :])
    o_cross = jnp.einsum('bnihd,bnhde->bnihe', qc, prev) * q_decay.T[None, None, :, :, None]
    return (o_intra + o_cross).reshape(B, S, H, dv)


def t5_bucket(n):
    max_exact = REL_BUCKETS // 2
    nf = jnp.maximum(n, 1).astype(jnp.float32)
    large = max_exact + (jnp.log(nf / max_exact) / math.log(REL_MAX_DIST / max_exact)
                         * (REL_BUCKETS - max_exact)).astype(jnp.int32)
    large = jnp.minimum(large, REL_BUCKETS - 1)
    return jnp.where(n < max_exact, n, large)


def swa_sink_attention(q, k, v, sinks, rel_bias):
    B, S, Hq, d = q.shape
    Hkv = k.shape[2]
    G = Hq // Hkv
    L = SWA_BLOCK
    NB = S // L
    qb = q.reshape(B, NB, L, Hkv, G, d)

    def band(t):
        tb = t.reshape(B, NB, L, Hkv, d)
        prev = jnp.pad(tb, ((0, 0), (1, 0), (0, 0), (0, 0), (0, 0)))[:, :NB]
        return jnp.concatenate([prev, tb], axis=2)

    kb, vb = band(k), band(v)
    logits = jnp.einsum('bnikgd,bnjkd->bnkgij', qb, kb).astype(jnp.float32) * (d ** -0.5)
    i = jnp.arange(L)[:, None]
    j = jnp.arange(2 * L)[None, :]
    dist = i + L - j
    blk = jnp.arange(NB)[:, None, None]
    valid = (dist >= 0) & (dist < SWA_WINDOW) & (blk * L + j - L >= 0)
    bias = rel_bias.astype(jnp.float32)[t5_bucket(jnp.maximum(dist, 0))]
    bias = jnp.transpose(bias, (2, 0, 1)).reshape(Hkv, G, L, 2 * L)
    logits = jnp.where(valid[None, :, None, None], logits + bias[None, None], -jnp.inf)
    sink = sinks.astype(jnp.float32).reshape(Hkv, G)[None, None, :, :, None, None]
    m = jnp.maximum(jnp.max(logits, axis=-1, keepdims=True), sink)
    p = jnp.exp(logits - m)
    p = p / (jnp.sum(p, axis=-1, keepdims=True) + jnp.exp(sink - m))
    out = jnp.einsum('bnkgij,bnjkd->bnikgd', p.astype(vb.dtype), vb)
    return out.reshape(B, S, Hq * d)


def memory_cross_attention(x, mem, wq, wkv, wo):
    B, S, _ = x.shape
    q = (x @ wq).reshape(B, S, XA_HEADS, XA_HD)
    kv = mem @ wkv
    k, v = jnp.split(kv, 2, axis=-1)
    k = k.reshape(B, -1, XA_HEADS, XA_HD)
    v = v.reshape(B, -1, XA_HEADS, XA_HD)
    logits = jnp.einsum('bshd,bmhd->bhsm', q, k).astype(jnp.float32) * (XA_HD ** -0.5)
    p = jax.nn.softmax(logits, axis=-1).astype(v.dtype)
    o = jnp.einsum('bhsm,bmhd->bshd', p, v).reshape(B, S, D_MODEL)
    return o @ wo


def conv_ffn(x, w_up, conv_w, conv_b, w_down):
    S = x.shape[1]
    u, g = jnp.split(x @ w_up, 2, axis=-1)
    gp = jnp.pad(g, ((0, 0), (CONV_W - 1, 0), (0, 0)))
    gc = conv_b + sum(gp[:, tap:tap + S] * conv_w[tap] for tap in range(CONV_W))
    return (jax.nn.silu(gc) * u) @ w_down


def setup_inputs(seed: int = 0) -> dict:
    key = jax.random.key(seed)
    ks = jax.random.split(key, 24)
    f32 = jnp.float32

    def nrm(k, shape, scale):
        return jax.random.normal(k, shape, f32) * scale

    col_scale = jnp.concatenate([
        jnp.full((SPLIT_SIZES[0],), 1.0, f32), jnp.full((SPLIT_SIZES[1],), 1.0, f32),
        jnp.full((SPLIT_SIZES[2],), DN_BETA, f32), jnp.full((SPLIT_SIZES[3],), 1.0, f32),
        jnp.full((SPLIT_SIZES[4],), 1.0, f32), jnp.full((SPLIT_SIZES[5],), 1.0, f32),
        jnp.full((SPLIT_SIZES[6],), DN_BETA, f32)])
    xa_kv_scale = jnp.concatenate([jnp.ones((D_MODEL,), f32), jnp.full((D_MODEL,), DN_BETA, f32)])
    return {
        "x": nrm(ks[0], (BATCH, SEQ, D_MODEL), 1.0),
        "mem": nrm(ks[1], (BATCH, MEM_LEN, D_MODEL), 1.0),
        "w_in": nrm(ks[2], (DEPTH, D_MODEL, IN_W), D_MODEL ** -0.5) * col_scale,
        "ret_gn_g": 1.0 + nrm(ks[3], (DEPTH, RET_W), 0.01),
        "swa_sinks": nrm(ks[4], (DEPTH, SWA_HEADS), 0.5),
        "rel_bias": nrm(ks[5], (REL_BUCKETS, SWA_HEADS), 0.5),
        "w_o": nrm(ks[6], (DEPTH, MIX_W, D_MODEL), MIX_W ** -0.5 * DN_BETA),
        "ln1_g": 1.0 + nrm(ks[7], (DEPTH, D_MODEL), 0.01),
        "ln1_b": nrm(ks[8], (DEPTH, D_MODEL), 0.01),
        "xa_wq": nrm(ks[9], (DEPTH, D_MODEL, D_MODEL), D_MODEL ** -0.5),
        "xa_wkv": nrm(ks[10], (DEPTH, D_MODEL, 2 * D_MODEL), D_MODEL ** -0.5) * xa_kv_scale,
        "xa_wo": nrm(ks[11], (DEPTH, D_MODEL, D_MODEL), D_MODEL ** -0.5 * DN_BETA),
        "ln2_g": 1.0 + nrm(ks[12], (DEPTH, D_MODEL), 0.01),
        "ln2_b": nrm(ks[13], (DEPTH, D_MODEL), 0.01),
        "ffn_w_up": nrm(ks[14], (DEPTH, D_MODEL, 2 * D_FF), D_MODEL ** -0.5 * DN_BETA),
        "ffn_conv_w": nrm(ks[15], (DEPTH, CONV_W, D_FF), CONV_W ** -0.5),
        "ffn_conv_b": nrm(ks[16], (DEPTH, D_FF), 0.01),
        "ffn_w_down": nrm(ks[17], (DEPTH, D_FF, D_MODEL), D_FF ** -0.5 * DN_BETA),
        "ln3_g": 1.0 + nrm(ks[18], (DEPTH, D_MODEL), 0.01),
        "ln3_b": nrm(ks[19], (DEPTH, D_MODEL), 0.01),
    }


def reference(x, mem, w_in, ret_gn_g, swa_sinks, rel_bias, w_o, ln1_g, ln1_b,
              xa_wq, xa_wkv, xa_wo, ln2_g, ln2_b,
              ffn_w_up, ffn_conv_w, ffn_conv_b, ffn_w_down, ln3_g, ln3_b):
    B, S, _ = x.shape
    pos = jnp.arange(S)
    offsets = [int(o) for o in np.cumsum(SPLIT_SIZES)[:-1]]
    for l in range(DEPTH):
        proj = x @ w_in[l]
        q_r, k_r, v_r, g_r, q_s, k_s, v_s = jnp.split(proj, offsets, axis=-1)
        qr = rotate(q_r.astype(jnp.float32).reshape(B, S, RET_HEADS, RET_DK), pos)
        kr = rotate(k_r.astype(jnp.float32).reshape(B, S, RET_HEADS, RET_DK), pos) * (RET_DK ** -0.5)
        vr = v_r.astype(jnp.float32).reshape(B, S, RET_HEADS, RET_DV)
        o_r = retention(qr, kr, vr)
        mu = jnp.mean(o_r, axis=-1, keepdims=True)
        var = jnp.mean(jnp.square(o_r - mu), axis=-1, keepdims=True)
        o_r = ((o_r - mu) * lax.rsqrt(var + LN_EPS)).reshape(B, S, RET_W) * ret_gn_g[l].astype(jnp.float32)
        o_r = (jax.nn.silu(g_r.astype(jnp.float32)) * o_r).astype(x.dtype)
        o_s = swa_sink_attention(q_s.reshape(B, S, SWA_HEADS, SWA_HD),
                                 k_s.reshape(B, S, SWA_KV_HEADS, SWA_HD),
                                 v_s.reshape(B, S, SWA_KV_HEADS, SWA_HD),
                                 swa_sinks[l], rel_bias).astype(x.dtype)
        mix = jnp.concatenate([o_r, o_s], axis=-1) @ w_o[l]
        x = layer_norm(DN_ALPHA * x + mix, ln1_g[l], ln1_b[l])
        xa = memory_cross_attention(x, mem, xa_wq[l], xa_wkv[l], xa_wo[l])
        x = layer_norm(DN_ALPHA * x + xa, ln2_g[l], ln2_b[l])
        ff = conv_ffn(x, ffn_w_up[l], ffn_conv_w[l], ffn_conv_b[l], ffn_w_down[l])
        x = layer_norm(DN_ALPHA * x + ff, ln3_g[l], ln3_b[l])
    return x
```

```python
import functools
import math

import jax
import jax.numpy as jnp
import numpy as np
from jax import lax
from jax.experimental import pallas as pl
from jax.experimental.pallas import tpu as pltpu

RET_HEADS = 8
RET_D = 128
RET_CHUNK = 128
ROPE_BASE = 10000.0
SWA_HEADS = 16
SWA_KV_HEADS = 4
SWA_HD = 64
SWA_WINDOW = 128
SWA_BLOCK = 128
REL_BUCKETS = 32
REL_MAX_DIST = 128
XA_HEADS = 4
CONV_W = 3
LN_EPS = 1e-5
DEPTH = 1
DN_ALPHA = (2 * DEPTH) ** 0.25

RET_W = RET_HEADS * RET_D
SWA_W = SWA_HEADS * SWA_HD
SWA_KV_W = SWA_KV_HEADS * SWA_HD
OFF_RQ, OFF_RK, OFF_RV, OFF_RG = 0, RET_W, 2 * RET_W, 3 * RET_W
OFF_SQ = 4 * RET_W
OFF_SK = OFF_SQ + SWA_W
OFF_SV = OFF_SK + SWA_KV_W
IN_W = OFF_SV + SWA_KV_W

LANES = 128
V7X_VMEM_BYTES = 64 * 1024 * 1024
VMEM_LIMIT = 56 * 1024 * 1024

BF16 = jnp.bfloat16
F32 = jnp.float32


def _cparams(sem, vmem=VMEM_LIMIT):
    return pltpu.CompilerParams(dimension_semantics=sem, vmem_limit_bytes=vmem)


def _layer_norm(y, g, b):
    mu = jnp.mean(y, axis=-1, keepdims=True)
    d = y - mu
    var = jnp.mean(d * d, axis=-1, keepdims=True)
    return d * lax.rsqrt(var + LN_EPS) * g + b


def _dot(a, b):
    return jnp.dot(a, b, preferred_element_type=F32)


def _dot_nt(a, b):
    return lax.dot_general(a, b, (((1,), (1,)), ((), ())), preferred_element_type=F32)


def _dot_tn(a, b):
    return lax.dot_general(a, b, (((0,), (0,)), ((), ())), preferred_element_type=F32)


def _in_proj_kernel(x_ref, w_ref, cos_ref, sin_ref, o_ref, xb_ref, *, tn):
    j = pl.program_id(1)

    @pl.when(j == 0)
    def _():
        xb_ref[...] = x_ref[...].astype(BF16)

    acc = _dot(xb_ref[...], w_ref[...])
    n_rope = (OFF_RV) // tn
    n_q = OFF_RK // tn

    @pl.when(j < n_rope)
    def _():
        cos = cos_ref[...]
        sin = sin_ref[...]
        scale = jnp.where(j >= n_q, RET_D ** -0.5, 1.0).astype(F32)
        for c in range(tn // RET_D):
            a = acc[:, c * RET_D:(c + 1) * RET_D]
            r = pltpu.roll(a, RET_D // 2, 1)
            o_ref[:, c * RET_D:(c + 1) * RET_D] = ((a * cos + r * sin) * scale).astype(o_ref.dtype)

    @pl.when(j >= n_rope)
    def _():
        o_ref[...] = acc.astype(o_ref.dtype)


def _in_proj(x2d, w_bf, cosf, sinf, seq, *, tm=1024, tn=512):
    m, d = x2d.shape
    n = w_bf.shape[1]
    pos_blocks = seq // tm
    return pl.pallas_call(
        functools.partial(_in_proj_kernel, tn=tn),
        out_shape=jax.ShapeDtypeStruct((m, n), BF16),
        grid=(m // tm, n // tn),
        in_specs=[
            pl.BlockSpec((tm, d), lambda i, j: (i, 0)),
            pl.BlockSpec((d, tn), lambda i, j: (0, j)),
            pl.BlockSpec((tm, RET_D), lambda i, j: (i % pos_blocks, 0)),
            pl.BlockSpec((tm, RET_D), lambda i, j: (i % pos_blocks, 0)),
        ],
        out_specs=pl.BlockSpec((tm, tn), lambda i, j: (i, j)),
        scratch_shapes=[pltpu.VMEM((tm, d), BF16)],
        compiler_params=_cparams(("parallel", "arbitrary")),
        name="in_proj_rope",
    )(x2d, w_bf, cosf, sinf)


def _retention_kernel(cdec_ref, q_ref, k_ref, v_ref, g_ref, dmat_ref, kdec_ref, qdec_ref,
                      gn_ref, o_ref, state_ref, *, chunks):
    h = pl.program_id(1)
    n = pl.program_id(2)
    c_sz = RET_CHUNK

    @pl.when(n == 0)
    def _():
        state_ref[...] = jnp.zeros_like(state_ref)

    dmat = dmat_ref[0]
    kdec = kdec_ref[0]
    qdec = qdec_ref[0]
    cdec = cdec_ref[h]
    gn = gn_ref[...]
    state = state_ref[...]
    for c in range(chunks):
        rows = slice(c * c_sz, (c + 1) * c_sz)
        q = q_ref[rows, :]
        k = k_ref[rows, :]
        v = v_ref[rows, :]
        s = _dot_nt(q, k) * dmat
        o = _dot(s.astype(BF16), v)
        o = o + _dot(q, state.astype(BF16)) * qdec
        kd = (k.astype(F32) * kdec).astype(BF16)
        state = state * cdec + _dot_tn(kd, v)
        mu = jnp.mean(o, axis=-1, keepdims=True)
        dlt = o - mu
        var = jnp.mean(dlt * dlt, axis=-1, keepdims=True)
        on = dlt * lax.rsqrt(var + LN_EPS) * gn
        g = g_ref[rows, :].astype(F32)
        o_ref[rows, :] = (g * jax.nn.sigmoid(g) * on).astype(o_ref.dtype)
    state_ref[...] = state


def _retention(proj, cdec, dmat, kdec, qdec, gn, batch, seq, *, rows=1024):
    m = proj.shape[0]
    nblk = seq // rows
    col = lambda off: off // RET_D

    def spec(off):
        return pl.BlockSpec((rows, RET_D), lambda b, h, n: (b * nblk + n, col(off) + h))

    hspec = lambda shape: pl.BlockSpec((1,) + shape, lambda b, h, n: (h, 0, 0))
    return pl.pallas_call(
        functools.partial(_retention_kernel, chunks=rows // RET_CHUNK),
        out_shape=jax.ShapeDtypeStruct((m, RET_W), BF16),
        grid=(batch, RET_HEADS, nblk),
        in_specs=[
            pl.BlockSpec(memory_space=pltpu.SMEM),
            spec(OFF_RQ), spec(OFF_RK), spec(OFF_RV), spec(OFF_RG),
            hspec((RET_CHUNK, RET_CHUNK)), hspec((RET_CHUNK, RET_D)), hspec((RET_CHUNK, RET_D)),
            pl.BlockSpec((1, RET_D), lambda b, h, n: (0, h)),
        ],
        out_specs=pl.BlockSpec((rows, RET_D), lambda b, h, n: (b * nblk + n, h)),
        scratch_shapes=[pltpu.VMEM((RET_D, RET_D), F32)],
        compiler_params=_cparams(("parallel", "parallel", "arbitrary")),
        name="retention",
    )(cdec, proj, proj, proj, proj, dmat, kdec, qdec, gn)


def _bias_kernel(rel_ref, bucket_ref, o_ref):
    h = pl.program_id(0)
    bucket = bucket_ref[...]

    def body(b, acc):
        return jnp.where(bucket == b, rel_ref[b, h], acc)

    o_ref[0] = lax.fori_loop(0, REL_BUCKETS, body, jnp.zeros(bucket.shape, F32))


def _bias_table(rel_bias, bucket):
    l, l2 = bucket.shape
    return pl.pallas_call(
        _bias_kernel,
        out_shape=jax.ShapeDtypeStruct((SWA_HEADS, l, l2), F32),
        grid=(SWA_HEADS,),
        in_specs=[pl.BlockSpec(memory_space=pltpu.SMEM),
                  pl.BlockSpec((l, l2), lambda h: (0, 0))],
        out_specs=pl.BlockSpec((1, l, l2), lambda h: (h, 0, 0)),
        compiler_params=pltpu.CompilerParams(dimension_semantics=("arbitrary",)),
        name="t5_bias_table",
    )(rel_bias, bucket)


def _swa_kernel(sink_ref, q_ref, kc_ref, vc_ref, kp_ref, vp_ref, bias_ref, o_ref, *, blocks):
    nb = pl.program_id(1)
    l = SWA_BLOCK
    grp = SWA_HEADS // SWA_KV_HEADS
    i = lax.broadcasted_iota(jnp.int32, (l, 2 * l), 0)
    j = lax.broadcasted_iota(jnp.int32, (l, 2 * l), 1)
    dist = i + l - j
    band = (dist >= 0) & (dist < SWA_WINDOW)
    for blk in range(blocks):
        r0 = blk * l
        if blk == 0:
            kb = jnp.concatenate([kp_ref[...], kc_ref[0:l, :]], axis=0)
            vb = jnp.concatenate([vp_ref[...], vc_ref[0:l, :]], axis=0)
            valid = band & ((j >= l) | (nb > 0))
        else:
            kb = kc_ref[r0 - l:r0 + l, :]
            vb = vc_ref[r0 - l:r0 + l, :]
            valid = band
        for kh in range(SWA_KV_HEADS):
            kbh = kb[:, kh * SWA_HD:(kh + 1) * SWA_HD]
            vbh = vb[:, kh * SWA_HD:(kh + 1) * SWA_HD]
            for g in range(grp):
                hd = kh * grp + g
                qh = q_ref[r0:r0 + l, hd * SWA_HD:(hd + 1) * SWA_HD]
                logits = _dot_nt(qh, kbh) * (SWA_HD ** -0.5) + bias_ref[hd]
                logits = jnp.where(valid, logits, -jnp.inf)
                sink = sink_ref[hd]
                mx = jnp.maximum(jnp.max(logits, axis=-1, keepdims=True), sink)
                p = jnp.exp(logits - mx)
                den = jnp.sum(p, axis=-1, keepdims=True) + jnp.exp(sink - mx)
                out = _dot(p.astype(BF16), vbh) / den
                o_ref[r0:r0 + l, hd * SWA_HD:(hd + 1) * SWA_HD] = out.astype(o_ref.dtype)


def _swa(proj, sinks, bias, batch, seq, *, rows=256):
    m = proj.shape[0]
    nblk = seq // rows
    per = rows // SWA_BLOCK
    kcol, vcol = OFF_SK // SWA_KV_W, OFF_SV // SWA_KV_W

    def prev_idx(b, n):
        return jnp.maximum((b * nblk + n) * per - 1, 0)

    return pl.pallas_call(
        functools.partial(_swa_kernel, blocks=per),
        out_shape=jax.ShapeDtypeStruct((m, SWA_W), BF16),
        grid=(batch, nblk),
        in_specs=[
            pl.BlockSpec(memory_space=pltpu.SMEM),
            pl.BlockSpec((rows, SWA_W), lambda b, n: (b * nblk + n, OFF_SQ // SWA_W)),
            pl.BlockSpec((rows, SWA_KV_W), lambda b, n: (b * nblk + n, kcol)),
            pl.BlockSpec((rows, SWA_KV_W), lambda b, n: (b * nblk + n, vcol)),
            pl.BlockSpec((SWA_BLOCK, SWA_KV_W), lambda b, n: (prev_idx(b, n), kcol)),
            pl.BlockSpec((SWA_BLOCK, SWA_KV_W), lambda b, n: (prev_idx(b, n), vcol)),
            pl.BlockSpec((SWA_HEADS, SWA_BLOCK, 2 * SWA_BLOCK), lambda b, n: (0, 0, 0)),
        ],
        out_specs=pl.BlockSpec((rows, SWA_W), lambda b, n: (b * nblk + n, 0)),
        compiler_params=_cparams(("parallel", "arbitrary")),
        name="swa_sink_attention",
    )(sinks, proj, proj, proj, proj, proj, bias)


def _proj_res_ln_kernel(a1_ref, a2_ref, w1_ref, w2_ref, x_ref, g_ref, b_ref, o_ref):
    y = _dot(a1_ref[...], w1_ref[...]) + _dot(a2_ref[...], w2_ref[...])
    o_ref[...] = _layer_norm(DN_ALPHA * x_ref[...] + y, g_ref[...], b_ref[...])


def _proj_res_ln(a1, a2, col1, col2, w_bf, x2d, g, b, *, tm=512):
    m, d = x2d.shape
    kh = w_bf.shape[0] // 2
    return pl.pallas_call(
        _proj_res_ln_kernel,
        out_shape=jax.ShapeDtypeStruct((m, d), F32),
        grid=(m // tm,),
        in_specs=[
            pl.BlockSpec((tm, kh), lambda i: (i, col1)),
            pl.BlockSpec((tm, kh), lambda i: (i, col2)),
            pl.BlockSpec((kh, d), lambda i: (0, 0)),
            pl.BlockSpec((kh, d), lambda i: (1, 0)),
            pl.BlockSpec((tm, d), lambda i: (i, 0)),
            pl.BlockSpec((1, d), lambda i: (0, 0)),
            pl.BlockSpec((1, d), lambda i: (0, 0)),
        ],
        out_specs=pl.BlockSpec((tm, d), lambda i: (i, 0)),
        compiler_params=_cparams(("parallel",)),
        name="proj_residual_ln",
    )(a1, a2, w_bf, w_bf, x2d, g, b)


def _matmul_kernel(a_ref, w_ref, o_ref):
    o_ref[...] = _dot(a_ref[...], w_ref[...]).astype(o_ref.dtype)


def _matmul(a_bf, w_bf, *, tm, tn):
    m, k = a_bf.shape
    n = w_bf.shape[1]
    return pl.pallas_call(
        _matmul_kernel,
        out_shape=jax.ShapeDtypeStruct((m, n), BF16),
        grid=(m // tm, n // tn),
        in_specs=[pl.BlockSpec((tm, k), lambda i, j: (i, 0)),
                  pl.BlockSpec((k, tn), lambda i, j: (0, j))],
        out_specs=pl.BlockSpec((tm, tn), lambda i, j: (i, j)),
        compiler_params=_cparams(("parallel", "arbitrary")),
        name="matmul",
    )(a_bf, w_bf)


def _xattn_kernel(x_ref, wq_ref, k_ref, v_ref, o_ref, xb_ref, *, hd):
    h = pl.program_id(1)

    @pl.when(h == 0)
    def _():
        xb_ref[...] = x_ref[...].astype(BF16)

    q = _dot(xb_ref[...], wq_ref[...])
    logits = _dot_nt(q.astype(BF16), k_ref[...]) * (hd ** -0.5)
    mx = jnp.max(logits, axis=-1, keepdims=True)
    p = jnp.exp(logits - mx)
    den = jnp.sum(p, axis=-1, keepdims=True)
    o_ref[...] = (_dot(p.astype(BF16), v_ref[...]) / den).astype(o_ref.dtype)


def _xattn(x2d, wq_bf, kv, seq, mem_len, *, tm=1024):
    m, d = x2d.shape
    hd = d // XA_HEADS
    per_b = seq // tm
    return pl.pallas_call(
        functools.partial(_xattn_kernel, hd=hd),
        out_shape=jax.ShapeDtypeStruct((m, d), BF16),
        grid=(m // tm, XA_HEADS),
        in_specs=[
            pl.BlockSpec((tm, d), lambda i, h: (i, 0)),
            pl.BlockSpec((d, hd), lambda i, h: (0, h)),
            pl.BlockSpec((mem_len, hd), lambda i, h: (i // per_b, h)),
            pl.BlockSpec((mem_len, hd), lambda i, h: (i // per_b, XA_HEADS + h)),
        ],
        out_specs=pl.BlockSpec((tm, hd), lambda i, h: (i, h)),
        scratch_shapes=[pltpu.VMEM((tm, d), BF16)],
        compiler_params=_cparams(("parallel", "arbitrary")),
        name="memory_cross_attention",
    )(x2d, wq_bf, kv, kv)


FFN_HALO = 16


def _ffn_kernel(x_ref, xh_ref, wu_ref, wg_ref, cw_ref, cb_ref, wd_ref, g_ref, b_ref, o_ref,
                xb_ref, *, tm, per_b):
    i = pl.program_id(0)
    f = pl.program_id(1)
    nf = pl.num_programs(1)

    @pl.when(f == 0)
    def _():
        halo = jnp.where(i % per_b == 0, 0.0, xh_ref[...])
        xb_ref[0:FFN_HALO, :] = halo.astype(BF16)
        xb_ref[FFN_HALO:, :] = x_ref[...].astype(BF16)

    u = _dot(xb_ref[FFN_HALO:, :], wu_ref[...])
    ge = _dot(xb_ref[...], wg_ref[...])
    cw = cw_ref[...]
    gc = cb_ref[...]
    for tap in range(CONV_W):
        lo = FFN_HALO - (CONV_W - 1) + tap
        gc = gc + ge[lo:lo + tm, :] * cw[tap:tap + 1, :]
    hcur = (gc * jax.nn.sigmoid(gc) * u).astype(BF16)
    part = _dot(hcur, wd_ref[...])

    @pl.when(f == 0)
    def _():
        o_ref[...] = part

    @pl.when(f > 0)
    def _():
        o_ref[...] += part

    @pl.when(f == nf - 1)
    def _():
        o_ref[...] = _layer_norm(DN_ALPHA * x_ref[...] + o_ref[...], g_ref[...], b_ref[...])


def _ffn(x2d, wup_bf, conv_w, conv_b, wdown_bf, g, b, seq, *, tm=512, fc=512):
    m, d = x2d.shape
    dff = wdown_bf.shape[0]
    nf = dff // fc
    per_b = seq // tm
    halo_per_tile = tm // FFN_HALO
    return pl.pallas_call(
        functools.partial(_ffn_kernel, tm=tm, per_b=per_b),
        out_shape=jax.ShapeDtypeStruct((m, d), F32),
        grid=(m // tm, nf),
        in_specs=[
            pl.BlockSpec((tm, d), lambda i, f: (i, 0)),
            pl.BlockSpec((FFN_HALO, d), lambda i, f: (jnp.maximum(i * halo_per_tile - 1, 0), 0)),
            pl.BlockSpec((d, fc), lambda i, f: (0, f)),
            pl.BlockSpec((d, fc), lambda i, f: (0, nf + f)),
            pl.BlockSpec((CONV_W, fc), lambda i, f: (0, f)),
            pl.BlockSpec((1, fc), lambda i, f: (0, f)),
            pl.BlockSpec((fc, d), lambda i, f: (f, 0)),
            pl.BlockSpec((1, d), lambda i, f: (0, 0)),
            pl.BlockSpec((1, d), lambda i, f: (0, 0)),
        ],
        out_specs=pl.BlockSpec((tm, d), lambda i, f: (i, 0)),
        scratch_shapes=[pltpu.VMEM((tm + FFN_HALO, d), BF16)],
        compiler_params=_cparams(("parallel", "arbitrary")),
        name="conv_ffn_ln",
    )(x2d, x2d, wup_bf, wup_bf, conv_w, conv_b, wdown_bf, g, b)


def _rope_tables(seq):
    half = RET_D // 2
    inv = 1.0 / (ROPE_BASE ** (jnp.arange(half, dtype=F32) / half))
    ang = jnp.arange(seq).astype(F32)[:, None] * inv[None, :]
    cos, sin = jnp.cos(ang), jnp.sin(ang)
    return jnp.concatenate([cos, cos], axis=-1), jnp.concatenate([-sin, sin], axis=-1)


def _retention_tables():
    c = RET_CHUNK
    log_gamma = jnp.log1p(-jnp.exp2(-5.0 - jnp.arange(RET_HEADS, dtype=F32)))
    idx = jnp.arange(c, dtype=F32)
    diff = idx[:, None] - idx[None, :]
    dmat = jnp.where(diff[None] >= 0,
                     jnp.exp(log_gamma[:, None, None] * jnp.maximum(diff, 0.0)[None]), 0.0)
    kdec = jnp.exp(log_gamma[:, None] * (c - 1 - idx)[None, :])
    qdec = jnp.exp(log_gamma[:, None] * (idx + 1.0)[None, :])
    cdec = jnp.exp(log_gamma * c)
    bc = lambda t: jnp.broadcast_to(t[:, :, None], (RET_HEADS, c, RET_D))
    return dmat, bc(kdec), bc(qdec), cdec


def _t5_bucket_table():
    l = SWA_BLOCK
    dist = jnp.maximum(jnp.arange(l)[:, None] + l - jnp.arange(2 * l)[None, :], 0)
    max_exact = REL_BUCKETS // 2
    nf = jnp.maximum(dist, 1).astype(F32)
    large = max_exact + (jnp.log(nf / max_exact) / math.log(REL_MAX_DIST / max_exact)
                         * (REL_BUCKETS - max_exact)).astype(jnp.int32)
    large = jnp.minimum(large, REL_BUCKETS - 1)
    return jnp.where(dist < max_exact, dist, large).astype(jnp.int32)


def kernel(x, mem, w_in, ret_gn_g, swa_sinks, rel_bias, w_o, ln1_g, ln1_b, xa_wq, xa_wkv, xa_wo,
           ln2_g, ln2_b, ffn_w_up, ffn_conv_w, ffn_conv_b, ffn_w_down, ln3_g, ln3_b):
    batch, seq, d = x.shape
    mem_len = mem.shape[1]
    assert w_in.shape[0] == DEPTH and w_in.shape[2] == IN_W
    assert seq % 1024 == 0 and d % LANES == 0

    cosf, sinf = _rope_tables(seq)
    dmat, kdec, qdec, cdec = _retention_tables()
    bucket = _t5_bucket_table()
    bias = _bias_table(rel_bias, bucket)
    row = lambda t: t.reshape(1, -1)

    x2d = x.reshape(batch * seq, d)
    mem_bf = mem.reshape(batch * mem_len, d).astype(BF16)
    for l in range(DEPTH):
        proj = _in_proj(x2d, w_in[l].astype(BF16), cosf, sinf, seq)
        o_r = _retention(proj, cdec, dmat, kdec, qdec, row(ret_gn_g[l]), batch, seq)
        o_s = _swa(proj, swa_sinks[l], bias, batch, seq)
        x2d = _proj_res_ln(o_r, o_s, 0, 0, w_o[l].astype(BF16), x2d, row(ln1_g[l]), row(ln1_b[l]))

        kv = _matmul(mem_bf, xa_wkv[l].astype(BF16), tm=batch * mem_len, tn=1024)
        xa = _xattn(x2d, xa_wq[l].astype(BF16), kv, seq, mem_len)
        x2d = _proj_res_ln(xa, xa, 0, 1, xa_wo[l].astype(BF16), x2d, row(ln2_g[l]), row(ln2_b[l]))

        x2d = _ffn(x2d, ffn_w_up[l].astype(BF16), ffn_conv_w[l], row(ffn_conv_b[l]),
                   ffn_w_down[l].astype(BF16), row(ln3_g[l]), row(ln3_b[l]), seq)
    return x2d.reshape(batch, seq, d)
```

```python
import functools
import math

import jax
import jax.numpy as jnp
import numpy as np
from jax import lax
from jax.experimental import pallas as pl
from jax.experimental.pallas import tpu as pltpu

RET_HEADS = 8
RET_D = 128
RET_CHUNK = 128
ROPE_BASE = 10000.0
SWA_HEADS = 16
SWA_KV_HEADS = 4
SWA_HD = 64
SWA_WINDOW = 128
SWA_BLOCK = 128
REL_BUCKETS = 32
REL_MAX_DIST = 128
XA_HEADS = 4
CONV_W = 3
LN_EPS = 1e-5
DEPTH = 1
DN_ALPHA = (2 * DEPTH) ** 0.25

RET_W = RET_HEADS * RET_D
SWA_W = SWA_HEADS * SWA_HD
SWA_KV_W = SWA_KV_HEADS * SWA_HD
OFF_RQ, OFF_RK, OFF_RV, OFF_RG = 0, RET_W, 2 * RET_W, 3 * RET_W
OFF_SQ = 4 * RET_W
OFF_SK = OFF_SQ + SWA_W
OFF_SV = OFF_SK + SWA_KV_W
IN_W = OFF_SV + SWA_KV_W

LANES = 128
MXU_COLS = 256
V7X_VMEM_BYTES = 64 * 1024 * 1024
VMEM_LIMIT = 56 * 1024 * 1024

BF16 = jnp.bfloat16
F32 = jnp.float32


def _cparams(sem, vmem=VMEM_LIMIT):
    return pltpu.CompilerParams(dimension_semantics=sem, vmem_limit_bytes=vmem)


def _layer_norm(y, g, b):
    mu = jnp.mean(y, axis=-1, keepdims=True)
    d = y - mu
    var = jnp.mean(d * d, axis=-1, keepdims=True)
    return d * lax.rsqrt(var + LN_EPS) * g + b


def _dot(a, b):
    return jnp.dot(a, b, preferred_element_type=F32)


def _dot_nt(a, b):
    return lax.dot_general(a, b, (((1,), (1,)), ((), ())), preferred_element_type=F32)


def _dot_tn(a, b):
    return lax.dot_general(a, b, (((0,), (0,)), ((), ())), preferred_element_type=F32)


IN_PROJ_ROW_SPLITS = 4


def _in_proj_kernel(x_ref, w_ref, cos_ref, sin_ref, o_ref, xb_ref, *, tn):
    j = pl.program_id(1)

    @pl.when(j == 0)
    def _():
        xb_ref[...] = x_ref[...].astype(BF16)

    n_q = OFF_RK // tn
    n_rope = OFF_RV // tn
    scale = jnp.where((j >= n_q) & (j < n_rope), RET_D ** -0.5, 1.0).astype(F32)
    tm = x_ref.shape[0]
    half = tm // IN_PROJ_ROW_SPLITS
    for mh in range(IN_PROJ_ROW_SPLITS):
        rows = slice(mh * half, (mh + 1) * half)
        cos = cos_ref[rows, :]
        sin = sin_ref[rows, :]
        for s in range(tn // MXU_COLS):
            acc = _dot(xb_ref[rows, :], w_ref[:, s * MXU_COLS:(s + 1) * MXU_COLS])
            for c in range(MXU_COLS // RET_D):
                a = acc[:, c * RET_D:(c + 1) * RET_D]
                r = pltpu.roll(a, RET_D // 2, 1)
                lo = s * MXU_COLS + c * RET_D
                o_ref[rows, lo:lo + RET_D] = ((a * cos + r * sin) * scale).astype(o_ref.dtype)


def _in_proj(x2d, w_bf, cosf, sinf, seq, *, tm=1024, tn=512):
    m, d = x2d.shape
    n = w_bf.shape[1]
    pos_blocks = seq // tm
    n_rope = OFF_RV // tn
    cosf = jnp.concatenate([cosf, jnp.ones((tm, RET_D), F32)], axis=0)
    sinf = jnp.concatenate([sinf, jnp.zeros((tm, RET_D), F32)], axis=0)
    tbl_idx = lambda i, j: (jnp.where(j < n_rope, i % pos_blocks, pos_blocks), 0)
    return pl.pallas_call(
        functools.partial(_in_proj_kernel, tn=tn),
        out_shape=jax.ShapeDtypeStruct((m, n), BF16),
        grid=(m // tm, n // tn),
        in_specs=[
            pl.BlockSpec((tm, d), lambda i, j: (i, 0)),
            pl.BlockSpec((d, tn), lambda i, j: (0, j)),
            pl.BlockSpec((tm, RET_D), tbl_idx),
            pl.BlockSpec((tm, RET_D), tbl_idx),
        ],
        out_specs=pl.BlockSpec((tm, tn), lambda i, j: (i, j)),
        scratch_shapes=[pltpu.VMEM((tm, d), BF16)],
        compiler_params=_cparams(("parallel", "arbitrary")),
        name="in_proj_rope",
    )(x2d, w_bf, cosf, sinf)


def _retention_kernel(cdec_ref, q_ref, k_ref, v_ref, g_ref, dmat_ref, kdec_ref, qdec_ref,
                      gn_ref, o_ref, state_ref, *, chunks):
    h = pl.program_id(1)
    n = pl.program_id(2)
    c_sz = RET_CHUNK

    @pl.when(n == 0)
    def _():
        state_ref[...] = jnp.zeros_like(state_ref)

    dmat = dmat_ref[0]
    kdec = kdec_ref[0]
    qdec = qdec_ref[0]
    cdec = cdec_ref[h]
    gn = gn_ref[...]
    state = state_ref[...]
    for c in range(chunks):
        rows = slice(c * c_sz, (c + 1) * c_sz)
        q = q_ref[rows, :]
        k = k_ref[rows, :]
        v = v_ref[rows, :]
        s = _dot_nt(q, k) * dmat
        o = _dot(s.astype(BF16), v)
        o = o + _dot(q, state.astype(BF16)) * qdec
        kd = (k.astype(F32) * kdec).astype(BF16)
        state = state * cdec + _dot_tn(kd, v)
        mu = jnp.mean(o, axis=-1, keepdims=True)
        dlt = o - mu
        var = jnp.mean(dlt * dlt, axis=-1, keepdims=True)
        on = dlt * lax.rsqrt(var + LN_EPS) * gn
        g = g_ref[rows, :].astype(F32)
        o_ref[rows, :] = (g * jax.nn.sigmoid(g) * on).astype(o_ref.dtype)
    state_ref[...] = state


def _retention(proj, cdec, dmat, kdec, qdec, gn, batch, seq, *, rows=1024):
    m = proj.shape[0]
    nblk = seq // rows
    col = lambda off: off // RET_D

    def spec(off):
        return pl.BlockSpec((rows, RET_D), lambda b, h, n: (b * nblk + n, col(off) + h))

    hspec = lambda shape: pl.BlockSpec((1,) + shape, lambda b, h, n: (h, 0, 0))
    return pl.pallas_call(
        functools.partial(_retention_kernel, chunks=rows // RET_CHUNK),
        out_shape=jax.ShapeDtypeStruct((m, RET_W), BF16),
        grid=(batch, RET_HEADS, nblk),
        in_specs=[
            pl.BlockSpec(memory_space=pltpu.SMEM),
            spec(OFF_RQ), spec(OFF_RK), spec(OFF_RV), spec(OFF_RG),
            hspec((RET_CHUNK, RET_CHUNK)), hspec((RET_CHUNK, RET_D)), hspec((RET_CHUNK, RET_D)),
            pl.BlockSpec((1, RET_D), lambda b, h, n: (0, h)),
        ],
        out_specs=pl.BlockSpec((rows, RET_D), lambda b, h, n: (b * nblk + n, h)),
        scratch_shapes=[pltpu.VMEM((RET_D, RET_D), F32)],
        compiler_params=_cparams(("parallel", "parallel", "arbitrary")),
        name="retention",
    )(cdec, proj, proj, proj, proj, dmat, kdec, qdec, gn)


def _bias_kernel(rel_ref, bucket_ref, o_ref):
    h = pl.program_id(0)
    bucket = bucket_ref[...]

    def body(b, acc):
        return jnp.where(bucket == b, rel_ref[b, h], acc)

    o_ref[0] = lax.fori_loop(0, REL_BUCKETS, body, jnp.zeros(bucket.shape, F32))


def _bias_table(rel_bias, bucket):
    l, l2 = bucket.shape
    return pl.pallas_call(
        _bias_kernel,
        out_shape=jax.ShapeDtypeStruct((SWA_HEADS, l, l2), F32),
        grid=(SWA_HEADS,),
        in_specs=[pl.BlockSpec(memory_space=pltpu.SMEM),
                  pl.BlockSpec((l, l2), lambda h: (0, 0))],
        out_specs=pl.BlockSpec((1, l, l2), lambda h: (h, 0, 0)),
        compiler_params=pltpu.CompilerParams(dimension_semantics=("arbitrary",)),
        name="t5_bias_table",
    )(rel_bias, bucket)


def _swa_kernel(sink_ref, q_ref, kc_ref, vc_ref, kp_ref, vp_ref, bias_ref, o_ref, *, blocks):
    nb = pl.program_id(1)
    l = SWA_BLOCK
    grp = SWA_HEADS // SWA_KV_HEADS
    i = lax.broadcasted_iota(jnp.int32, (l, 2 * l), 0)
    j = lax.broadcasted_iota(jnp.int32, (l, 2 * l), 1)
    dist = i + l - j
    band = (dist >= 0) & (dist < SWA_WINDOW)
    for blk in range(blocks):
        r0 = blk * l
        if blk == 0:
            kb = jnp.concatenate([kp_ref[...], kc_ref[0:l, :]], axis=0)
            vb = jnp.concatenate([vp_ref[...], vc_ref[0:l, :]], axis=0)
            valid = band & ((j >= l) | (nb > 0))
        else:
            kb = kc_ref[r0 - l:r0 + l, :]
            vb = vc_ref[r0 - l:r0 + l, :]
            valid = band
        for kh in range(SWA_KV_HEADS):
            kbh = kb[:, kh * SWA_HD:(kh + 1) * SWA_HD]
            vbh = vb[:, kh * SWA_HD:(kh + 1) * SWA_HD]
            for g in range(grp):
                hd = kh * grp + g
                qh = q_ref[r0:r0 + l, hd * SWA_HD:(hd + 1) * SWA_HD]
                logits = _dot_nt(qh, kbh) * (SWA_HD ** -0.5) + bias_ref[hd]
                logits = jnp.where(valid, logits, -jnp.inf)
                sink = sink_ref[hd]
                mx = jnp.maximum(jnp.max(logits, axis=-1, keepdims=True), sink)
                p = jnp.exp(logits - mx)
                den = jnp.sum(p, axis=-1, keepdims=True) + jnp.exp(sink - mx)
                out = _dot(p.astype(BF16), vbh) / den
                o_ref[r0:r0 + l, hd * SWA_HD:(hd + 1) * SWA_HD] = out.astype(o_ref.dtype)


def _swa(proj, sinks, bias, batch, seq, *, rows=256):
    m = proj.shape[0]
    nblk = seq // rows
    per = rows // SWA_BLOCK
    kcol, vcol = OFF_SK // SWA_KV_W, OFF_SV // SWA_KV_W

    def prev_idx(b, n):
        return jnp.maximum((b * nblk + n) * per - 1, 0)

    return pl.pallas_call(
        functools.partial(_swa_kernel, blocks=per),
        out_shape=jax.ShapeDtypeStruct((m, SWA_W), BF16),
        grid=(batch, nblk),
        in_specs=[
            pl.BlockSpec(memory_space=pltpu.SMEM),
            pl.BlockSpec((rows, SWA_W), lambda b, n: (b * nblk + n, OFF_SQ // SWA_W)),
            pl.BlockSpec((rows, SWA_KV_W), lambda b, n: (b * nblk + n, kcol)),
            pl.BlockSpec((rows, SWA_KV_W), lambda b, n: (b * nblk + n, vcol)),
            pl.BlockSpec((SWA_BLOCK, SWA_KV_W), lambda b, n: (prev_idx(b, n), kcol)),
            pl.BlockSpec((SWA_BLOCK, SWA_KV_W), lambda b, n: (prev_idx(b, n), vcol)),
            pl.BlockSpec((SWA_HEADS, SWA_BLOCK, 2 * SWA_BLOCK), lambda b, n: (0, 0, 0)),
        ],
        out_specs=pl.BlockSpec((rows, SWA_W), lambda b, n: (b * nblk + n, 0)),
        compiler_params=_cparams(("parallel", "arbitrary")),
        name="swa_sink_attention",
    )(sinks, proj, proj, proj, proj, proj, bias)


def _proj_res_ln_kernel(a1_ref, a2_ref, w1_ref, w2_ref, x_ref, g_ref, b_ref, o_ref):
    y = _dot(a1_ref[...], w1_ref[...]) + _dot(a2_ref[...], w2_ref[...])
    o_ref[...] = _layer_norm(DN_ALPHA * x_ref[...] + y, g_ref[...], b_ref[...])


def _proj_res_ln(a1, a2, col1, col2, w_bf, x2d, g, b, *, tm=512):
    m, d = x2d.shape
    kh = w_bf.shape[0] // 2
    return pl.pallas_call(
        _proj_res_ln_kernel,
        out_shape=jax.ShapeDtypeStruct((m, d), F32),
        grid=(m // tm,),
        in_specs=[
            pl.BlockSpec((tm, kh), lambda i: (i, col1)),
            pl.BlockSpec((tm, kh), lambda i: (i, col2)),
            pl.BlockSpec((kh, d), lambda i: (0, 0)),
            pl.BlockSpec((kh, d), lambda i: (1, 0)),
            pl.BlockSpec((tm, d), lambda i: (i, 0)),
            pl.BlockSpec((1, d), lambda i: (0, 0)),
            pl.BlockSpec((1, d), lambda i: (0, 0)),
        ],
        out_specs=pl.BlockSpec((tm, d), lambda i: (i, 0)),
        compiler_params=_cparams(("parallel",)),
        name="proj_residual_ln",
    )(a1, a2, w_bf, w_bf, x2d, g, b)


def _matmul_kernel(a_ref, w_ref, o_ref):
    o_ref[...] = _dot(a_ref[...], w_ref[...]).astype(o_ref.dtype)


def _matmul(a_bf, w_bf, *, tm, tn):
    m, k = a_bf.shape
    n = w_bf.shape[1]
    return pl.pallas_call(
        _matmul_kernel,
        out_shape=jax.ShapeDtypeStruct((m, n), BF16),
        grid=(m // tm, n // tn),
        in_specs=[pl.BlockSpec((tm, k), lambda i, j: (i, 0)),
                  pl.BlockSpec((k, tn), lambda i, j: (0, j))],
        out_specs=pl.BlockSpec((tm, tn), lambda i, j: (i, j)),
        compiler_params=_cparams(("parallel", "arbitrary")),
        name="matmul",
    )(a_bf, w_bf)


def _xattn_kernel(x_ref, wq_ref, k_ref, v_ref, o_ref, xb_ref, *, hd):
    h = pl.program_id(1)

    @pl.when(h == 0)
    def _():
        xb_ref[...] = x_ref[...].astype(BF16)

    q = _dot(xb_ref[...], wq_ref[...])
    logits = _dot_nt(q.astype(BF16), k_ref[...]) * (hd ** -0.5)
    mx = jnp.max(logits, axis=-1, keepdims=True)
    p = jnp.exp(logits - mx)
    den = jnp.sum(p, axis=-1, keepdims=True)
    o_ref[...] = (_dot(p.astype(BF16), v_ref[...]) / den).astype(o_ref.dtype)


def _xattn(x2d, wq_bf, kv, seq, mem_len, *, tm=1024):
    m, d = x2d.shape
    hd = d // XA_HEADS
    per_b = seq // tm
    return pl.pallas_call(
        functools.partial(_xattn_kernel, hd=hd),
        out_shape=jax.ShapeDtypeStruct((m, d), BF16),
        grid=(m // tm, XA_HEADS),
        in_specs=[
            pl.BlockSpec((tm, d), lambda i, h: (i, 0)),
            pl.BlockSpec((d, hd), lambda i, h: (0, h)),
            pl.BlockSpec((mem_len, hd), lambda i, h: (i // per_b, h)),
            pl.BlockSpec((mem_len, hd), lambda i, h: (i // per_b, XA_HEADS + h)),
        ],
        out_specs=pl.BlockSpec((tm, hd), lambda i, h: (i, h)),
        scratch_shapes=[pltpu.VMEM((tm, d), BF16)],
        compiler_params=_cparams(("parallel", "arbitrary")),
        name="memory_cross_attention",
    )(x2d, wq_bf, kv, kv)


FFN_HALO = 16
FFN_GROUP_ROWS = 256


def _ffn_kernel(x_ref, xh_ref, wu_ref, wg_ref, cw_ref, cb_ref, wd_ref, g_ref, b_ref, o_ref,
                xb_ref, *, tm, per_b):
    i = pl.program_id(0)
    f = pl.program_id(1)
    nf = pl.num_programs(1)

    @pl.when(f == 0)
    def _():
        halo = jnp.where(i % per_b == 0, 0.0, xh_ref[...])
        xb_ref[0:FFN_HALO, :] = halo.astype(BF16)
        xb_ref[FFN_HALO:, :] = x_ref[...].astype(BF16)
        o_ref[...] = jnp.zeros_like(o_ref)

    cw = cw_ref[...]
    cb = cb_ref[...]
    def up(r):
        r0 = r * FFN_GROUP_ROWS
        u = _dot(xb_ref[FFN_HALO + r0:FFN_HALO + r0 + FFN_GROUP_ROWS, :], wu_ref[...])
        ge = _dot(xb_ref[r0:r0 + FFN_GROUP_ROWS + FFN_HALO, :], wg_ref[...])
        return u, ge

    groups = tm // FFN_GROUP_ROWS
    nxt = up(0)
    for r in range(groups):
        u, ge = nxt
        if r + 1 < groups:
            nxt = up(r + 1)
        gc = cb
        for tap in range(CONV_W):
            lo = FFN_HALO - (CONV_W - 1) + tap
            gc = gc + ge[lo:lo + FFN_GROUP_ROWS, :] * cw[tap:tap + 1, :]
        hcur = (gc * jax.nn.sigmoid(gc) * u).astype(BF16)
        r0 = r * FFN_GROUP_ROWS
        o_ref[r0:r0 + FFN_GROUP_ROWS, :] += _dot(hcur, wd_ref[...])

    @pl.when(f == nf - 1)
    def _():
        o_ref[...] = _layer_norm(DN_ALPHA * x_ref[...] + o_ref[...], g_ref[...], b_ref[...])


def _ffn(x2d, wup_bf, conv_w, conv_b, wdown_bf, g, b, seq, *, tm=512, fc=512):
    m, d = x2d.shape
    dff = wdown_bf.shape[0]
    nf = dff // fc
    per_b = seq // tm
    halo_per_tile = tm // FFN_HALO
    return pl.pallas_call(
        functools.partial(_ffn_kernel, tm=tm, per_b=per_b),
        out_shape=jax.ShapeDtypeStruct((m, d), F32),
        grid=(m // tm, nf),
        in_specs=[
            pl.BlockSpec((tm, d), lambda i, f: (i, 0)),
            pl.BlockSpec((FFN_HALO, d), lambda i, f: (jnp.maximum(i * halo_per_tile - 1, 0), 0)),
            pl.BlockSpec((d, fc), lambda i, f: (0, f)),
            pl.BlockSpec((d, fc), lambda i, f: (0, nf + f)),
            pl.BlockSpec((CONV_W, fc), lambda i, f: (0, f)),
            pl.BlockSpec((1, fc), lambda i, f: (0, f)),
            pl.BlockSpec((fc, d), lambda i, f: (f, 0)),
            pl.BlockSpec((1, d), lambda i, f: (0, 0)),
            pl.BlockSpec((1, d), lambda i, f: (0, 0)),
        ],
        out_specs=pl.BlockSpec((tm, d), lambda i, f: (i, 0)),
        scratch_shapes=[pltpu.VMEM((tm + FFN_HALO, d), BF16)],
        compiler_params=_cparams(("parallel", "arbitrary")),
        name="conv_ffn_ln",
    )(x2d, x2d, wup_bf, wup_bf, conv_w, conv_b, wdown_bf, g, b)


def _rope_tables(seq):
    half = RET_D // 2
    inv = 1.0 / (ROPE_BASE ** (jnp.arange(half, dtype=F32) / half))
    ang = jnp.arange(seq).astype(F32)[:, None] * inv[None, :]
    cos, sin = jnp.cos(ang), jnp.sin(ang)
    return jnp.concatenate([cos, cos], axis=-1), jnp.concatenate([-sin, sin], axis=-1)


def _retention_tables():
    c = RET_CHUNK
    log_gamma = jnp.log1p(-jnp.exp2(-5.0 - jnp.arange(RET_HEADS, dtype=F32)))
    idx = jnp.arange(c, dtype=F32)
    diff = idx[:, None] - idx[None, :]
    dmat = jnp.where(diff[None] >= 0,
                     jnp.exp(log_gamma[:, None, None] * jnp.maximum(diff, 0.0)[None]), 0.0)
    kdec = jnp.exp(log_gamma[:, None] * (c - 1 - idx)[None, :])
    qdec = jnp.exp(log_gamma[:, None] * (idx + 1.0)[None, :])
    cdec = jnp.exp(log_gamma * c)
    bc = lambda t: jnp.broadcast_to(t[:, :, None], (RET_HEADS, c, RET_D))
    return dmat, bc(kdec), bc(qdec), cdec


def _t5_bucket_table():
    l = SWA_BLOCK
    dist = jnp.maximum(jnp.arange(l)[:, None] + l - jnp.arange(2 * l)[None, :], 0)
    max_exact = REL_BUCKETS // 2
    nf = jnp.maximum(dist, 1).astype(F32)
    large = max_exact + (jnp.log(nf / max_exact) / math.log(REL_MAX_DIST / max_exact)
                         * (REL_BUCKETS - max_exact)).astype(jnp.int32)
    large = jnp.minimum(large, REL_BUCKETS - 1)
    return jnp.where(dist < max_exact, dist, large).astype(jnp.int32)


def kernel(x, mem, w_in, ret_gn_g, swa_sinks, rel_bias, w_o, ln1_g, ln1_b, xa_wq, xa_wkv, xa_wo,
           ln2_g, ln2_b, ffn_w_up, ffn_conv_w, ffn_conv_b, ffn_w_down, ln3_g, ln3_b):
    batch, seq, d = x.shape
    mem_len = mem.shape[1]
    assert w_in.shape[0] == DEPTH and w_in.shape[2] == IN_W
    assert seq % 1024 == 0 and d % LANES == 0

    cosf, sinf = _rope_tables(seq)
    dmat, kdec, qdec, cdec = _retention_tables()
    bucket = _t5_bucket_table()
    bias = _bias_table(rel_bias, bucket)
    row = lambda t: t.reshape(1, -1)

    x2d = x.reshape(batch * seq, d)
    mem_bf = mem.reshape(batch * mem_len, d).astype(BF16)
    for l in range(DEPTH):
        proj = _in_proj(x2d, w_in[l].astype(BF16), cosf, sinf, seq)
        o_r = _retention(proj, cdec, dmat, kdec, qdec, row(ret_gn_g[l]), batch, seq)
        o_s = _swa(proj, swa_sinks[l], bias, batch, seq)
        x2d = _proj_res_ln(o_r, o_s, 0, 0, w_o[l].astype(BF16), x2d, row(ln1_g[l]), row(ln1_b[l]))

        kv = _matmul(mem_bf, xa_wkv[l].astype(BF16), tm=batch * mem_len, tn=1024)
        xa = _xattn(x2d, xa_wq[l].astype(BF16), kv, seq, mem_len)
        x2d = _proj_res_ln(xa, xa, 0, 1, xa_wo[l].astype(BF16), x2d, row(ln2_g[l]), row(ln2_b[l]))

        x2d = _ffn(x2d, ffn_w_up[l].astype(BF16), ffn_conv_w[l], row(ffn_conv_b[l]),
                   ffn_w_down[l].astype(BF16), row(ln3_g[l]), row(ln3_b[l]), seq)
    return x2d.reshape(batch, seq, d)
```

```python
import functools
import math

import jax
import jax.numpy as jnp
import numpy as np
from jax import lax
from jax.experimental import pallas as pl
from jax.experimental.pallas import tpu as pltpu

RET_HEADS = 8
RET_D = 128
RET_CHUNK = 128
ROPE_BASE = 10000.0
SWA_HEADS = 16
SWA_KV_HEADS = 4
SWA_HD = 64
SWA_WINDOW = 128
SWA_BLOCK = 128
REL_BUCKETS = 32
REL_MAX_DIST = 128
XA_HEADS = 4
CONV_W = 3
LN_EPS = 1e-5
DEPTH = 1
DN_ALPHA = (2 * DEPTH) ** 0.25

RET_W = RET_HEADS * RET_D
SWA_W = SWA_HEADS * SWA_HD
SWA_KV_W = SWA_KV_HEADS * SWA_HD
OFF_RQ, OFF_RK, OFF_RV, OFF_RG = 0, RET_W, 2 * RET_W, 3 * RET_W
OFF_SQ = 4 * RET_W
OFF_SK = OFF_SQ + SWA_W
OFF_SV = OFF_SK + SWA_KV_W
IN_W = OFF_SV + SWA_KV_W

LANES = 128
MXU_COLS = 256
V7X_VMEM_BYTES = 64 * 1024 * 1024
VMEM_LIMIT = 56 * 1024 * 1024

BF16 = jnp.bfloat16
F32 = jnp.float32


def _cparams(sem, vmem=VMEM_LIMIT):
    return pltpu.CompilerParams(dimension_semantics=sem, vmem_limit_bytes=vmem)


def _layer_norm(y, g, b):
    mu = jnp.mean(y, axis=-1, keepdims=True)
    d = y - mu
    var = jnp.mean(d * d, axis=-1, keepdims=True)
    return d * lax.rsqrt(var + LN_EPS) * g + b


def _dot(a, b):
    return jnp.dot(a, b, preferred_element_type=F32)


def _dot_nt(a, b):
    return lax.dot_general(a, b, (((1,), (1,)), ((), ())), preferred_element_type=F32)


def _dot_tn(a, b):
    return lax.dot_general(a, b, (((0,), (0,)), ((), ())), preferred_element_type=F32)


IN_PROJ_ROW_SPLITS = 4


def _in_proj_kernel(x_ref, w_ref, cos_ref, sin_ref, o_ref, xb_ref, *, tn):
    j = pl.program_id(1)

    @pl.when(j == 0)
    def _():
        xb_ref[...] = x_ref[...].astype(BF16)

    n_q = OFF_RK // tn
    n_rope = OFF_RV // tn
    scale = jnp.where((j >= n_q) & (j < n_rope), RET_D ** -0.5, 1.0).astype(F32)
    tm = x_ref.shape[0]
    half = tm // IN_PROJ_ROW_SPLITS
    for mh in range(IN_PROJ_ROW_SPLITS):
        rows = slice(mh * half, (mh + 1) * half)
        cos = cos_ref[rows, :]
        sin = sin_ref[rows, :]
        for s in range(tn // MXU_COLS):
            acc = _dot(xb_ref[rows, :], w_ref[:, s * MXU_COLS:(s + 1) * MXU_COLS])
            for c in range(MXU_COLS // RET_D):
                a = acc[:, c * RET_D:(c + 1) * RET_D]
                r = pltpu.roll(a, RET_D // 2, 1)
                lo = s * MXU_COLS + c * RET_D
                o_ref[rows, lo:lo + RET_D] = ((a * cos + r * sin) * scale).astype(o_ref.dtype)


def _in_proj(x2d, w_bf, cosf, sinf, seq, *, tm=1024, tn=512):
    m, d = x2d.shape
    n = w_bf.shape[1]
    pos_blocks = seq // tm
    n_rope = OFF_RV // tn
    cosf = jnp.concatenate([cosf, jnp.ones((tm, RET_D), F32)], axis=0)
    sinf = jnp.concatenate([sinf, jnp.zeros((tm, RET_D), F32)], axis=0)
    tbl_idx = lambda i, j: (jnp.where(j < n_rope, i % pos_blocks, pos_blocks), 0)
    return pl.pallas_call(
        functools.partial(_in_proj_kernel, tn=tn),
        out_shape=jax.ShapeDtypeStruct((m, n), BF16),
        grid=(m // tm, n // tn),
        in_specs=[
            pl.BlockSpec((tm, d), lambda i, j: (i, 0)),
            pl.BlockSpec((d, tn), lambda i, j: (0, j)),
            pl.BlockSpec((tm, RET_D), tbl_idx),
            pl.BlockSpec((tm, RET_D), tbl_idx),
        ],
        out_specs=pl.BlockSpec((tm, tn), lambda i, j: (i, j)),
        scratch_shapes=[pltpu.VMEM((tm, d), BF16)],
        compiler_params=_cparams(("parallel", "arbitrary")),
        name="in_proj_rope",
    )(x2d, w_bf, cosf, sinf)


def _retention_kernel(cdec_ref, q_ref, k_ref, v_ref, g_ref, dmat_ref, kdec_ref, qdec_ref,
                      gn_ref, o_ref, state_ref, *, chunks):
    h = pl.program_id(1)
    n = pl.program_id(2)
    c_sz = RET_CHUNK

    @pl.when(n == 0)
    def _():
        state_ref[...] = jnp.zeros_like(state_ref)

    dmat = dmat_ref[0]
    kdec = kdec_ref[0]
    qdec = qdec_ref[0]
    cdec = cdec_ref[h]
    gn = gn_ref[...]
    state = state_ref[...]
    for c in range(chunks):
        rows = slice(c * c_sz, (c + 1) * c_sz)
        q = q_ref[rows, :]
        k = k_ref[rows, :]
        v = v_ref[rows, :]
        s = _dot_nt(q, k) * dmat
        o = _dot(s.astype(BF16), v)
        o = o + _dot(q, state.astype(BF16)) * qdec
        kd = (k.astype(F32) * kdec).astype(BF16)
        state = state * cdec + _dot_tn(kd, v)
        mu = jnp.mean(o, axis=-1, keepdims=True)
        dlt = o - mu
        var = jnp.mean(dlt * dlt, axis=-1, keepdims=True)
        on = dlt * lax.rsqrt(var + LN_EPS) * gn
        g = g_ref[rows, :].astype(F32)
        o_ref[rows, :] = (g * jax.nn.sigmoid(g) * on).astype(o_ref.dtype)
    state_ref[...] = state


def _retention(proj, cdec, dmat, kdec, qdec, gn, batch, seq, *, rows=1024):
    m = proj.shape[0]
    nblk = seq // rows
    col = lambda off: off // RET_D

    def spec(off):
        return pl.BlockSpec((rows, RET_D), lambda b, h, n: (b * nblk + n, col(off) + h))

    hspec = lambda shape: pl.BlockSpec((1,) + shape, lambda b, h, n: (h, 0, 0))
    return pl.pallas_call(
        functools.partial(_retention_kernel, chunks=rows // RET_CHUNK),
        out_shape=jax.ShapeDtypeStruct((m, RET_W), BF16),
        grid=(batch, RET_HEADS, nblk),
        in_specs=[
            pl.BlockSpec(memory_space=pltpu.SMEM),
            spec(OFF_RQ), spec(OFF_RK), spec(OFF_RV), spec(OFF_RG),
            hspec((RET_CHUNK, RET_CHUNK)), hspec((RET_CHUNK, RET_D)), hspec((RET_CHUNK, RET_D)),
            pl.BlockSpec((1, RET_D), lambda b, h, n: (0, h)),
        ],
        out_specs=pl.BlockSpec((rows, RET_D), lambda b, h, n: (b * nblk + n, h)),
        scratch_shapes=[pltpu.VMEM((RET_D, RET_D), F32)],
        compiler_params=_cparams(("parallel", "parallel", "arbitrary")),
        name="retention",
    )(cdec, proj, proj, proj, proj, dmat, kdec, qdec, gn)


def _bias_kernel(rel_ref, bucket_ref, o_ref):
    h = pl.program_id(0)
    bucket = bucket_ref[...]

    def body(b, acc):
        return jnp.where(bucket == b, rel_ref[b, h], acc)

    o_ref[0] = lax.fori_loop(0, REL_BUCKETS, body, jnp.zeros(bucket.shape, F32))


def _bias_table(rel_bias, bucket):
    l, l2 = bucket.shape
    return pl.pallas_call(
        _bias_kernel,
        out_shape=jax.ShapeDtypeStruct((SWA_HEADS, l, l2), F32),
        grid=(SWA_HEADS,),
        in_specs=[pl.BlockSpec(memory_space=pltpu.SMEM),
                  pl.BlockSpec((l, l2), lambda h: (0, 0))],
        out_specs=pl.BlockSpec((1, l, l2), lambda h: (h, 0, 0)),
        compiler_params=pltpu.CompilerParams(dimension_semantics=("arbitrary",)),
        name="t5_bias_table",
    )(rel_bias, bucket)


def _swa_kernel(sink_ref, q_ref, kc_ref, vc_ref, kp_ref, vp_ref, bias_ref, o_ref, *, blocks):
    nb = pl.program_id(1)
    l = SWA_BLOCK
    grp = SWA_HEADS // SWA_KV_HEADS
    i = lax.broadcasted_iota(jnp.int32, (l, 2 * l), 0)
    j = lax.broadcasted_iota(jnp.int32, (l, 2 * l), 1)
    dist = i + l - j
    band = (dist >= 0) & (dist < SWA_WINDOW)
    for blk in range(blocks):
        r0 = blk * l
        if blk == 0:
            kb = jnp.concatenate([kp_ref[...], kc_ref[0:l, :]], axis=0)
            vb = jnp.concatenate([vp_ref[...], vc_ref[0:l, :]], axis=0)
            valid = band & ((j >= l) | (nb > 0))
        else:
            kb = kc_ref[r0 - l:r0 + l, :]
            vb = vc_ref[r0 - l:r0 + l, :]
            valid = band
        for kh in range(SWA_KV_HEADS):
            kbh = kb[:, kh * SWA_HD:(kh + 1) * SWA_HD]
            vbh = vb[:, kh * SWA_HD:(kh + 1) * SWA_HD]
            for g in range(grp):
                hd = kh * grp + g
                qh = q_ref[r0:r0 + l, hd * SWA_HD:(hd + 1) * SWA_HD]
                logits = _dot_nt(qh, kbh) * (SWA_HD ** -0.5) + bias_ref[hd]
                logits = jnp.where(valid, logits, -jnp.inf)
                sink = sink_ref[hd]
                mx = jnp.maximum(jnp.max(logits, axis=-1, keepdims=True), sink)
                p = jnp.exp(logits - mx)
                den = jnp.sum(p, axis=-1, keepdims=True) + jnp.exp(sink - mx)
                out = _dot(p.astype(BF16), vbh) / den
                o_ref[r0:r0 + l, hd * SWA_HD:(hd + 1) * SWA_HD] = out.astype(o_ref.dtype)


def _swa(proj, sinks, bias, batch, seq, *, rows=256):
    m = proj.shape[0]
    nblk = seq // rows
    per = rows // SWA_BLOCK
    kcol, vcol = OFF_SK // SWA_KV_W, OFF_SV // SWA_KV_W

    def prev_idx(b, n):
        return jnp.maximum((b * nblk + n) * per - 1, 0)

    return pl.pallas_call(
        functools.partial(_swa_kernel, blocks=per),
        out_shape=jax.ShapeDtypeStruct((m, SWA_W), BF16),
        grid=(batch, nblk),
        in_specs=[
            pl.BlockSpec(memory_space=pltpu.SMEM),
            pl.BlockSpec((rows, SWA_W), lambda b, n: (b * nblk + n, OFF_SQ // SWA_W)),
            pl.BlockSpec((rows, SWA_KV_W), lambda b, n: (b * nblk + n, kcol)),
            pl.BlockSpec((rows, SWA_KV_W), lambda b, n: (b * nblk + n, vcol)),
            pl.BlockSpec((SWA_BLOCK, SWA_KV_W), lambda b, n: (prev_idx(b, n), kcol)),
            pl.BlockSpec((SWA_BLOCK, SWA_KV_W), lambda b, n: (prev_idx(b, n), vcol)),
            pl.BlockSpec((SWA_HEADS, SWA_BLOCK, 2 * SWA_BLOCK), lambda b, n: (0, 0, 0)),
        ],
        out_specs=pl.BlockSpec((rows, SWA_W), lambda b, n: (b * nblk + n, 0)),
        compiler_params=_cparams(("parallel", "arbitrary")),
        name="swa_sink_attention",
    )(sinks, proj, proj, proj, proj, proj, bias)


def _proj_res_ln_kernel(a1_ref, a2_ref, w1_ref, w2_ref, x_ref, g_ref, b_ref, o_ref):
    y = _dot(a1_ref[...], w1_ref[...]) + _dot(a2_ref[...], w2_ref[...])
    o_ref[...] = _layer_norm(DN_ALPHA * x_ref[...] + y, g_ref[...], b_ref[...])


def _proj_res_ln(a1, a2, col1, col2, w_bf, x2d, g, b, *, tm=512):
    m, d = x2d.shape
    kh = w_bf.shape[0] // 2
    return pl.pallas_call(
        _proj_res_ln_kernel,
        out_shape=jax.ShapeDtypeStruct((m, d), F32),
        grid=(m // tm,),
        in_specs=[
            pl.BlockSpec((tm, kh), lambda i: (i, col1)),
            pl.BlockSpec((tm, kh), lambda i: (i, col2)),
            pl.BlockSpec((kh, d), lambda i: (0, 0)),
            pl.BlockSpec((kh, d), lambda i: (1, 0)),
            pl.BlockSpec((tm, d), lambda i: (i, 0)),
            pl.BlockSpec((1, d), lambda i: (0, 0)),
            pl.BlockSpec((1, d), lambda i: (0, 0)),
        ],
        out_specs=pl.BlockSpec((tm, d), lambda i: (i, 0)),
        compiler_params=_cparams(("parallel",)),
        name="proj_residual_ln",
    )(a1, a2, w_bf, w_bf, x2d, g, b)


def _matmul_kernel(a_ref, w_ref, o_ref):
    o_ref[...] = _dot(a_ref[...], w_ref[...]).astype(o_ref.dtype)


def _matmul(a_bf, w_bf, *, tm, tn):
    m, k = a_bf.shape
    n = w_bf.shape[1]
    return pl.pallas_call(
        _matmul_kernel,
        out_shape=jax.ShapeDtypeStruct((m, n), BF16),
        grid=(m // tm, n // tn),
        in_specs=[pl.BlockSpec((tm, k), lambda i, j: (i, 0)),
                  pl.BlockSpec((k, tn), lambda i, j: (0, j))],
        out_specs=pl.BlockSpec((tm, tn), lambda i, j: (i, j)),
        compiler_params=_cparams(("parallel", "arbitrary")),
        name="matmul",
    )(a_bf, w_bf)


def _xattn_kernel(x_ref, wq_ref, k_ref, v_ref, o_ref, xb_ref, *, hd):
    h = pl.program_id(1)

    @pl.when(h == 0)
    def _():
        xb_ref[...] = x_ref[...].astype(BF16)

    q = _dot(xb_ref[...], wq_ref[...])
    logits = _dot_nt(q.astype(BF16), k_ref[...]) * (hd ** -0.5)
    mx = jnp.max(logits, axis=-1, keepdims=True)
    p = jnp.exp(logits - mx)
    den = jnp.sum(p, axis=-1, keepdims=True)
    o_ref[...] = (_dot(p.astype(BF16), v_ref[...]) / den).astype(o_ref.dtype)


def _xattn(x2d, wq_bf, kv, seq, mem_len, *, tm=1024):
    m, d = x2d.shape
    hd = d // XA_HEADS
    per_b = seq // tm
    return pl.pallas_call(
        functools.partial(_xattn_kernel, hd=hd),
        out_shape=jax.ShapeDtypeStruct((m, d), BF16),
        grid=(m // tm, XA_HEADS),
        in_specs=[
            pl.BlockSpec((tm, d), lambda i, h: (i, 0)),
            pl.BlockSpec((d, hd), lambda i, h: (0, h)),
            pl.BlockSpec((mem_len, hd), lambda i, h: (i // per_b, h)),
            pl.BlockSpec((mem_len, hd), lambda i, h: (i // per_b, XA_HEADS + h)),
        ],
        out_specs=pl.BlockSpec((tm, hd), lambda i, h: (i, h)),
        scratch_shapes=[pltpu.VMEM((tm, d), BF16)],
        compiler_params=_cparams(("parallel", "arbitrary")),
        name="memory_cross_attention",
    )(x2d, wq_bf, kv, kv)


FFN_HALO = 16
FFN_GROUP_ROWS = 256
FFN_CHUNK = 256


def _ffn_kernel(x_ref, xh_ref, wu_ref, wg_ref, cw_ref, cb_ref, wd_ref, g_ref, b_ref, o_ref,
                xb_ref, *, tm, per_b):
    i = pl.program_id(0)
    f = pl.program_id(1)
    nf = pl.num_programs(1)

    @pl.when(f == 0)
    def _():
        halo = jnp.where(i % per_b == 0, 0.0, xh_ref[...])
        xb_ref[0:FFN_HALO, :] = halo.astype(BF16)
        xb_ref[FFN_HALO:, :] = x_ref[...].astype(BF16)
        o_ref[...] = jnp.zeros_like(o_ref)

    cw = cw_ref[...]
    cb = cb_ref[...]
    def up(r):
        r0 = r * FFN_GROUP_ROWS
        u = _dot(xb_ref[FFN_HALO + r0:FFN_HALO + r0 + FFN_GROUP_ROWS, :], wu_ref[0])
        ge = _dot(xb_ref[r0:r0 + FFN_GROUP_ROWS + FFN_HALO, :], wg_ref[0])
        return u, ge

    groups = tm // FFN_GROUP_ROWS
    nxt = up(0)
    for r in range(groups):
        u, ge = nxt
        if r + 1 < groups:
            nxt = up(r + 1)
        gc = cb
        for tap in range(CONV_W):
            lo = FFN_HALO - (CONV_W - 1) + tap
            gc = gc + ge[lo:lo + FFN_GROUP_ROWS, :] * cw[tap:tap + 1, :]
        hcur = (gc * jax.nn.sigmoid(gc) * u).astype(BF16)
        r0 = r * FFN_GROUP_ROWS
        o_ref[r0:r0 + FFN_GROUP_ROWS, :] += _dot(hcur, wd_ref[...])

    @pl.when(f == nf - 1)
    def _():
        o_ref[...] = _layer_norm(DN_ALPHA * x_ref[...] + o_ref[...], g_ref[...], b_ref[...])


def _ffn(x2d, wup_t, conv_w, conv_b, wdown_bf, g, b, seq, *, tm=1024):
    m, d = x2d.shape
    fc = wup_t.shape[2]
    nf = wup_t.shape[0] // 2
    per_b = seq // tm
    halo_per_tile = tm // FFN_HALO
    return pl.pallas_call(
        functools.partial(_ffn_kernel, tm=tm, per_b=per_b),
        out_shape=jax.ShapeDtypeStruct((m, d), F32),
        grid=(m // tm, nf),
        in_specs=[
            pl.BlockSpec((tm, d), lambda i, f: (i, 0)),
            pl.BlockSpec((FFN_HALO, d), lambda i, f: (jnp.maximum(i * halo_per_tile - 1, 0), 0)),
            pl.BlockSpec((1, d, fc), lambda i, f: (f, 0, 0)),
            pl.BlockSpec((1, d, fc), lambda i, f: (nf + f, 0, 0)),
            pl.BlockSpec((CONV_W, fc), lambda i, f: (0, f)),
            pl.BlockSpec((1, fc), lambda i, f: (0, f)),
            pl.BlockSpec((fc, d), lambda i, f: (f, 0)),
            pl.BlockSpec((1, d), lambda i, f: (0, 0)),
            pl.BlockSpec((1, d), lambda i, f: (0, 0)),
        ],
        out_specs=pl.BlockSpec((tm, d), lambda i, f: (i, 0)),
        scratch_shapes=[pltpu.VMEM((tm + FFN_HALO, d), BF16)],
        compiler_params=_cparams(("parallel", "arbitrary")),
        name="conv_ffn_ln",
    )(x2d, x2d, wup_t, wup_t, conv_w, conv_b, wdown_bf, g, b)


def _rope_tables(seq):
    half = RET_D // 2
    inv = 1.0 / (ROPE_BASE ** (jnp.arange(half, dtype=F32) / half))
    ang = jnp.arange(seq).astype(F32)[:, None] * inv[None, :]
    cos, sin = jnp.cos(ang), jnp.sin(ang)
    return jnp.concatenate([cos, cos], axis=-1), jnp.concatenate([-sin, sin], axis=-1)


def _retention_tables():
    c = RET_CHUNK
    log_gamma = jnp.log1p(-jnp.exp2(-5.0 - jnp.arange(RET_HEADS, dtype=F32)))
    idx = jnp.arange(c, dtype=F32)
    diff = idx[:, None] - idx[None, :]
    dmat = jnp.where(diff[None] >= 0,
                     jnp.exp(log_gamma[:, None, None] * jnp.maximum(diff, 0.0)[None]), 0.0)
    kdec = jnp.exp(log_gamma[:, None] * (c - 1 - idx)[None, :])
    qdec = jnp.exp(log_gamma[:, None] * (idx + 1.0)[None, :])
    cdec = jnp.exp(log_gamma * c)
    bc = lambda t: jnp.broadcast_to(t[:, :, None], (RET_HEADS, c, RET_D))
    return dmat, bc(kdec), bc(qdec), cdec


def _t5_bucket_table():
    l = SWA_BLOCK
    dist = jnp.maximum(jnp.arange(l)[:, None] + l - jnp.arange(2 * l)[None, :], 0)
    max_exact = REL_BUCKETS // 2
    nf = jnp.maximum(dist, 1).astype(F32)
    large = max_exact + (jnp.log(nf / max_exact) / math.log(REL_MAX_DIST / max_exact)
                         * (REL_BUCKETS - max_exact)).astype(jnp.int32)
    large = jnp.minimum(large, REL_BUCKETS - 1)
    return jnp.where(dist < max_exact, dist, large).astype(jnp.int32)


def kernel(x, mem, w_in, ret_gn_g, swa_sinks, rel_bias, w_o, ln1_g, ln1_b, xa_wq, xa_wkv, xa_wo,
           ln2_g, ln2_b, ffn_w_up, ffn_conv_w, ffn_conv_b, ffn_w_down, ln3_g, ln3_b):
    batch, seq, d = x.shape
    mem_len = mem.shape[1]
    assert w_in.shape[0] == DEPTH and w_in.shape[2] == IN_W
    assert seq % 1024 == 0 and d % LANES == 0

    cosf, sinf = _rope_tables(seq)
    dmat, kdec, qdec, cdec = _retention_tables()
    bucket = _t5_bucket_table()
    bias = _bias_table(rel_bias, bucket)
    row = lambda t: t.reshape(1, -1)

    x2d = x.reshape(batch * seq, d)
    mem_bf = mem.reshape(batch * mem_len, d).astype(BF16)
    for l in range(DEPTH):
        proj = _in_proj(x2d, w_in[l].astype(BF16), cosf, sinf, seq)
        o_r = _retention(proj, cdec, dmat, kdec, qdec, row(ret_gn_g[l]), batch, seq)
        o_s = _swa(proj, swa_sinks[l], bias, batch, seq)
        x2d = _proj_res_ln(o_r, o_s, 0, 0, w_o[l].astype(BF16), x2d, row(ln1_g[l]), row(ln1_b[l]))

        kv = _matmul(mem_bf, xa_wkv[l].astype(BF16), tm=batch * mem_len, tn=1024)
        xa = _xattn(x2d, xa_wq[l].astype(BF16), kv, seq, mem_len)
        x2d = _proj_res_ln(xa, xa, 0, 1, xa_wo[l].astype(BF16), x2d, row(ln2_g[l]), row(ln2_b[l]))

        wup_t = ffn_w_up[l].astype(BF16).reshape(d, -1, FFN_CHUNK).transpose(1, 0, 2)
        x2d = _ffn(x2d, wup_t, ffn_conv_w[l], row(ffn_conv_b[l]),
                   ffn_w_down[l].astype(BF16), row(ln3_g[l]), row(ln3_b[l]), seq)
    return x2d.reshape(batch, seq, d)
```

```python
import functools
import math

import jax
import jax.numpy as jnp
import numpy as np
from jax import lax
from jax.experimental import pallas as pl
from jax.experimental.pallas import tpu as pltpu

RET_HEADS = 8
RET_D = 128
RET_CHUNK = 128
ROPE_BASE = 10000.0
SWA_HEADS = 16
SWA_KV_HEADS = 4
SWA_HD = 64
SWA_WINDOW = 128
SWA_BLOCK = 128
REL_BUCKETS = 32
REL_MAX_DIST = 128
XA_HEADS = 4
CONV_W = 3
LN_EPS = 1e-5
DEPTH = 1
DN_ALPHA = (2 * DEPTH) ** 0.25

RET_W = RET_HEADS * RET_D
SWA_W = SWA_HEADS * SWA_HD
SWA_KV_W = SWA_KV_HEADS * SWA_HD
OFF_RQ, OFF_RK, OFF_RV, OFF_RG = 0, RET_W, 2 * RET_W, 3 * RET_W
OFF_SQ = 4 * RET_W
OFF_SK = OFF_SQ + SWA_W
OFF_SV = OFF_SK + SWA_KV_W
IN_W = OFF_SV + SWA_KV_W

LANES = 128
MXU_COLS = 256
V7X_VMEM_BYTES = 64 * 1024 * 1024
VMEM_LIMIT = 56 * 1024 * 1024

BF16 = jnp.bfloat16
F32 = jnp.float32


def _cparams(sem, vmem=VMEM_LIMIT):
    return pltpu.CompilerParams(dimension_semantics=sem, vmem_limit_bytes=vmem)


def _layer_norm(y, g, b):
    mu = jnp.mean(y, axis=-1, keepdims=True)
    d = y - mu
    var = jnp.mean(d * d, axis=-1, keepdims=True)
    return d * lax.rsqrt(var + LN_EPS) * g + b


def _dot(a, b):
    return jnp.dot(a, b, preferred_element_type=F32)


def _dot_nt(a, b):
    return lax.dot_general(a, b, (((1,), (1,)), ((), ())), preferred_element_type=F32)


def _dot_tn(a, b):
    return lax.dot_general(a, b, (((0,), (0,)), ((), ())), preferred_element_type=F32)


IN_PROJ_ROW_SPLITS = 4


def _in_proj_kernel(x_ref, w_ref, cos_ref, sin_ref, o_ref, xb_ref, *, tn):
    j = pl.program_id(1)

    @pl.when(j == 0)
    def _():
        xb_ref[...] = x_ref[...].astype(BF16)

    n_q = OFF_RK // tn
    n_rope = OFF_RV // tn
    scale = jnp.where((j >= n_q) & (j < n_rope), RET_D ** -0.5, 1.0).astype(F32)
    tm = x_ref.shape[0]
    half = tm // IN_PROJ_ROW_SPLITS
    for mh in range(IN_PROJ_ROW_SPLITS):
        rows = slice(mh * half, (mh + 1) * half)
        cos = cos_ref[rows, :]
        sin = sin_ref[rows, :]
        for s in range(tn // MXU_COLS):
            acc = _dot(xb_ref[rows, :], w_ref[:, s * MXU_COLS:(s + 1) * MXU_COLS])
            for c in range(MXU_COLS // RET_D):
                a = acc[:, c * RET_D:(c + 1) * RET_D]
                r = pltpu.roll(a, RET_D // 2, 1)
                lo = s * MXU_COLS + c * RET_D
                o_ref[rows, lo:lo + RET_D] = ((a * cos + r * sin) * scale).astype(o_ref.dtype)


def _in_proj(x2d, w_bf, cosf, sinf, seq, *, tm=1024, tn=512):
    m, d = x2d.shape
    n = w_bf.shape[1]
    pos_blocks = seq // tm
    n_rope = OFF_RV // tn
    cosf = jnp.concatenate([cosf, jnp.ones((tm, RET_D), F32)], axis=0)
    sinf = jnp.concatenate([sinf, jnp.zeros((tm, RET_D), F32)], axis=0)
    tbl_idx = lambda i, j: (jnp.where(j < n_rope, i % pos_blocks, pos_blocks), 0)
    return pl.pallas_call(
        functools.partial(_in_proj_kernel, tn=tn),
        out_shape=jax.ShapeDtypeStruct((m, n), BF16),
        grid=(m // tm, n // tn),
        in_specs=[
            pl.BlockSpec((tm, d), lambda i, j: (i, 0)),
            pl.BlockSpec((d, tn), lambda i, j: (0, j)),
            pl.BlockSpec((tm, RET_D), tbl_idx),
            pl.BlockSpec((tm, RET_D), tbl_idx),
        ],
        out_specs=pl.BlockSpec((tm, tn), lambda i, j: (i, j)),
        scratch_shapes=[pltpu.VMEM((tm, d), BF16)],
        compiler_params=_cparams(("parallel", "arbitrary")),
        name="in_proj_rope",
    )(x2d, w_bf, cosf, sinf)


def _retention_kernel(cdec_ref, q_ref, k_ref, v_ref, g_ref, dmat_ref, kdec_ref, qdec_ref,
                      gn_ref, o_ref, state_ref, *, chunks):
    h = pl.program_id(1)
    n = pl.program_id(2)
    c_sz = RET_CHUNK

    @pl.when(n == 0)
    def _():
        state_ref[...] = jnp.zeros_like(state_ref)

    dmat = dmat_ref[0]
    kdec = kdec_ref[0]
    qdec = qdec_ref[0]
    cdec = cdec_ref[h]
    gn = gn_ref[...]
    rows_of = lambda c: slice(c * c_sz, (c + 1) * c_sz)
    scores = [(_dot_nt(q_ref[rows_of(c), :], k_ref[rows_of(c), :]) * dmat).astype(BF16)
              for c in range(chunks)]
    kvs = [_dot_tn((k_ref[rows_of(c), :].astype(F32) * kdec).astype(BF16), v_ref[rows_of(c), :])
           for c in range(chunks)]
    state = state_ref[...]
    prevs = []
    for c in range(chunks):
        prevs.append(state.astype(BF16))
        state = state * cdec + kvs[c]
    state_ref[...] = state
    for c in range(chunks):
        rows = rows_of(c)
        o = _dot(scores[c], v_ref[rows, :]) + _dot(q_ref[rows, :], prevs[c]) * qdec
        mu = jnp.mean(o, axis=-1, keepdims=True)
        dlt = o - mu
        var = jnp.mean(dlt * dlt, axis=-1, keepdims=True)
        on = dlt * lax.rsqrt(var + LN_EPS) * gn
        g = g_ref[rows, :].astype(F32)
        o_ref[rows, :] = (g * jax.nn.sigmoid(g) * on).astype(o_ref.dtype)


def _retention(proj, cdec, dmat, kdec, qdec, gn, batch, seq, *, rows=1024):
    m = proj.shape[0]
    nblk = seq // rows
    col = lambda off: off // RET_D

    def spec(off):
        return pl.BlockSpec((rows, RET_D), lambda b, h, n: (b * nblk + n, col(off) + h))

    hspec = lambda shape: pl.BlockSpec((1,) + shape, lambda b, h, n: (h, 0, 0))
    return pl.pallas_call(
        functools.partial(_retention_kernel, chunks=rows // RET_CHUNK),
        out_shape=jax.ShapeDtypeStruct((m, RET_W), BF16),
        grid=(batch, RET_HEADS, nblk),
        in_specs=[
            pl.BlockSpec(memory_space=pltpu.SMEM),
            spec(OFF_RQ), spec(OFF_RK), spec(OFF_RV), spec(OFF_RG),
            hspec((RET_CHUNK, RET_CHUNK)), hspec((RET_CHUNK, RET_D)), hspec((RET_CHUNK, RET_D)),
            pl.BlockSpec((1, RET_D), lambda b, h, n: (0, h)),
        ],
        out_specs=pl.BlockSpec((rows, RET_D), lambda b, h, n: (b * nblk + n, h)),
        scratch_shapes=[pltpu.VMEM((RET_D, RET_D), F32)],
        compiler_params=_cparams(("parallel", "parallel", "arbitrary")),
        name="retention",
    )(cdec, proj, proj, proj, proj, dmat, kdec, qdec, gn)


def _bias_kernel(rel_ref, bucket_ref, o_ref):
    h = pl.program_id(0)
    bucket = bucket_ref[...]
    l = bucket.shape[0]

    def body(b, acc):
        return jnp.where(bucket == b, rel_ref[b, h], acc)

    bias = lax.fori_loop(0, REL_BUCKETS, body, jnp.zeros(bucket.shape, F32))
    i = lax.broadcasted_iota(jnp.int32, bucket.shape, 0)
    j = lax.broadcasted_iota(jnp.int32, bucket.shape, 1)
    dist = i + l - j
    o_ref[0] = jnp.where((dist >= 0) & (dist < SWA_WINDOW), bias, -jnp.inf)


def _bias_table(rel_bias, bucket):
    l, l2 = bucket.shape
    return pl.pallas_call(
        _bias_kernel,
        out_shape=jax.ShapeDtypeStruct((SWA_HEADS, l, l2), F32),
        grid=(SWA_HEADS,),
        in_specs=[pl.BlockSpec(memory_space=pltpu.SMEM),
                  pl.BlockSpec((l, l2), lambda h: (0, 0))],
        out_specs=pl.BlockSpec((1, l, l2), lambda h: (h, 0, 0)),
        compiler_params=pltpu.CompilerParams(dimension_semantics=("arbitrary",)),
        name="t5_bias_table",
    )(rel_bias, bucket)


SWA_QK_AHEAD = 2


def _swap_lane_halves(x):
    u = pltpu.bitcast(x, jnp.uint32)
    return pltpu.bitcast(pltpu.roll(u, LANES // 2, 1), x.dtype)


def _swa_kernel(sink_ref, q_ref, kc_ref, vc_ref, kp_ref, vp_ref, bias_ref, first_ref, o_ref, *,
                blocks):
    l = SWA_BLOCK
    grp = SWA_HEADS // SWA_KV_HEADS
    low = lax.broadcasted_iota(jnp.int32, (2 * l, LANES), 1) < SWA_HD
    low_q = lax.broadcasted_iota(jnp.int32, (l, LANES), 1) < SWA_HD
    zero = jnp.zeros((2 * l, LANES), BF16)
    kv2 = {}

    def kv_pair(blk, kh):
        if (blk, kh) in kv2:
            return kv2[(blk, kh)]
        r0 = blk * l
        t, sub = divmod(kh, LANES // SWA_HD)
        cols = slice(t * LANES, (t + 1) * LANES)
        if blk == 0:
            kt = jnp.concatenate([kp_ref[:, cols], kc_ref[0:l, cols]], axis=0)
            vt = jnp.concatenate([vp_ref[:, cols], vc_ref[0:l, cols]], axis=0)
        else:
            kt = kc_ref[r0 - l:r0 + l, cols]
            vt = vc_ref[r0 - l:r0 + l, cols]
        kt = kt * (SWA_HD ** -0.5)
        kr = _swap_lane_halves(kt)
        vr = _swap_lane_halves(vt)
        k_lo, k_hi = (kt, kr) if sub == 0 else (kr, kt)
        v_lo, v_hi = (vt, vr) if sub == 0 else (vr, vt)
        k2 = jnp.concatenate([jnp.where(low, k_lo, zero), jnp.where(low, zero, k_hi)], axis=0)
        v2 = jnp.concatenate([jnp.where(low, v_lo, zero), jnp.where(low, zero, v_hi)], axis=0)
        kv2[(blk, kh)] = (k2, v2)
        return k2, v2

    tasks = [(blk, pair) for blk in range(blocks) for pair in range(SWA_HEADS // 2)]

    def qk(task):
        blk, pair = task
        k2, _ = kv_pair(blk, pair // (grp // 2))
        q2 = q_ref[blk * l:(blk + 1) * l, pair * LANES:(pair + 1) * LANES]
        return _dot_nt(q2, k2)

    def finish(task, logits2):
        blk, pair = task
        _, v2 = kv_pair(blk, pair // (grp // 2))
        ps, dens = [], []
        for s in range(2):
            hd = 2 * pair + s
            lg = logits2[:, s * 2 * l:(s + 1) * 2 * l] + bias_ref[hd]
            if blk == 0:
                lg = lg + first_ref[0]
            sink = sink_ref[hd]
            mx = jnp.maximum(jnp.max(lg, axis=-1, keepdims=True), sink)
            p = jnp.exp(lg - mx)
            dens.append(jnp.sum(p, axis=-1, keepdims=True) + jnp.exp(sink - mx))
            ps.append(p.astype(BF16))
        out2 = _dot(jnp.concatenate(ps, axis=1), v2)
        den2 = jnp.where(low_q, dens[0], dens[1])
        o_ref[blk * l:(blk + 1) * l, pair * LANES:(pair + 1) * LANES] = (out2 / den2).astype(o_ref.dtype)

    pending = [qk(t) for t in tasks[:SWA_QK_AHEAD]]
    for n, task in enumerate(tasks):
        if n + SWA_QK_AHEAD < len(tasks):
            pending.append(qk(tasks[n + SWA_QK_AHEAD]))
        finish(task, pending.pop(0))


def _swa(proj, sinks, bias, batch, seq, *, rows=256):
    m = proj.shape[0]
    nblk = seq // rows
    per = rows // SWA_BLOCK
    kcol, vcol = OFF_SK // SWA_KV_W, OFF_SV // SWA_KV_W
    l = SWA_BLOCK
    first = jnp.stack([jnp.zeros((l, 2 * l), F32),
                       jnp.where(jnp.arange(2 * l)[None, :] < l, -jnp.inf, 0.0)
                       * jnp.ones((l, 1), F32)])

    def prev_idx(b, n):
        return jnp.maximum((b * nblk + n) * per - 1, 0)

    return pl.pallas_call(
        functools.partial(_swa_kernel, blocks=per),
        out_shape=jax.ShapeDtypeStruct((m, SWA_W), BF16),
        grid=(batch, nblk),
        in_specs=[
            pl.BlockSpec(memory_space=pltpu.SMEM),
            pl.BlockSpec((rows, SWA_W), lambda b, n: (b * nblk + n, OFF_SQ // SWA_W)),
            pl.BlockSpec((rows, SWA_KV_W), lambda b, n: (b * nblk + n, kcol)),
            pl.BlockSpec((rows, SWA_KV_W), lambda b, n: (b * nblk + n, vcol)),
            pl.BlockSpec((SWA_BLOCK, SWA_KV_W), lambda b, n: (prev_idx(b, n), kcol)),
            pl.BlockSpec((SWA_BLOCK, SWA_KV_W), lambda b, n: (prev_idx(b, n), vcol)),
            pl.BlockSpec((SWA_HEADS, SWA_BLOCK, 2 * SWA_BLOCK), lambda b, n: (0, 0, 0)),
            pl.BlockSpec((1, SWA_BLOCK, 2 * SWA_BLOCK), lambda b, n: (jnp.where(n == 0, 1, 0), 0, 0)),
        ],
        out_specs=pl.BlockSpec((rows, SWA_W), lambda b, n: (b * nblk + n, 0)),
        compiler_params=_cparams(("parallel", "arbitrary")),
        name="swa_sink_attention",
    )(sinks, proj, proj, proj, proj, proj, bias, first)


def _proj_res_ln_kernel(a1_ref, a2_ref, w1_ref, w2_ref, x_ref, g_ref, b_ref, o_ref):
    y = _dot(a1_ref[...], w1_ref[...]) + _dot(a2_ref[...], w2_ref[...])
    o_ref[...] = _layer_norm(DN_ALPHA * x_ref[...] + y, g_ref[...], b_ref[...])


def _proj_res_ln(a1, a2, col1, col2, w_bf, x2d, g, b, *, tm=512):
    m, d = x2d.shape
    kh = w_bf.shape[0] // 2
    return pl.pallas_call(
        _proj_res_ln_kernel,
        out_shape=jax.ShapeDtypeStruct((m, d), F32),
        grid=(m // tm,),
        in_specs=[
            pl.BlockSpec((tm, kh), lambda i: (i, col1)),
            pl.BlockSpec((tm, kh), lambda i: (i, col2)),
            pl.BlockSpec((kh, d), lambda i: (0, 0)),
            pl.BlockSpec((kh, d), lambda i: (1, 0)),
            pl.BlockSpec((tm, d), lambda i: (i, 0)),
            pl.BlockSpec((1, d), lambda i: (0, 0)),
            pl.BlockSpec((1, d), lambda i: (0, 0)),
        ],
        out_specs=pl.BlockSpec((tm, d), lambda i: (i, 0)),
        compiler_params=_cparams(("parallel",)),
        name="proj_residual_ln",
    )(a1, a2, w_bf, w_bf, x2d, g, b)


def _matmul_kernel(a_ref, w_ref, o_ref):
    o_ref[...] = _dot(a_ref[...], w_ref[...]).astype(o_ref.dtype)


def _matmul(a_bf, w_bf, *, tm, tn):
    m, k = a_bf.shape
    n = w_bf.shape[1]
    return pl.pallas_call(
        _matmul_kernel,
        out_shape=jax.ShapeDtypeStruct((m, n), BF16),
        grid=(m // tm, n // tn),
        in_specs=[pl.BlockSpec((tm, k), lambda i, j: (i, 0)),
                  pl.BlockSpec((k, tn), lambda i, j: (0, j))],
        out_specs=pl.BlockSpec((tm, tn), lambda i, j: (i, j)),
        compiler_params=_cparams(("parallel", "arbitrary")),
        name="matmul",
    )(a_bf, w_bf)


def _xattn_kernel(x_ref, wq_ref, k_ref, v_ref, o_ref, xb_ref, *, hd):
    h = pl.program_id(1)

    @pl.when(h == 0)
    def _():
        xb_ref[...] = x_ref[...].astype(BF16)

    q = _dot(xb_ref[...], wq_ref[...])
    logits = _dot_nt(q.astype(BF16), k_ref[...]) * (hd ** -0.5)
    mx = jnp.max(logits, axis=-1, keepdims=True)
    p = jnp.exp(logits - mx)
    den = jnp.sum(p, axis=-1, keepdims=True)
    o_ref[...] = (_dot(p.astype(BF16), v_ref[...]) / den).astype(o_ref.dtype)


def _xattn(x2d, wq_bf, kv, seq, mem_len, *, tm=1024):
    m, d = x2d.shape
    hd = d // XA_HEADS
    per_b = seq // tm
    return pl.pallas_call(
        functools.partial(_xattn_kernel, hd=hd),
        out_shape=jax.ShapeDtypeStruct((m, d), BF16),
        grid=(m // tm, XA_HEADS),
        in_specs=[
            pl.BlockSpec((tm, d), lambda i, h: (i, 0)),
            pl.BlockSpec((d, hd), lambda i, h: (0, h)),
            pl.BlockSpec((mem_len, hd), lambda i, h: (i // per_b, h)),
            pl.BlockSpec((mem_len, hd), lambda i, h: (i // per_b, XA_HEADS + h)),
        ],
        out_specs=pl.BlockSpec((tm, hd), lambda i, h: (i, h)),
        scratch_shapes=[pltpu.VMEM((tm, d), BF16)],
        compiler_params=_cparams(("parallel", "arbitrary")),
        name="memory_cross_attention",
    )(x2d, wq_bf, kv, kv)


FFN_HALO = 16
FFN_GROUP_ROWS = 256
FFN_CHUNK = 256


def _ffn_kernel(x_ref, xh_ref, wu_ref, wg_ref, cw_ref, cb_ref, wd_ref, g_ref, b_ref, o_ref,
                xb_ref, *, tm, per_b):
    i = pl.program_id(0)
    f = pl.program_id(1)
    nf = pl.num_programs(1)

    @pl.when(f == 0)
    def _():
        halo = jnp.where(i % per_b == 0, 0.0, xh_ref[...])
        xb_ref[0:FFN_HALO, :] = halo.astype(BF16)
        xb_ref[FFN_HALO:, :] = x_ref[...].astype(BF16)
        o_ref[...] = jnp.zeros_like(o_ref)

    cw = cw_ref[...]
    cb = cb_ref[...]
    def up(r):
        r0 = r * FFN_GROUP_ROWS
        u = _dot(xb_ref[FFN_HALO + r0:FFN_HALO + r0 + FFN_GROUP_ROWS, :], wu_ref[0])
        ge = _dot(xb_ref[r0:r0 + FFN_GROUP_ROWS + FFN_HALO, :], wg_ref[0])
        return u, ge

    groups = tm // FFN_GROUP_ROWS
    nxt = up(0)
    for r in range(groups):
        u, ge = nxt
        if r + 1 < groups:
            nxt = up(r + 1)
        gc = cb
        for tap in range(CONV_W):
            lo = FFN_HALO - (CONV_W - 1) + tap
            gc = gc + ge[lo:lo + FFN_GROUP_ROWS, :] * cw[tap:tap + 1, :]
        hcur = (gc * jax.nn.sigmoid(gc) * u).astype(BF16)
        r0 = r * FFN_GROUP_ROWS
        o_ref[r0:r0 + FFN_GROUP_ROWS, :] += _dot(hcur, wd_ref[...])

    @pl.when(f == nf - 1)
    def _():
        o_ref[...] = _layer_norm(DN_ALPHA * x_ref[...] + o_ref[...], g_ref[...], b_ref[...])


def _ffn(x2d, wup_t, conv_w, conv_b, wdown_bf, g, b, seq, *, tm=1024):
    m, d = x2d.shape
    fc = wup_t.shape[2]
    nf = wup_t.shape[0] // 2
    per_b = seq // tm
    halo_per_tile = tm // FFN_HALO
    return pl.pallas_call(
        functools.partial(_ffn_kernel, tm=tm, per_b=per_b),
        out_shape=jax.ShapeDtypeStruct((m, d), F32),
        grid=(m // tm, nf),
        in_specs=[
            pl.BlockSpec((tm, d), lambda i, f: (i, 0)),
            pl.BlockSpec((FFN_HALO, d), lambda i, f: (jnp.maximum(i * halo_per_tile - 1, 0), 0)),
            pl.BlockSpec((1, d, fc), lambda i, f: (f, 0, 0)),
            pl.BlockSpec((1, d, fc), lambda i, f: (nf + f, 0, 0)),
            pl.BlockSpec((CONV_W, fc), lambda i, f: (0, f)),
            pl.BlockSpec((1, fc), lambda i, f: (0, f)),
            pl.BlockSpec((fc, d), lambda i, f: (f, 0)),
            pl.BlockSpec((1, d), lambda i, f: (0, 0)),
            pl.BlockSpec((1, d), lambda i, f: (0, 0)),
        ],
        out_specs=pl.BlockSpec((tm, d), lambda i, f: (i, 0)),
        scratch_shapes=[pltpu.VMEM((tm + FFN_HALO, d), BF16)],
        compiler_params=_cparams(("parallel", "arbitrary")),
        name="conv_ffn_ln",
    )(x2d, x2d, wup_t, wup_t, conv_w, conv_b, wdown_bf, g, b)


def _rope_tables(seq):
    half = RET_D // 2
    inv = 1.0 / (ROPE_BASE ** (jnp.arange(half, dtype=F32) / half))
    ang = jnp.arange(seq).astype(F32)[:, None] * inv[None, :]
    cos, sin = jnp.cos(ang), jnp.sin(ang)
    return jnp.concatenate([cos, cos], axis=-1), jnp.concatenate([-sin, sin], axis=-1)


def _retention_tables():
    c = RET_CHUNK
    log_gamma = jnp.log1p(-jnp.exp2(-5.0 - jnp.arange(RET_HEADS, dtype=F32)))
    idx = jnp.arange(c, dtype=F32)
    diff = idx[:, None] - idx[None, :]
    dmat = jnp.where(diff[None] >= 0,
                     jnp.exp(log_gamma[:, None, None] * jnp.maximum(diff, 0.0)[None]), 0.0)
    kdec = jnp.exp(log_gamma[:, None] * (c - 1 - idx)[None, :])
    qdec = jnp.exp(log_gamma[:, None] * (idx + 1.0)[None, :])
    cdec = jnp.exp(log_gamma * c)
    bc = lambda t: jnp.broadcast_to(t[:, :, None], (RET_HEADS, c, RET_D))
    return dmat, bc(kdec), bc(qdec), cdec


def _t5_bucket_table():
    l = SWA_BLOCK
    dist = jnp.maximum(jnp.arange(l)[:, None] + l - jnp.arange(2 * l)[None, :], 0)
    max_exact = REL_BUCKETS // 2
    nf = jnp.maximum(dist, 1).astype(F32)
    large = max_exact + (jnp.log(nf / max_exact) / math.log(REL_MAX_DIST / max_exact)
                         * (REL_BUCKETS - max_exact)).astype(jnp.int32)
    large = jnp.minimum(large, REL_BUCKETS - 1)
    return jnp.where(dist < max_exact, dist, large).astype(jnp.int32)


def kernel(x, mem, w_in, ret_gn_g, swa_sinks, rel_bias, w_o, ln1_g, ln1_b, xa_wq, xa_wkv, xa_wo,
           ln2_g, ln2_b, ffn_w_up, ffn_conv_w, ffn_conv_b, ffn_w_down, ln3_g, ln3_b):
    batch, seq, d = x.shape
    mem_len = mem.shape[1]
    assert w_in.shape[0] == DEPTH and w_in.shape[2] == IN_W
    assert seq % 1024 == 0 and d % LANES == 0

    cosf, sinf = _rope_tables(seq)
    dmat, kdec, qdec, cdec = _retention_tables()
    bucket = _t5_bucket_table()
    bias = _bias_table(rel_bias, bucket)
    row = lambda t: t.reshape(1, -1)

    x2d = x.reshape(batch * seq, d)
    mem_bf = mem.reshape(batch * mem_len, d).astype(BF16)
    for l in range(DEPTH):
        proj = _in_proj(x2d, w_in[l].astype(BF16), cosf, sinf, seq)
        o_r = _retention(proj, cdec, dmat, kdec, qdec, row(ret_gn_g[l]), batch, seq)
        o_s = _swa(proj, swa_sinks[l], bias, batch, seq)
        x2d = _proj_res_ln(o_r, o_s, 0, 0, w_o[l].astype(BF16), x2d, row(ln1_g[l]), row(ln1_b[l]))

        kv = _matmul(mem_bf, xa_wkv[l].astype(BF16), tm=batch * mem_len, tn=1024)
        xa = _xattn(x2d, xa_wq[l].astype(BF16), kv, seq, mem_len)
        x2d = _proj_res_ln(xa, xa, 0, 1, xa_wo[l].astype(BF16), x2d, row(ln2_g[l]), row(ln2_b[l]))

        wup_t = ffn_w_up[l].astype(BF16).reshape(d, -1, FFN_CHUNK).transpose(1, 0, 2)
        x2d = _ffn(x2d, wup_t, ffn_conv_w[l], row(ffn_conv_b[l]),
                   ffn_w_down[l].astype(BF16), row(ln3_g[l]), row(ln3_b[l]), seq)
    return x2d.reshape(batch, seq, d)
```

```python
import functools
import math

import jax
import jax.numpy as jnp
import numpy as np
from jax import lax
from jax.experimental import pallas as pl
from jax.experimental.pallas import tpu as pltpu

RET_HEADS = 8
RET_D = 128
RET_CHUNK = 128
ROPE_BASE = 10000.0
SWA_HEADS = 16
SWA_KV_HEADS = 4
SWA_HD = 64
SWA_WINDOW = 128
SWA_BLOCK = 128
REL_BUCKETS = 32
REL_MAX_DIST = 128
XA_HEADS = 4
CONV_W = 3
LN_EPS = 1e-5
DEPTH = 1
DN_ALPHA = (2 * DEPTH) ** 0.25

RET_W = RET_HEADS * RET_D
SWA_W = SWA_HEADS * SWA_HD
SWA_KV_W = SWA_KV_HEADS * SWA_HD
OFF_RQ, OFF_RK, OFF_RV, OFF_RG = 0, RET_W, 2 * RET_W, 3 * RET_W
OFF_SQ = 4 * RET_W
OFF_SK = OFF_SQ + SWA_W
OFF_SV = OFF_SK + SWA_KV_W
IN_W = OFF_SV + SWA_KV_W

LANES = 128
MXU_COLS = 256
V7X_VMEM_BYTES = 64 * 1024 * 1024
VMEM_LIMIT = 56 * 1024 * 1024

BF16 = jnp.bfloat16
F32 = jnp.float32


def _cparams(sem, vmem=VMEM_LIMIT):
    return pltpu.CompilerParams(dimension_semantics=sem, vmem_limit_bytes=vmem)


def _layer_norm(y, g, b):
    mu = jnp.mean(y, axis=-1, keepdims=True)
    d = y - mu
    var = jnp.mean(d * d, axis=-1, keepdims=True)
    return d * lax.rsqrt(var + LN_EPS) * g + b


def _dot(a, b):
    return jnp.dot(a, b, preferred_element_type=F32)


def _dot_nt(a, b):
    return lax.dot_general(a, b, (((1,), (1,)), ((), ())), preferred_element_type=F32)


def _dot_tn(a, b):
    return lax.dot_general(a, b, (((0,), (0,)), ((), ())), preferred_element_type=F32)


IN_PROJ_GROUP_ROWS = 256


def _in_proj_kernel(x_ref, w_ref, cos_ref, sin_ref, o_ref, xb_ref, *, tn):
    j = pl.program_id(1)

    @pl.when(j == 0)
    def _():
        xb_ref[...] = x_ref[...].astype(BF16)

    n_q = OFF_RK // tn
    n_rope = OFF_RV // tn
    scale = jnp.where((j >= n_q) & (j < n_rope), RET_D ** -0.5, 1.0).astype(F32)
    tm = x_ref.shape[0]
    half = IN_PROJ_GROUP_ROWS
    for mh in range(tm // IN_PROJ_GROUP_ROWS):
        rows = slice(mh * half, (mh + 1) * half)
        cos = cos_ref[rows, :]
        sin = sin_ref[rows, :]
        for s in range(tn // MXU_COLS):
            acc = _dot(xb_ref[rows, :], w_ref[:, s * MXU_COLS:(s + 1) * MXU_COLS])
            for c in range(MXU_COLS // RET_D):
                a = acc[:, c * RET_D:(c + 1) * RET_D]
                r = pltpu.roll(a, RET_D // 2, 1)
                lo = s * MXU_COLS + c * RET_D
                o_ref[rows, lo:lo + RET_D] = ((a * cos + r * sin) * scale).astype(o_ref.dtype)


def _in_proj(x2d, w_bf, cosf, sinf, seq, *, tm=2048, tn=512):
    m, d = x2d.shape
    n = w_bf.shape[1]
    pos_blocks = seq // tm
    n_rope = OFF_RV // tn
    cosf = jnp.concatenate([cosf, jnp.ones((tm, RET_D), F32)], axis=0)
    sinf = jnp.concatenate([sinf, jnp.zeros((tm, RET_D), F32)], axis=0)
    tbl_idx = lambda i, j: (jnp.where(j < n_rope, i % pos_blocks, pos_blocks), 0)
    return pl.pallas_call(
        functools.partial(_in_proj_kernel, tn=tn),
        out_shape=jax.ShapeDtypeStruct((m, n), BF16),
        grid=(m // tm, n // tn),
        in_specs=[
            pl.BlockSpec((tm, d), lambda i, j: (i, 0)),
            pl.BlockSpec((d, tn), lambda i, j: (0, j)),
            pl.BlockSpec((tm, RET_D), tbl_idx),
            pl.BlockSpec((tm, RET_D), tbl_idx),
        ],
        out_specs=pl.BlockSpec((tm, tn), lambda i, j: (i, j)),
        scratch_shapes=[pltpu.VMEM((tm, d), BF16)],
        compiler_params=_cparams(("parallel", "arbitrary")),
        name="in_proj_rope",
    )(x2d, w_bf, cosf, sinf)


def _retention_kernel(cdec_ref, q_ref, k_ref, v_ref, g_ref, dmat_ref, kdec_ref, qdec_ref,
                      gn_ref, o_ref, state_ref, *, chunks):
    h = pl.program_id(1)
    n = pl.program_id(2)
    c_sz = RET_CHUNK

    @pl.when(n == 0)
    def _():
        state_ref[...] = jnp.zeros_like(state_ref)

    dmat = dmat_ref[0]
    kdec = kdec_ref[0]
    qdec = qdec_ref[0]
    cdec = cdec_ref[h]
    gn = gn_ref[...]
    rows_of = lambda c: slice(c * c_sz, (c + 1) * c_sz)
    scores = [(_dot_nt(q_ref[rows_of(c), :], k_ref[rows_of(c), :]) * dmat).astype(BF16)
              for c in range(chunks)]
    kvs = [_dot_tn((k_ref[rows_of(c), :].astype(F32) * kdec).astype(BF16), v_ref[rows_of(c), :])
           for c in range(chunks)]
    state = state_ref[...]
    prevs = []
    for c in range(chunks):
        prevs.append(state.astype(BF16))
        state = state * cdec + kvs[c]
    state_ref[...] = state
    for c in range(chunks):
        rows = rows_of(c)
        o = _dot(scores[c], v_ref[rows, :]) + _dot(q_ref[rows, :], prevs[c]) * qdec
        mu = jnp.mean(o, axis=-1, keepdims=True)
        dlt = o - mu
        var = jnp.mean(dlt * dlt, axis=-1, keepdims=True)
        on = dlt * lax.rsqrt(var + LN_EPS) * gn
        g = g_ref[rows, :].astype(F32)
        o_ref[rows, :] = (g * jax.nn.sigmoid(g) * on).astype(o_ref.dtype)


def _retention(proj, cdec, dmat, kdec, qdec, gn, batch, seq, *, rows=1024):
    m = proj.shape[0]
    nblk = seq // rows
    col = lambda off: off // RET_D

    def spec(off):
        return pl.BlockSpec((rows, RET_D), lambda b, h, n: (b * nblk + n, col(off) + h))

    hspec = lambda shape: pl.BlockSpec((1,) + shape, lambda b, h, n: (h, 0, 0))
    return pl.pallas_call(
        functools.partial(_retention_kernel, chunks=rows // RET_CHUNK),
        out_shape=jax.ShapeDtypeStruct((m, RET_W), BF16),
        grid=(batch, RET_HEADS, nblk),
        in_specs=[
            pl.BlockSpec(memory_space=pltpu.SMEM),
            spec(OFF_RQ), spec(OFF_RK), spec(OFF_RV), spec(OFF_RG),
            hspec((RET_CHUNK, RET_CHUNK)), hspec((RET_CHUNK, RET_D)), hspec((RET_CHUNK, RET_D)),
            pl.BlockSpec((1, RET_D), lambda b, h, n: (0, h)),
        ],
        out_specs=pl.BlockSpec((rows, RET_D), lambda b, h, n: (b * nblk + n, h)),
        scratch_shapes=[pltpu.VMEM((RET_D, RET_D), F32)],
        compiler_params=_cparams(("parallel", "parallel", "arbitrary")),
        name="retention",
    )(cdec, proj, proj, proj, proj, dmat, kdec, qdec, gn)


def _bias_kernel(rel_ref, bucket_ref, o_ref):
    h = pl.program_id(0)
    bucket = bucket_ref[...]
    l = bucket.shape[0]

    def body(b, acc):
        return jnp.where(bucket == b, rel_ref[b, h], acc)

    bias = lax.fori_loop(0, REL_BUCKETS, body, jnp.zeros(bucket.shape, F32))
    i = lax.broadcasted_iota(jnp.int32, bucket.shape, 0)
    j = lax.broadcasted_iota(jnp.int32, bucket.shape, 1)
    dist = i + l - j
    o_ref[0] = jnp.where((dist >= 0) & (dist < SWA_WINDOW), bias, -jnp.inf)


def _bias_table(rel_bias, bucket):
    l, l2 = bucket.shape
    return pl.pallas_call(
        _bias_kernel,
        out_shape=jax.ShapeDtypeStruct((SWA_HEADS, l, l2), F32),
        grid=(SWA_HEADS,),
        in_specs=[pl.BlockSpec(memory_space=pltpu.SMEM),
                  pl.BlockSpec((l, l2), lambda h: (0, 0))],
        out_specs=pl.BlockSpec((1, l, l2), lambda h: (h, 0, 0)),
        compiler_params=pltpu.CompilerParams(dimension_semantics=("arbitrary",)),
        name="t5_bias_table",
    )(rel_bias, bucket)


SWA_QK_AHEAD = 2


def _swap_lane_halves(x):
    u = pltpu.bitcast(x, jnp.uint32)
    return pltpu.bitcast(pltpu.roll(u, LANES // 2, 1), x.dtype)


def _swa_kernel(sink_ref, q_ref, kc_ref, vc_ref, kp_ref, vp_ref, bias_ref, first_ref, o_ref, *,
                blocks):
    l = SWA_BLOCK
    grp = SWA_HEADS // SWA_KV_HEADS
    low = lax.broadcasted_iota(jnp.int32, (2 * l, LANES), 1) < SWA_HD
    low_q = lax.broadcasted_iota(jnp.int32, (l, LANES), 1) < SWA_HD
    zero = jnp.zeros((2 * l, LANES), BF16)
    kv2 = {}

    def kv_pair(blk, kh):
        if (blk, kh) in kv2:
            return kv2[(blk, kh)]
        r0 = blk * l
        t, sub = divmod(kh, LANES // SWA_HD)
        cols = slice(t * LANES, (t + 1) * LANES)
        if blk == 0:
            kt = jnp.concatenate([kp_ref[:, cols], kc_ref[0:l, cols]], axis=0)
            vt = jnp.concatenate([vp_ref[:, cols], vc_ref[0:l, cols]], axis=0)
        else:
            kt = kc_ref[r0 - l:r0 + l, cols]
            vt = vc_ref[r0 - l:r0 + l, cols]
        kt = kt * (SWA_HD ** -0.5)
        kr = _swap_lane_halves(kt)
        vr = _swap_lane_halves(vt)
        k_lo, k_hi = (kt, kr) if sub == 0 else (kr, kt)
        v_lo, v_hi = (vt, vr) if sub == 0 else (vr, vt)
        k2 = jnp.concatenate([jnp.where(low, k_lo, zero), jnp.where(low, zero, k_hi)], axis=0)
        v2 = jnp.concatenate([jnp.where(low, v_lo, zero), jnp.where(low, zero, v_hi)], axis=0)
        kv2[(blk, kh)] = (k2, v2)
        return k2, v2

    tasks = [(blk, pair) for blk in range(blocks) for pair in range(SWA_HEADS // 2)]

    def qk(task):
        blk, pair = task
        k2, _ = kv_pair(blk, pair // (grp // 2))
        q2 = q_ref[blk * l:(blk + 1) * l, pair * LANES:(pair + 1) * LANES]
        return _dot_nt(q2, k2)

    def finish(task, logits2):
        blk, pair = task
        _, v2 = kv_pair(blk, pair // (grp // 2))
        ps, dens = [], []
        for s in range(2):
            hd = 2 * pair + s
            lg = logits2[:, s * 2 * l:(s + 1) * 2 * l] + bias_ref[hd]
            if blk == 0:
                lg = lg + first_ref[0]
            sink = sink_ref[hd]
            mx = jnp.maximum(jnp.max(lg, axis=-1, keepdims=True), sink)
            p = jnp.exp(lg - mx)
            dens.append(jnp.sum(p, axis=-1, keepdims=True) + jnp.exp(sink - mx))
            ps.append(p.astype(BF16))
        out2 = _dot(jnp.concatenate(ps, axis=1), v2)
        den2 = jnp.where(low_q, dens[0], dens[1])
        o_ref[blk * l:(blk + 1) * l, pair * LANES:(pair + 1) * LANES] = (out2 / den2).astype(o_ref.dtype)

    pending = [qk(t) for t in tasks[:SWA_QK_AHEAD]]
    for n, task in enumerate(tasks):
        if n + SWA_QK_AHEAD < len(tasks):
            pending.append(qk(tasks[n + SWA_QK_AHEAD]))
        finish(task, pending.pop(0))


def _swa(proj, sinks, bias, batch, seq, *, rows=256):
    m = proj.shape[0]
    nblk = seq // rows
    per = rows // SWA_BLOCK
    kcol, vcol = OFF_SK // SWA_KV_W, OFF_SV // SWA_KV_W
    l = SWA_BLOCK
    first = jnp.stack([jnp.zeros((l, 2 * l), F32),
                       jnp.where(jnp.arange(2 * l)[None, :] < l, -jnp.inf, 0.0)
                       * jnp.ones((l, 1), F32)])

    def prev_idx(b, n):
        return jnp.maximum((b * nblk + n) * per - 1, 0)

    return pl.pallas_call(
        functools.partial(_swa_kernel, blocks=per),
        out_shape=jax.ShapeDtypeStruct((m, SWA_W), BF16),
        grid=(batch, nblk),
        in_specs=[
            pl.BlockSpec(memory_space=pltpu.SMEM),
            pl.BlockSpec((rows, SWA_W), lambda b, n: (b * nblk + n, OFF_SQ // SWA_W)),
            pl.BlockSpec((rows, SWA_KV_W), lambda b, n: (b * nblk + n, kcol)),
            pl.BlockSpec((rows, SWA_KV_W), lambda b, n: (b * nblk + n, vcol)),
            pl.BlockSpec((SWA_BLOCK, SWA_KV_W), lambda b, n: (prev_idx(b, n), kcol)),
            pl.BlockSpec((SWA_BLOCK, SWA_KV_W), lambda b, n: (prev_idx(b, n), vcol)),
            pl.BlockSpec((SWA_HEADS, SWA_BLOCK, 2 * SWA_BLOCK), lambda b, n: (0, 0, 0)),
            pl.BlockSpec((1, SWA_BLOCK, 2 * SWA_BLOCK), lambda b, n: (jnp.where(n == 0, 1, 0), 0, 0)),
        ],
        out_specs=pl.BlockSpec((rows, SWA_W), lambda b, n: (b * nblk + n, 0)),
        compiler_params=_cparams(("parallel", "arbitrary")),
        name="swa_sink_attention",
    )(sinks, proj, proj, proj, proj, proj, bias, first)


LN_GROUP_ROWS = 256


def _proj_res_ln_kernel(a1_ref, a2_ref, w1_ref, w2_ref, x_ref, g_ref, b_ref, o_ref):
    for r in range(x_ref.shape[0] // LN_GROUP_ROWS):
        rows = slice(r * LN_GROUP_ROWS, (r + 1) * LN_GROUP_ROWS)
        y = _dot(a1_ref[rows, :], w1_ref[...]) + _dot(a2_ref[rows, :], w2_ref[...])
        o_ref[rows, :] = _layer_norm(DN_ALPHA * x_ref[rows, :] + y, g_ref[...], b_ref[...])


def _proj_res_ln(a1, a2, col1, col2, w_bf, x2d, g, b, *, tm=1024):
    m, d = x2d.shape
    kh = w_bf.shape[0] // 2
    return pl.pallas_call(
        _proj_res_ln_kernel,
        out_shape=jax.ShapeDtypeStruct((m, d), F32),
        grid=(m // tm,),
        in_specs=[
            pl.BlockSpec((tm, kh), lambda i: (i, col1)),
            pl.BlockSpec((tm, kh), lambda i: (i, col2)),
            pl.BlockSpec((kh, d), lambda i: (0, 0), pipeline_mode=pl.Buffered(1)),
            pl.BlockSpec((kh, d), lambda i: (1, 0), pipeline_mode=pl.Buffered(1)),
            pl.BlockSpec((tm, d), lambda i: (i, 0)),
            pl.BlockSpec((1, d), lambda i: (0, 0)),
            pl.BlockSpec((1, d), lambda i: (0, 0)),
        ],
        out_specs=pl.BlockSpec((tm, d), lambda i: (i, 0)),
        compiler_params=_cparams(("parallel",)),
        name="proj_residual_ln",
    )(a1, a2, w_bf, w_bf, x2d, g, b)


def _matmul_kernel(a_ref, w_ref, o_ref):
    o_ref[...] = _dot(a_ref[...], w_ref[...]).astype(o_ref.dtype)


def _matmul(a_bf, w_bf, *, tm, tn):
    m, k = a_bf.shape
    n = w_bf.shape[1]
    return pl.pallas_call(
        _matmul_kernel,
        out_shape=jax.ShapeDtypeStruct((m, n), BF16),
        grid=(m // tm, n // tn),
        in_specs=[pl.BlockSpec((tm, k), lambda i, j: (i, 0)),
                  pl.BlockSpec((k, tn), lambda i, j: (0, j))],
        out_specs=pl.BlockSpec((tm, tn), lambda i, j: (i, j)),
        compiler_params=_cparams(("parallel", "arbitrary")),
        name="matmul",
    )(a_bf, w_bf)


XA_GROUP_ROWS = 256


def _xattn_kernel(x_ref, wq_ref, k_ref, v_ref, o_ref, xb_ref, *, hd):
    h = pl.program_id(1)

    @pl.when(h == 0)
    def _():
        xb_ref[...] = x_ref[...].astype(BF16)

    def qproj(r):
        return _dot(xb_ref[r * XA_GROUP_ROWS:(r + 1) * XA_GROUP_ROWS, :], wq_ref[...]).astype(BF16)

    groups = xb_ref.shape[0] // XA_GROUP_ROWS
    q_next = qproj(0)
    for r in range(groups):
        q = q_next
        if r + 1 < groups:
            q_next = qproj(r + 1)
        logits = _dot_nt(q, k_ref[...]) * (hd ** -0.5)
        mx = jnp.max(logits, axis=-1, keepdims=True)
        p = jnp.exp(logits - mx)
        den = jnp.sum(p, axis=-1, keepdims=True)
        rows = slice(r * XA_GROUP_ROWS, (r + 1) * XA_GROUP_ROWS)
        o_ref[rows, :] = (_dot(p.astype(BF16), v_ref[...]) / den).astype(o_ref.dtype)


def _xattn(x2d, wq_bf, kv, seq, mem_len, *, tm=1024):
    m, d = x2d.shape
    hd = d // XA_HEADS
    per_b = seq // tm
    return pl.pallas_call(
        functools.partial(_xattn_kernel, hd=hd),
        out_shape=jax.ShapeDtypeStruct((m, d), BF16),
        grid=(m // tm, XA_HEADS),
        in_specs=[
            pl.BlockSpec((tm, d), lambda i, h: (i, 0)),
            pl.BlockSpec((d, hd), lambda i, h: (0, h)),
            pl.BlockSpec((mem_len, hd), lambda i, h: (i // per_b, h)),
            pl.BlockSpec((mem_len, hd), lambda i, h: (i // per_b, XA_HEADS + h)),
        ],
        out_specs=pl.BlockSpec((tm, hd), lambda i, h: (i, h)),
        scratch_shapes=[pltpu.VMEM((tm, d), BF16)],
        compiler_params=_cparams(("parallel", "arbitrary")),
        name="memory_cross_attention",
    )(x2d, wq_bf, kv, kv)


FFN_HALO = 16
FFN_GROUP_ROWS = 256
FFN_CHUNK = 256


def _ffn_kernel(x_ref, xh_ref, wu_ref, wg_ref, cw_ref, cb_ref, wd_ref, g_ref, b_ref, o_ref,
                xb_ref, *, tm, per_b):
    i = pl.program_id(0)
    f = pl.program_id(1)
    nf = pl.num_programs(1)

    @pl.when(f == 0)
    def _():
        halo = jnp.where(i % per_b == 0, 0.0, xh_ref[...])
        xb_ref[0:FFN_HALO, :] = halo.astype(BF16)
        xb_ref[FFN_HALO:, :] = x_ref[...].astype(BF16)
        o_ref[...] = jnp.zeros_like(o_ref)

    cw = cw_ref[...]
    cb = cb_ref[...]
    def up(r):
        r0 = r * FFN_GROUP_ROWS
        u = _dot(xb_ref[FFN_HALO + r0:FFN_HALO + r0 + FFN_GROUP_ROWS, :], wu_ref[0])
        ge = _dot(xb_ref[r0:r0 + FFN_GROUP_ROWS + FFN_HALO, :], wg_ref[0])
        return u, ge

    groups = tm // FFN_GROUP_ROWS
    nxt = up(0)
    for r in range(groups):
        u, ge = nxt
        if r + 1 < groups:
            nxt = up(r + 1)
        gc = cb
        for tap in range(CONV_W):
            lo = FFN_HALO - (CONV_W - 1) + tap
            gc = gc + ge[lo:lo + FFN_GROUP_ROWS, :] * cw[tap:tap + 1, :]
        hcur = (gc * jax.nn.sigmoid(gc) * u).astype(BF16)
        r0 = r * FFN_GROUP_ROWS
        o_ref[r0:r0 + FFN_GROUP_ROWS, :] += _dot(hcur, wd_ref[...])

    @pl.when(f == nf - 1)
    def _():
        o_ref[...] = _layer_norm(DN_ALPHA * x_ref[...] + o_ref[...], g_ref[...], b_ref[...])


def _ffn(x2d, wup_t, conv_w, conv_b, wdown_bf, g, b, seq, *, tm=1024):
    m, d = x2d.shape
    fc = wup_t.shape[2]
    nf = wup_t.shape[0] // 2
    per_b = seq // tm
    halo_per_tile = tm // FFN_HALO
    return pl.pallas_call(
        functools.partial(_ffn_kernel, tm=tm, per_b=per_b),
        out_shape=jax.ShapeDtypeStruct((m, d), F32),
        grid=(m // tm, nf),
        in_specs=[
            pl.BlockSpec((tm, d), lambda i, f: (i, 0)),
            pl.BlockSpec((FFN_HALO, d), lambda i, f: (jnp.maximum(i * halo_per_tile - 1, 0), 0)),
            pl.BlockSpec((1, d, fc), lambda i, f: (f, 0, 0)),
            pl.BlockSpec((1, d, fc), lambda i, f: (nf + f, 0, 0)),
            pl.BlockSpec((CONV_W, fc), lambda i, f: (0, f)),
            pl.BlockSpec((1, fc), lambda i, f: (0, f)),
            pl.BlockSpec((fc, d), lambda i, f: (f, 0)),
            pl.BlockSpec((1, d), lambda i, f: (0, 0)),
            pl.BlockSpec((1, d), lambda i, f: (0, 0)),
        ],
        out_specs=pl.BlockSpec((tm, d), lambda i, f: (i, 0)),
        scratch_shapes=[pltpu.VMEM((tm + FFN_HALO, d), BF16)],
        compiler_params=_cparams(("parallel", "arbitrary")),
        name="conv_ffn_ln",
    )(x2d, x2d, wup_t, wup_t, conv_w, conv_b, wdown_bf, g, b)


def _rope_tables(seq):
    half = RET_D // 2
    inv = 1.0 / (ROPE_BASE ** (jnp.arange(half, dtype=F32) / half))
    ang = jnp.arange(seq).astype(F32)[:, None] * inv[None, :]
    cos, sin = jnp.cos(ang), jnp.sin(ang)
    return jnp.concatenate([cos, cos], axis=-1), jnp.concatenate([-sin, sin], axis=-1)


def _retention_tables():
    c = RET_CHUNK
    log_gamma = jnp.log1p(-jnp.exp2(-5.0 - jnp.arange(RET_HEADS, dtype=F32)))
    idx = jnp.arange(c, dtype=F32)
    diff = idx[:, None] - idx[None, :]
    dmat = jnp.where(diff[None] >= 0,
                     jnp.exp(log_gamma[:, None, None] * jnp.maximum(diff, 0.0)[None]), 0.0)
    kdec = jnp.exp(log_gamma[:, None] * (c - 1 - idx)[None, :])
    qdec = jnp.exp(log_gamma[:, None] * (idx + 1.0)[None, :])
    cdec = jnp.exp(log_gamma * c)
    bc = lambda t: jnp.broadcast_to(t[:, :, None], (RET_HEADS, c, RET_D))
    return dmat, bc(kdec), bc(qdec), cdec


def _t5_bucket_table():
    l = SWA_BLOCK
    dist = jnp.maximum(jnp.arange(l)[:, None] + l - jnp.arange(2 * l)[None, :], 0)
    max_exact = REL_BUCKETS // 2
    nf = jnp.maximum(dist, 1).astype(F32)
    large = max_exact + (jnp.log(nf / max_exact) / math.log(REL_MAX_DIST / max_exact)
                         * (REL_BUCKETS - max_exact)).astype(jnp.int32)
    large = jnp.minimum(large, REL_BUCKETS - 1)
    return jnp.where(dist < max_exact, dist, large).astype(jnp.int32)


def kernel(x, mem, w_in, ret_gn_g, swa_sinks, rel_bias, w_o, ln1_g, ln1_b, xa_wq, xa_wkv, xa_wo,
           ln2_g, ln2_b, ffn_w_up, ffn_conv_w, ffn_conv_b, ffn_w_down, ln3_g, ln3_b):
    batch, seq, d = x.shape
    mem_len = mem.shape[1]
    assert w_in.shape[0] == DEPTH and w_in.shape[2] == IN_W
    assert seq % 1024 == 0 and d % LANES == 0

    cosf, sinf = _rope_tables(seq)
    dmat, kdec, qdec, cdec = _retention_tables()
    bucket = _t5_bucket_table()
    bias = _bias_table(rel_bias, bucket)
    row = lambda t: t.reshape(1, -1)

    x2d = x.reshape(batch * seq, d)
    mem_bf = mem.reshape(batch * mem_len, d).astype(BF16)
    for l in range(DEPTH):
        proj = _in_proj(x2d, w_in[l].astype(BF16), cosf, sinf, seq)
        o_r = _retention(proj, cdec, dmat, kdec, qdec, row(ret_gn_g[l]), batch, seq)
        o_s = _swa(proj, swa_sinks[l], bias, batch, seq)
        x2d = _proj_res_ln(o_r, o_s, 0, 0, w_o[l].astype(BF16), x2d, row(ln1_g[l]), row(ln1_b[l]))

        kv = _matmul(mem_bf, xa_wkv[l].astype(BF16), tm=batch * mem_len, tn=1024)
        xa = _xattn(x2d, xa_wq[l].astype(BF16), kv, seq, mem_len)
        x2d = _proj_res_ln(xa, xa, 0, 1, xa_wo[l].astype(BF16), x2d, row(ln2_g[l]), row(ln2_b[l]))

        wup_t = ffn_w_up[l].astype(BF16).reshape(d, -1, FFN_CHUNK).transpose(1, 0, 2)
        x2d = _ffn(x2d, wup_t, ffn_conv_w[l], row(ffn_conv_b[l]),
                   ffn_w_down[l].astype(BF16), row(ln3_g[l]), row(ln3_b[l]), seq)
    return x2d.reshape(batch, seq, d)
```

```python
import functools
import math

import jax
import jax.numpy as jnp
import numpy as np
from jax import lax
from jax.experimental import pallas as pl
from jax.experimental.pallas import tpu as pltpu

RET_HEADS = 8
RET_D = 128
RET_CHUNK = 128
ROPE_BASE = 10000.0
SWA_HEADS = 16
SWA_KV_HEADS = 4
SWA_HD = 64
SWA_WINDOW = 128
SWA_BLOCK = 128
REL_BUCKETS = 32
REL_MAX_DIST = 128
XA_HEADS = 4
CONV_W = 3
LN_EPS = 1e-5
DEPTH = 1
DN_ALPHA = (2 * DEPTH) ** 0.25

RET_W = RET_HEADS * RET_D
SWA_W = SWA_HEADS * SWA_HD
SWA_KV_W = SWA_KV_HEADS * SWA_HD
OFF_RQ, OFF_RK, OFF_RV, OFF_RG = 0, RET_W, 2 * RET_W, 3 * RET_W
OFF_SQ = 4 * RET_W
OFF_SK = OFF_SQ + SWA_W
OFF_SV = OFF_SK + SWA_KV_W
IN_W = OFF_SV + SWA_KV_W

LANES = 128
MXU_COLS = 256
V7X_VMEM_BYTES = 64 * 1024 * 1024
VMEM_LIMIT = 56 * 1024 * 1024

BF16 = jnp.bfloat16
F32 = jnp.float32


def _cparams(sem, vmem=VMEM_LIMIT):
    return pltpu.CompilerParams(dimension_semantics=sem, vmem_limit_bytes=vmem)


def _layer_norm(y, g, b):
    mu = jnp.mean(y, axis=-1, keepdims=True)
    d = y - mu
    var = jnp.mean(d * d, axis=-1, keepdims=True)
    return d * lax.rsqrt(var + LN_EPS) * g + b


def _dot(a, b):
    return jnp.dot(a, b, preferred_element_type=F32)


def _dot_nt(a, b):
    return lax.dot_general(a, b, (((1,), (1,)), ((), ())), preferred_element_type=F32)


def _dot_tn(a, b):
    return lax.dot_general(a, b, (((0,), (0,)), ((), ())), preferred_element_type=F32)


IN_PROJ_GROUP_ROWS = 256


def _in_proj_kernel(x_ref, w_ref, cos_ref, sin_ref, o_ref, xb_ref, *, tn):
    j = pl.program_id(1)

    @pl.when(j == 0)
    def _():
        xb_ref[...] = x_ref[...].astype(BF16)

    n_q = OFF_RK // tn
    n_rope = OFF_RV // tn
    scale = jnp.where((j >= n_q) & (j < n_rope), RET_D ** -0.5, 1.0).astype(F32)
    tm = x_ref.shape[0]
    half = IN_PROJ_GROUP_ROWS
    for mh in range(tm // IN_PROJ_GROUP_ROWS):
        rows = slice(mh * half, (mh + 1) * half)
        cos = cos_ref[rows, :]
        sin = sin_ref[rows, :]
        for s in range(tn // MXU_COLS):
            acc = _dot(xb_ref[rows, :], w_ref[:, s * MXU_COLS:(s + 1) * MXU_COLS])
            for c in range(MXU_COLS // RET_D):
                a = acc[:, c * RET_D:(c + 1) * RET_D]
                r = pltpu.roll(a, RET_D // 2, 1)
                lo = s * MXU_COLS + c * RET_D
                o_ref[rows, lo:lo + RET_D] = ((a * cos + r * sin) * scale).astype(o_ref.dtype)


def _in_proj(x2d, w_bf, cosf, sinf, seq, *, tm=2048, tn=512):
    m, d = x2d.shape
    n = w_bf.shape[1]
    pos_blocks = seq // tm
    n_rope = OFF_RV // tn
    cosf = jnp.concatenate([cosf, jnp.ones((tm, RET_D), F32)], axis=0)
    sinf = jnp.concatenate([sinf, jnp.zeros((tm, RET_D), F32)], axis=0)
    tbl_idx = lambda i, j: (jnp.where(j < n_rope, i % pos_blocks, pos_blocks), 0)
    return pl.pallas_call(
        functools.partial(_in_proj_kernel, tn=tn),
        out_shape=jax.ShapeDtypeStruct((m, n), BF16),
        grid=(m // tm, n // tn),
        in_specs=[
            pl.BlockSpec((tm, d), lambda i, j: (i, 0)),
            pl.BlockSpec((d, tn), lambda i, j: (0, j)),
            pl.BlockSpec((tm, RET_D), tbl_idx),
            pl.BlockSpec((tm, RET_D), tbl_idx),
        ],
        out_specs=pl.BlockSpec((tm, tn), lambda i, j: (i, j)),
        scratch_shapes=[pltpu.VMEM((tm, d), BF16)],
        compiler_params=_cparams(("parallel", "arbitrary")),
        name="in_proj_rope",
    )(x2d, w_bf, cosf, sinf)


def _retention_kernel(cdec_ref, q_ref, k_ref, v_ref, g_ref, dmat_ref, kdec_ref, qdec_ref,
                      gn_ref, o_ref, state_ref, *, chunks):
    h = pl.program_id(1)
    n = pl.program_id(2)
    c_sz = RET_CHUNK

    @pl.when(n == 0)
    def _():
        state_ref[...] = jnp.zeros_like(state_ref)

    dmat = dmat_ref[0]
    kdec = kdec_ref[0]
    qdec = qdec_ref[0]
    cdec = cdec_ref[h]
    gn = gn_ref[...]
    rows_of = lambda c: slice(c * c_sz, (c + 1) * c_sz)
    scores = [(_dot_nt(q_ref[rows_of(c), :], k_ref[rows_of(c), :]) * dmat).astype(BF16)
              for c in range(chunks)]
    kvs = [_dot_tn((k_ref[rows_of(c), :].astype(F32) * kdec).astype(BF16), v_ref[rows_of(c), :])
           for c in range(chunks)]
    state = state_ref[...]
    prevs = []
    for c in range(chunks):
        prevs.append(state.astype(BF16))
        state = state * cdec + kvs[c]
    state_ref[...] = state
    for c in range(chunks):
        rows = rows_of(c)
        o = _dot(scores[c], v_ref[rows, :]) + _dot(q_ref[rows, :], prevs[c]) * qdec
        mu = jnp.mean(o, axis=-1, keepdims=True)
        dlt = o - mu
        var = jnp.mean(dlt * dlt, axis=-1, keepdims=True)
        on = dlt * lax.rsqrt(var + LN_EPS) * gn
        g = g_ref[rows, :].astype(F32)
        o_ref[rows, :] = (g * jax.nn.sigmoid(g) * on).astype(o_ref.dtype)


def _retention(proj, cdec, dmat, kdec, qdec, gn, batch, seq, *, rows=1024):
    m = proj.shape[0]
    nblk = seq // rows
    col = lambda off: off // RET_D

    def spec(off):
        return pl.BlockSpec((rows, RET_D), lambda b, h, n: (b * nblk + n, col(off) + h))

    hspec = lambda shape: pl.BlockSpec((1,) + shape, lambda b, h, n: (h, 0, 0))
    return pl.pallas_call(
        functools.partial(_retention_kernel, chunks=rows // RET_CHUNK),
        out_shape=jax.ShapeDtypeStruct((m, RET_W), BF16),
        grid=(batch, RET_HEADS, nblk),
        in_specs=[
            pl.BlockSpec(memory_space=pltpu.SMEM),
            spec(OFF_RQ), spec(OFF_RK), spec(OFF_RV), spec(OFF_RG),
            hspec((RET_CHUNK, RET_CHUNK)), hspec((RET_CHUNK, RET_D)), hspec((RET_CHUNK, RET_D)),
            pl.BlockSpec((1, RET_D), lambda b, h, n: (0, h)),
        ],
        out_specs=pl.BlockSpec((rows, RET_D), lambda b, h, n: (b * nblk + n, h)),
        scratch_shapes=[pltpu.VMEM((RET_D, RET_D), F32)],
        compiler_params=_cparams(("parallel", "parallel", "arbitrary")),
        name="retention",
    )(cdec, proj, proj, proj, proj, dmat, kdec, qdec, gn)


def _bias_kernel(rel_ref, bucket_ref, o_ref):
    h = pl.program_id(0)
    bucket = bucket_ref[...]
    l = bucket.shape[0]

    def body(b, acc):
        return jnp.where(bucket == b, rel_ref[b, h], acc)

    bias = lax.fori_loop(0, REL_BUCKETS, body, jnp.zeros(bucket.shape, F32))
    i = lax.broadcasted_iota(jnp.int32, bucket.shape, 0)
    j = lax.broadcasted_iota(jnp.int32, bucket.shape, 1)
    dist = i + l - j
    o_ref[0] = jnp.where((dist >= 0) & (dist < SWA_WINDOW), bias, -jnp.inf)


def _bias_table(rel_bias, bucket):
    l, l2 = bucket.shape
    return pl.pallas_call(
        _bias_kernel,
        out_shape=jax.ShapeDtypeStruct((SWA_HEADS, l, l2), F32),
        grid=(SWA_HEADS,),
        in_specs=[pl.BlockSpec(memory_space=pltpu.SMEM),
                  pl.BlockSpec((l, l2), lambda h: (0, 0))],
        out_specs=pl.BlockSpec((1, l, l2), lambda h: (h, 0, 0)),
        compiler_params=pltpu.CompilerParams(dimension_semantics=("arbitrary",)),
        name="t5_bias_table",
    )(rel_bias, bucket)


SWA_QK_AHEAD = 2


def _swap_lane_halves(x):
    u = pltpu.bitcast(x, jnp.uint32)
    return pltpu.bitcast(pltpu.roll(u, LANES // 2, 1), x.dtype)


def _swa_kernel(sink_ref, q_ref, kc_ref, vc_ref, kp_ref, vp_ref, bias_ref, first_ref, o_ref, *,
                blocks):
    l = SWA_BLOCK
    grp = SWA_HEADS // SWA_KV_HEADS
    low = lax.broadcasted_iota(jnp.int32, (2 * l, LANES), 1) < SWA_HD
    low_q = lax.broadcasted_iota(jnp.int32, (l, LANES), 1) < SWA_HD
    zero = jnp.zeros((2 * l, LANES), BF16)
    kv2 = {}

    def kv_pair(blk, kh):
        if (blk, kh) in kv2:
            return kv2[(blk, kh)]
        r0 = blk * l
        t, sub = divmod(kh, LANES // SWA_HD)
        cols = slice(t * LANES, (t + 1) * LANES)
        if blk == 0:
            kt = jnp.concatenate([kp_ref[:, cols], kc_ref[0:l, cols]], axis=0)
            vt = jnp.concatenate([vp_ref[:, cols], vc_ref[0:l, cols]], axis=0)
        else:
            kt = kc_ref[r0 - l:r0 + l, cols]
            vt = vc_ref[r0 - l:r0 + l, cols]
        kt = kt * (SWA_HD ** -0.5)
        kr = _swap_lane_halves(kt)
        vr = _swap_lane_halves(vt)
        k_lo, k_hi = (kt, kr) if sub == 0 else (kr, kt)
        v_lo, v_hi = (vt, vr) if sub == 0 else (vr, vt)
        k2 = jnp.concatenate([jnp.where(low, k_lo, zero), jnp.where(low, zero, k_hi)], axis=0)
        v2 = jnp.concatenate([jnp.where(low, v_lo, zero), jnp.where(low, zero, v_hi)], axis=0)
        kv2[(blk, kh)] = (k2, v2)
        return k2, v2

    tasks = [(blk, pair) for blk in range(blocks) for pair in range(SWA_HEADS // 2)]

    def qk(task):
        blk, pair = task
        k2, _ = kv_pair(blk, pair // (grp // 2))
        q2 = q_ref[blk * l:(blk + 1) * l, pair * LANES:(pair + 1) * LANES]
        return _dot_nt(q2, k2)

    def finish(task, logits2):
        blk, pair = task
        _, v2 = kv_pair(blk, pair // (grp // 2))
        ps, dens = [], []
        for s in range(2):
            hd = 2 * pair + s
            lg = logits2[:, s * 2 * l:(s + 1) * 2 * l] + bias_ref[hd]
            if blk == 0:
                lg = lg + first_ref[0]
            sink = sink_ref[hd]
            mx = jnp.maximum(jnp.max(lg, axis=-1, keepdims=True), sink)
            p = jnp.exp(lg - mx)
            dens.append(jnp.sum(p, axis=-1, keepdims=True) + jnp.exp(sink - mx))
            ps.append(p.astype(BF16))
        out2 = _dot(jnp.concatenate(ps, axis=1), v2)
        den2 = jnp.where(low_q, dens[0], dens[1])
        o_ref[blk * l:(blk + 1) * l, pair * LANES:(pair + 1) * LANES] = (out2 / den2).astype(o_ref.dtype)

    pending = [qk(t) for t in tasks[:SWA_QK_AHEAD]]
    for n, task in enumerate(tasks):
        if n + SWA_QK_AHEAD < len(tasks):
            pending.append(qk(tasks[n + SWA_QK_AHEAD]))
        finish(task, pending.pop(0))


def _swa(proj, sinks, bias, batch, seq, *, rows=256):
    m = proj.shape[0]
    nblk = seq // rows
    per = rows // SWA_BLOCK
    kcol, vcol = OFF_SK // SWA_KV_W, OFF_SV // SWA_KV_W
    l = SWA_BLOCK
    first = jnp.stack([jnp.zeros((l, 2 * l), F32),
                       jnp.where(jnp.arange(2 * l)[None, :] < l, -jnp.inf, 0.0)
                       * jnp.ones((l, 1), F32)])

    def prev_idx(b, n):
        return jnp.maximum((b * nblk + n) * per - 1, 0)

    return pl.pallas_call(
        functools.partial(_swa_kernel, blocks=per),
        out_shape=jax.ShapeDtypeStruct((m, SWA_W), BF16),
        grid=(batch, nblk),
        in_specs=[
            pl.BlockSpec(memory_space=pltpu.SMEM),
            pl.BlockSpec((rows, SWA_W), lambda b, n: (b * nblk + n, OFF_SQ // SWA_W)),
            pl.BlockSpec((rows, SWA_KV_W), lambda b, n: (b * nblk + n, kcol)),
            pl.BlockSpec((rows, SWA_KV_W), lambda b, n: (b * nblk + n, vcol)),
            pl.BlockSpec((SWA_BLOCK, SWA_KV_W), lambda b, n: (prev_idx(b, n), kcol)),
            pl.BlockSpec((SWA_BLOCK, SWA_KV_W), lambda b, n: (prev_idx(b, n), vcol)),
            pl.BlockSpec((SWA_HEADS, SWA_BLOCK, 2 * SWA_BLOCK), lambda b, n: (0, 0, 0)),
            pl.BlockSpec((1, SWA_BLOCK, 2 * SWA_BLOCK), lambda b, n: (jnp.where(n == 0, 1, 0), 0, 0)),
        ],
        out_specs=pl.BlockSpec((rows, SWA_W), lambda b, n: (b * nblk + n, 0)),
        compiler_params=_cparams(("parallel", "arbitrary")),
        name="swa_sink_attention",
    )(sinks, proj, proj, proj, proj, proj, bias, first)


LN_GROUP_ROWS = 256


def _proj_res_ln_kernel(a1_ref, a2_ref, w1_ref, w2_ref, x_ref, g_ref, b_ref, o_ref):
    for r in range(x_ref.shape[0] // LN_GROUP_ROWS):
        rows = slice(r * LN_GROUP_ROWS, (r + 1) * LN_GROUP_ROWS)
        y = _dot(a1_ref[rows, :], w1_ref[...]) + _dot(a2_ref[rows, :], w2_ref[...])
        o_ref[rows, :] = _layer_norm(DN_ALPHA * x_ref[rows, :] + y, g_ref[...], b_ref[...])


def _proj_res_ln(a1, a2, col1, col2, w_bf, x2d, g, b, *, tm=1024):
    m, d = x2d.shape
    kh = w_bf.shape[0] // 2
    return pl.pallas_call(
        _proj_res_ln_kernel,
        out_shape=jax.ShapeDtypeStruct((m, d), F32),
        grid=(m // tm,),
        in_specs=[
            pl.BlockSpec((tm, kh), lambda i: (i, col1)),
            pl.BlockSpec((tm, kh), lambda i: (i, col2)),
            pl.BlockSpec((kh, d), lambda i: (0, 0), pipeline_mode=pl.Buffered(1)),
            pl.BlockSpec((kh, d), lambda i: (1, 0), pipeline_mode=pl.Buffered(1)),
            pl.BlockSpec((tm, d), lambda i: (i, 0)),
            pl.BlockSpec((1, d), lambda i: (0, 0)),
            pl.BlockSpec((1, d), lambda i: (0, 0)),
        ],
        out_specs=pl.BlockSpec((tm, d), lambda i: (i, 0)),
        compiler_params=_cparams(("parallel",)),
        name="proj_residual_ln",
    )(a1, a2, w_bf, w_bf, x2d, g, b)


def _matmul_kernel(a_ref, w_ref, o_ref):
    o_ref[...] = _dot(a_ref[...], w_ref[...]).astype(o_ref.dtype)


def _matmul(a_bf, w_bf, *, tm, tn):
    m, k = a_bf.shape
    n = w_bf.shape[1]
    return pl.pallas_call(
        _matmul_kernel,
        out_shape=jax.ShapeDtypeStruct((m, n), BF16),
        grid=(m // tm, n // tn),
        in_specs=[pl.BlockSpec((tm, k), lambda i, j: (i, 0)),
                  pl.BlockSpec((k, tn), lambda i, j: (0, j))],
        out_specs=pl.BlockSpec((tm, tn), lambda i, j: (i, j)),
        compiler_params=_cparams(("parallel", "arbitrary")),
        name="matmul",
    )(a_bf, w_bf)


XA_GROUP_ROWS = 256


def _xattn_kernel(x_ref, wq_ref, k_ref, v_ref, o_ref, *, hd):
    groups = x_ref.shape[0] // XA_GROUP_ROWS
    tasks = [(r, h) for r in range(groups) for h in range(XA_HEADS)]
    xb = {}

    def qproj(task):
        r, h = task
        if r not in xb:
            xb[r] = x_ref[r * XA_GROUP_ROWS:(r + 1) * XA_GROUP_ROWS, :].astype(BF16)
        return _dot(xb[r], wq_ref[:, h * hd:(h + 1) * hd]).astype(BF16)

    q_next = qproj(tasks[0])
    for n, (r, h) in enumerate(tasks):
        q = q_next
        if n + 1 < len(tasks):
            q_next = qproj(tasks[n + 1])
        cols = slice(h * hd, (h + 1) * hd)
        logits = _dot_nt(q, k_ref[:, cols]) * (hd ** -0.5)
        mx = jnp.max(logits, axis=-1, keepdims=True)
        p = jnp.exp(logits - mx)
        den = jnp.sum(p, axis=-1, keepdims=True)
        rows = slice(r * XA_GROUP_ROWS, (r + 1) * XA_GROUP_ROWS)
        o_ref[rows, cols] = (_dot(p.astype(BF16), v_ref[:, cols]) / den).astype(o_ref.dtype)


def _xattn(x2d, wq_bf, kv, seq, mem_len, *, tm=1024):
    m, d = x2d.shape
    hd = d // XA_HEADS
    per_b = seq // tm
    return pl.pallas_call(
        functools.partial(_xattn_kernel, hd=hd),
        out_shape=jax.ShapeDtypeStruct((m, d), BF16),
        grid=(m // tm,),
        in_specs=[
            pl.BlockSpec((tm, d), lambda i: (i, 0)),
            pl.BlockSpec((d, d), lambda i: (0, 0), pipeline_mode=pl.Buffered(1)),
            pl.BlockSpec((mem_len, d), lambda i: (i // per_b, 0)),
            pl.BlockSpec((mem_len, d), lambda i: (i // per_b, 1)),
        ],
        out_specs=pl.BlockSpec((tm, d), lambda i: (i, 0)),
        compiler_params=_cparams(("parallel",)),
        name="memory_cross_attention",
    )(x2d, wq_bf, kv, kv)


FFN_HALO = 16
FFN_GROUP_ROWS = 256
FFN_CHUNK = 256


def _ffn_kernel(x_ref, xh_ref, wu_ref, wg_ref, cw_ref, cb_ref, wd_ref, g_ref, b_ref, o_ref,
                xb_ref, *, tm, per_b):
    i = pl.program_id(0)
    f = pl.program_id(1)
    nf = pl.num_programs(1)

    @pl.when(f == 0)
    def _():
        halo = jnp.where(i % per_b == 0, 0.0, xh_ref[...])
        xb_ref[0:FFN_HALO, :] = halo.astype(BF16)
        xb_ref[FFN_HALO:, :] = x_ref[...].astype(BF16)
        o_ref[...] = jnp.zeros_like(o_ref)

    cw = cw_ref[...]
    cb = cb_ref[...]
    def up(r):
        r0 = r * FFN_GROUP_ROWS
        u = _dot(xb_ref[FFN_HALO + r0:FFN_HALO + r0 + FFN_GROUP_ROWS, :], wu_ref[...])
        ge = _dot(xb_ref[r0:r0 + FFN_GROUP_ROWS + FFN_HALO, :], wg_ref[...])
        return u, ge

    groups = tm // FFN_GROUP_ROWS
    nxt = up(0)
    for r in range(groups):
        u, ge = nxt
        if r + 1 < groups:
            nxt = up(r + 1)
        gc = cb
        for tap in range(CONV_W):
            lo = FFN_HALO - (CONV_W - 1) + tap
            gc = gc + ge[lo:lo + FFN_GROUP_ROWS, :] * cw[tap:tap + 1, :]
        hcur = (gc * jax.nn.sigmoid(gc) * u).astype(BF16)
        r0 = r * FFN_GROUP_ROWS
        o_ref[r0:r0 + FFN_GROUP_ROWS, :] += _dot(hcur, wd_ref[...])

    @pl.when(f == nf - 1)
    def _():
        o_ref[...] = _layer_norm(DN_ALPHA * x_ref[...] + o_ref[...], g_ref[...], b_ref[...])


def _ffn(x2d, wup_bf, conv_w, conv_b, wdown_bf, g, b, seq, *, tm=1024, fc=FFN_CHUNK):
    m, d = x2d.shape
    nf = wdown_bf.shape[0] // fc
    per_b = seq // tm
    halo_per_tile = tm // FFN_HALO
    return pl.pallas_call(
        functools.partial(_ffn_kernel, tm=tm, per_b=per_b),
        out_shape=jax.ShapeDtypeStruct((m, d), F32),
        grid=(m // tm, nf),
        in_specs=[
            pl.BlockSpec((tm, d), lambda i, f: (i, 0)),
            pl.BlockSpec((FFN_HALO, d), lambda i, f: (jnp.maximum(i * halo_per_tile - 1, 0), 0)),
            pl.BlockSpec((d, fc), lambda i, f: (0, f)),
            pl.BlockSpec((d, fc), lambda i, f: (0, nf + f)),
            pl.BlockSpec((CONV_W, fc), lambda i, f: (0, f)),
            pl.BlockSpec((1, fc), lambda i, f: (0, f)),
            pl.BlockSpec((fc, d), lambda i, f: (f, 0)),
            pl.BlockSpec((1, d), lambda i, f: (0, 0)),
            pl.BlockSpec((1, d), lambda i, f: (0, 0)),
        ],
        out_specs=pl.BlockSpec((tm, d), lambda i, f: (i, 0)),
        scratch_shapes=[pltpu.VMEM((tm + FFN_HALO, d), BF16)],
        compiler_params=_cparams(("parallel", "arbitrary")),
        name="conv_ffn_ln",
    )(x2d, x2d, wup_bf, wup_bf, conv_w, conv_b, wdown_bf, g, b)


def _rope_tables(seq):
    half = RET_D // 2
    inv = 1.0 / (ROPE_BASE ** (jnp.arange(half, dtype=F32) / half))
    ang = jnp.arange(seq).astype(F32)[:, None] * inv[None, :]
    cos, sin = jnp.cos(ang), jnp.sin(ang)
    return jnp.concatenate([cos, cos], axis=-1), jnp.concatenate([-sin, sin], axis=-1)


def _retention_tables():
    c = RET_CHUNK
    log_gamma = jnp.log1p(-jnp.exp2(-5.0 - jnp.arange(RET_HEADS, dtype=F32)))
    idx = jnp.arange(c, dtype=F32)
    diff = idx[:, None] - idx[None, :]
    dmat = jnp.where(diff[None] >= 0,
                     jnp.exp(log_gamma[:, None, None] * jnp.maximum(diff, 0.0)[None]), 0.0)
    kdec = jnp.exp(log_gamma[:, None] * (c - 1 - idx)[None, :])
    qdec = jnp.exp(log_gamma[:, None] * (idx + 1.0)[None, :])
    cdec = jnp.exp(log_gamma * c)
    bc = lambda t: jnp.broadcast_to(t[:, :, None], (RET_HEADS, c, RET_D))
    return dmat, bc(kdec), bc(qdec), cdec


def _t5_bucket_table():
    l = SWA_BLOCK
    dist = jnp.maximum(jnp.arange(l)[:, None] + l - jnp.arange(2 * l)[None, :], 0)
    max_exact = REL_BUCKETS // 2
    nf = jnp.maximum(dist, 1).astype(F32)
    large = max_exact + (jnp.log(nf / max_exact) / math.log(REL_MAX_DIST / max_exact)
                         * (REL_BUCKETS - max_exact)).astype(jnp.int32)
    large = jnp.minimum(large, REL_BUCKETS - 1)
    return jnp.where(dist < max_exact, dist, large).astype(jnp.int32)


def kernel(x, mem, w_in, ret_gn_g, swa_sinks, rel_bias, w_o, ln1_g, ln1_b, xa_wq, xa_wkv, xa_wo,
           ln2_g, ln2_b, ffn_w_up, ffn_conv_w, ffn_conv_b, ffn_w_down, ln3_g, ln3_b):
    batch, seq, d = x.shape
    mem_len = mem.shape[1]
    assert w_in.shape[0] == DEPTH and w_in.shape[2] == IN_W
    assert seq % 1024 == 0 and d % LANES == 0

    cosf, sinf = _rope_tables(seq)
    dmat, kdec, qdec, cdec = _retention_tables()
    bucket = _t5_bucket_table()
    bias = _bias_table(rel_bias, bucket)
    row = lambda t: t.reshape(1, -1)

    x2d = x.reshape(batch * seq, d)
    mem_bf = mem.reshape(batch * mem_len, d).astype(BF16)
    for l in range(DEPTH):
        proj = _in_proj(x2d, w_in[l].astype(BF16), cosf, sinf, seq)
        o_r = _retention(proj, cdec, dmat, kdec, qdec, row(ret_gn_g[l]), batch, seq)
        o_s = _swa(proj, swa_sinks[l], bias, batch, seq)
        x2d = _proj_res_ln(o_r, o_s, 0, 0, w_o[l].astype(BF16), x2d, row(ln1_g[l]), row(ln1_b[l]))

        kv = _matmul(mem_bf, xa_wkv[l].astype(BF16), tm=batch * mem_len, tn=1024)
        xa = _xattn(x2d, xa_wq[l].astype(BF16), kv, seq, mem_len)
        x2d = _proj_res_ln(xa, xa, 0, 1, xa_wo[l].astype(BF16), x2d, row(ln2_g[l]), row(ln2_b[l]))

        x2d = _ffn(x2d, ffn_w_up[l].astype(BF16), ffn_conv_w[l], row(ffn_conv_b[l]),
                   ffn_w_down[l].astype(BF16), row(ln3_g[l]), row(ln3_b[l]), seq)
    return x2d.reshape(batch, seq, d)
```

```python
import functools
import math

import jax
import jax.numpy as jnp
import numpy as np
from jax import lax
from jax.experimental import pallas as pl
from jax.experimental.pallas import tpu as pltpu

RET_HEADS = 8
RET_D = 128
RET_CHUNK = 128
ROPE_BASE = 10000.0
SWA_HEADS = 16
SWA_KV_HEADS = 4
SWA_HD = 64
SWA_WINDOW = 128
SWA_BLOCK = 128
REL_BUCKETS = 32
REL_MAX_DIST = 128
XA_HEADS = 4
CONV_W = 3
LN_EPS = 1e-5
DEPTH = 1
DN_ALPHA = (2 * DEPTH) ** 0.25

RET_W = RET_HEADS * RET_D
SWA_W = SWA_HEADS * SWA_HD
SWA_KV_W = SWA_KV_HEADS * SWA_HD
OFF_RQ, OFF_RK, OFF_RV, OFF_RG = 0, RET_W, 2 * RET_W, 3 * RET_W
OFF_SQ = 4 * RET_W
OFF_SK = OFF_SQ + SWA_W
OFF_SV = OFF_SK + SWA_KV_W
IN_W = OFF_SV + SWA_KV_W

LANES = 128
MXU_COLS = 256
V7X_VMEM_BYTES = 64 * 1024 * 1024
VMEM_LIMIT = 56 * 1024 * 1024

BF16 = jnp.bfloat16
F32 = jnp.float32


def _cparams(sem, vmem=VMEM_LIMIT):
    return pltpu.CompilerParams(dimension_semantics=sem, vmem_limit_bytes=vmem)


def _layer_norm(y, g, b):
    mu = jnp.mean(y, axis=-1, keepdims=True)
    d = y - mu
    var = jnp.mean(d * d, axis=-1, keepdims=True)
    return d * lax.rsqrt(var + LN_EPS) * g + b


def _dot(a, b):
    return jnp.dot(a, b, preferred_element_type=F32)


def _dot_nt(a, b):
    return lax.dot_general(a, b, (((1,), (1,)), ((), ())), preferred_element_type=F32)


def _dot_tn(a, b):
    return lax.dot_general(a, b, (((0,), (0,)), ((), ())), preferred_element_type=F32)


IN_PROJ_GROUP_ROWS = 256


def _in_proj_kernel(x_ref, w_ref, cos_ref, sin_ref, o_ref, xb_ref, *, tn):
    j = pl.program_id(1)

    @pl.when(j == 0)
    def _():
        xb_ref[...] = x_ref[...].astype(BF16)

    n_q = OFF_RK // tn
    n_rope = OFF_RV // tn
    scale = jnp.where((j >= n_q) & (j < n_rope), RET_D ** -0.5, 1.0).astype(F32)
    tm = x_ref.shape[0]
    half = IN_PROJ_GROUP_ROWS
    for mh in range(tm // IN_PROJ_GROUP_ROWS):
        rows = slice(mh * half, (mh + 1) * half)
        cos = cos_ref[rows, :]
        sin = sin_ref[rows, :]
        for s in range(tn // MXU_COLS):
            acc = _dot(xb_ref[rows, :], w_ref[:, s * MXU_COLS:(s + 1) * MXU_COLS])
            for c in range(MXU_COLS // RET_D):
                a = acc[:, c * RET_D:(c + 1) * RET_D]
                r = pltpu.roll(a, RET_D // 2, 1)
                lo = s * MXU_COLS + c * RET_D
                o_ref[rows, lo:lo + RET_D] = ((a * cos + r * sin) * scale).astype(o_ref.dtype)


def _in_proj(x2d, w_bf, cosf, sinf, seq, *, tm=2048, tn=512):
    m, d = x2d.shape
    n = w_bf.shape[1]
    pos_blocks = seq // tm
    n_rope = OFF_RV // tn
    cosf = jnp.concatenate([cosf, jnp.ones((tm, RET_D), F32)], axis=0)
    sinf = jnp.concatenate([sinf, jnp.zeros((tm, RET_D), F32)], axis=0)
    tbl_idx = lambda i, j: (jnp.where(j < n_rope, i % pos_blocks, pos_blocks), 0)
    return pl.pallas_call(
        functools.partial(_in_proj_kernel, tn=tn),
        out_shape=jax.ShapeDtypeStruct((m, n), BF16),
        grid=(m // tm, n // tn),
        in_specs=[
            pl.BlockSpec((tm, d), lambda i, j: (i, 0)),
            pl.BlockSpec((d, tn), lambda i, j: (0, j)),
            pl.BlockSpec((tm, RET_D), tbl_idx),
            pl.BlockSpec((tm, RET_D), tbl_idx),
        ],
        out_specs=pl.BlockSpec((tm, tn), lambda i, j: (i, j)),
        scratch_shapes=[pltpu.VMEM((tm, d), BF16)],
        compiler_params=_cparams(("parallel", "arbitrary")),
        name="in_proj_rope",
    )(x2d, w_bf, cosf, sinf)


CAST_COLS = 2048
CAST_BLOCK_ROWS = 256


def _retention_kernel(cdec_ref, q_ref, k_ref, v_ref, g_ref, dmat_ref, kdec_ref, qdec_ref,
                      gn_ref, *rest, chunks, cast_ranges):
    n_cast = len(cast_ranges)
    w_refs = rest[:n_cast]
    o_ref = rest[n_cast]
    wb_refs = rest[n_cast + 1:2 * n_cast + 1]
    state_ref = rest[2 * n_cast + 1]
    h = pl.program_id(1)
    n = pl.program_id(2)
    c_sz = RET_CHUNK

    @pl.when(n == 0)
    def _():
        state_ref[...] = jnp.zeros_like(state_ref)

    dmat = dmat_ref[0]
    kdec = kdec_ref[0]
    qdec = qdec_ref[0]
    cdec = cdec_ref[h]
    gn = gn_ref[...]
    rows_of = lambda c: slice(c * c_sz, (c + 1) * c_sz)
    scores = [(_dot_nt(q_ref[rows_of(c), :], k_ref[rows_of(c), :]) * dmat).astype(BF16)
              for c in range(chunks)]
    kvs = [_dot_tn((k_ref[rows_of(c), :].astype(F32) * kdec).astype(BF16), v_ref[rows_of(c), :])
           for c in range(chunks)]
    state = state_ref[...]
    prevs = []
    for c in range(chunks):
        prevs.append(state.astype(BF16))
        state = state * cdec + kvs[c]
    state_ref[...] = state
    for c in range(chunks):
        rows = rows_of(c)
        o = _dot(scores[c], v_ref[rows, :]) + _dot(q_ref[rows, :], prevs[c]) * qdec
        mu = jnp.mean(o, axis=-1, keepdims=True)
        dlt = o - mu
        var = jnp.mean(dlt * dlt, axis=-1, keepdims=True)
        on = dlt * lax.rsqrt(var + LN_EPS) * gn
        g = g_ref[rows, :].astype(F32)
        o_ref[rows, :] = (g * jax.nn.sigmoid(g) * on).astype(o_ref.dtype)

    step = (pl.program_id(0) * pl.num_programs(1) + h) * pl.num_programs(2) + n
    for (start, nblocks), w_ref, wb_ref in zip(cast_ranges, w_refs, wb_refs):
        @pl.when((step >= start) & (step < start + nblocks))
        def _(w_ref=w_ref, wb_ref=wb_ref):
            wb_ref[...] = w_ref[...].astype(BF16)


def _retention(proj, cdec, dmat, kdec, qdec, gn, batch, seq, cast_ws, *, rows=1024):
    m = proj.shape[0]
    nblk = seq // rows
    col = lambda off: off // RET_D

    def spec(off):
        return pl.BlockSpec((rows, RET_D), lambda b, h, n: (b * nblk + n, col(off) + h))

    hspec = lambda shape: pl.BlockSpec((1,) + shape, lambda b, h, n: (h, 0, 0))

    flats = [w.reshape(-1, CAST_COLS) for w in cast_ws]
    cast_ranges, start = [], 0
    for f in flats:
        assert f.shape[0] % CAST_BLOCK_ROWS == 0
        cast_ranges.append((start, f.shape[0] // CAST_BLOCK_ROWS))
        start += cast_ranges[-1][1]
    assert start <= batch * RET_HEADS * nblk, "not enough grid steps to cast the weights"

    def cast_spec(start, nblocks):
        def idx(b, h, n):
            step = (b * RET_HEADS + h) * nblk + n
            return (jnp.clip(step - start, 0, nblocks - 1), 0)
        return pl.BlockSpec((CAST_BLOCK_ROWS, CAST_COLS), idx)

    cast_specs = [cast_spec(*r) for r in cast_ranges]
    outs = pl.pallas_call(
        functools.partial(_retention_kernel, chunks=rows // RET_CHUNK, cast_ranges=tuple(cast_ranges)),
        out_shape=[jax.ShapeDtypeStruct((m, RET_W), BF16)]
        + [jax.ShapeDtypeStruct(f.shape, BF16) for f in flats],
        grid=(batch, RET_HEADS, nblk),
        in_specs=[
            pl.BlockSpec(memory_space=pltpu.SMEM),
            spec(OFF_RQ), spec(OFF_RK), spec(OFF_RV), spec(OFF_RG),
            hspec((RET_CHUNK, RET_CHUNK)), hspec((RET_CHUNK, RET_D)), hspec((RET_CHUNK, RET_D)),
            pl.BlockSpec((1, RET_D), lambda b, h, n: (0, h)),
        ] + cast_specs,
        out_specs=[pl.BlockSpec((rows, RET_D), lambda b, h, n: (b * nblk + n, h))] + cast_specs,
        scratch_shapes=[pltpu.VMEM((RET_D, RET_D), F32)],
        compiler_params=_cparams(("arbitrary", "arbitrary", "arbitrary")),
        name="retention",
    )(cdec, proj, proj, proj, proj, dmat, kdec, qdec, gn, *flats)
    return outs[0], [o.reshape(w.shape) for o, w in zip(outs[1:], cast_ws)]


def _bias_kernel(rel_ref, bucket_ref, o_ref):
    h = pl.program_id(0)
    bucket = bucket_ref[...]
    l = bucket.shape[0]

    def body(b, acc):
        return jnp.where(bucket == b, rel_ref[b, h], acc)

    bias = lax.fori_loop(0, REL_BUCKETS, body, jnp.zeros(bucket.shape, F32))
    i = lax.broadcasted_iota(jnp.int32, bucket.shape, 0)
    j = lax.broadcasted_iota(jnp.int32, bucket.shape, 1)
    dist = i + l - j
    o_ref[0] = jnp.where((dist >= 0) & (dist < SWA_WINDOW), bias, -jnp.inf)


def _bias_table(rel_bias, bucket):
    l, l2 = bucket.shape
    return pl.pallas_call(
        _bias_kernel,
        out_shape=jax.ShapeDtypeStruct((SWA_HEADS, l, l2), F32),
        grid=(SWA_HEADS,),
        in_specs=[pl.BlockSpec(memory_space=pltpu.SMEM),
                  pl.BlockSpec((l, l2), lambda h: (0, 0))],
        out_specs=pl.BlockSpec((1, l, l2), lambda h: (h, 0, 0)),
        compiler_params=pltpu.CompilerParams(dimension_semantics=("arbitrary",)),
        name="t5_bias_table",
    )(rel_bias, bucket)


SWA_QK_AHEAD = 2


def _swap_lane_halves(x):
    u = pltpu.bitcast(x, jnp.uint32)
    return pltpu.bitcast(pltpu.roll(u, LANES // 2, 1), x.dtype)


def _swa_kernel(sink_ref, q_ref, kc_ref, vc_ref, kp_ref, vp_ref, bias_ref, first_ref, o_ref, *,
                blocks):
    l = SWA_BLOCK
    grp = SWA_HEADS // SWA_KV_HEADS
    low = lax.broadcasted_iota(jnp.int32, (2 * l, LANES), 1) < SWA_HD
    low_q = lax.broadcasted_iota(jnp.int32, (l, LANES), 1) < SWA_HD
    zero = jnp.zeros((2 * l, LANES), BF16)
    kv2 = {}

    def kv_pair(blk, kh):
        if (blk, kh) in kv2:
            return kv2[(blk, kh)]
        r0 = blk * l
        t, sub = divmod(kh, LANES // SWA_HD)
        cols = slice(t * LANES, (t + 1) * LANES)
        if blk == 0:
            kt = jnp.concatenate([kp_ref[:, cols], kc_ref[0:l, cols]], axis=0)
            vt = jnp.concatenate([vp_ref[:, cols], vc_ref[0:l, cols]], axis=0)
        else:
            kt = kc_ref[r0 - l:r0 + l, cols]
            vt = vc_ref[r0 - l:r0 + l, cols]
        kt = kt * (SWA_HD ** -0.5)
        kr = _swap_lane_halves(kt)
        vr = _swap_lane_halves(vt)
        k_lo, k_hi = (kt, kr) if sub == 0 else (kr, kt)
        v_lo, v_hi = (vt, vr) if sub == 0 else (vr, vt)
        k2 = jnp.concatenate([jnp.where(low, k_lo, zero), jnp.where(low, zero, k_hi)], axis=0)
        v2 = jnp.concatenate([jnp.where(low, v_lo, zero), jnp.where(low, zero, v_hi)], axis=0)
        kv2[(blk, kh)] = (k2, v2)
        return k2, v2

    tasks = [(blk, pair) for blk in range(blocks) for pair in range(SWA_HEADS // 2)]

    def qk(task):
        blk, pair = task
        k2, _ = kv_pair(blk, pair // (grp // 2))
        q2 = q_ref[blk * l:(blk + 1) * l, pair * LANES:(pair + 1) * LANES]
        return _dot_nt(q2, k2)

    def finish(task, logits2):
        blk, pair = task
        _, v2 = kv_pair(blk, pair // (grp // 2))
        ps, dens = [], []
        for s in range(2):
            hd = 2 * pair + s
            lg = logits2[:, s * 2 * l:(s + 1) * 2 * l] + bias_ref[hd]
            if blk == 0:
                lg = lg + first_ref[0]
            sink = sink_ref[hd]
            mx = jnp.maximum(jnp.max(lg, axis=-1, keepdims=True), sink)
            p = jnp.exp(lg - mx)
            dens.append(jnp.sum(p, axis=-1, keepdims=True) + jnp.exp(sink - mx))
            ps.append(p.astype(BF16))
        out2 = _dot(jnp.concatenate(ps, axis=1), v2)
        den2 = jnp.where(low_q, dens[0], dens[1])
        o_ref[blk * l:(blk + 1) * l, pair * LANES:(pair + 1) * LANES] = (out2 / den2).astype(o_ref.dtype)

    pending = [qk(t) for t in tasks[:SWA_QK_AHEAD]]
    for n, task in enumerate(tasks):
        if n + SWA_QK_AHEAD < len(tasks):
            pending.append(qk(tasks[n + SWA_QK_AHEAD]))
        finish(task, pending.pop(0))


def _swa(proj, sinks, bias, batch, seq, *, rows=256):
    m = proj.shape[0]
    nblk = seq // rows
    per = rows // SWA_BLOCK
    kcol, vcol = OFF_SK // SWA_KV_W, OFF_SV // SWA_KV_W
    l = SWA_BLOCK
    first = jnp.stack([jnp.zeros((l, 2 * l), F32),
                       jnp.where(jnp.arange(2 * l)[None, :] < l, -jnp.inf, 0.0)
                       * jnp.ones((l, 1), F32)])

    def prev_idx(b, n):
        return jnp.maximum((b * nblk + n) * per - 1, 0)

    return pl.pallas_call(
        functools.partial(_swa_kernel, blocks=per),
        out_shape=jax.ShapeDtypeStruct((m, SWA_W), BF16),
        grid=(batch, nblk),
        in_specs=[
            pl.BlockSpec(memory_space=pltpu.SMEM),
            pl.BlockSpec((rows, SWA_W), lambda b, n: (b * nblk + n, OFF_SQ // SWA_W)),
            pl.BlockSpec((rows, SWA_KV_W), lambda b, n: (b * nblk + n, kcol)),
            pl.BlockSpec((rows, SWA_KV_W), lambda b, n: (b * nblk + n, vcol)),
            pl.BlockSpec((SWA_BLOCK, SWA_KV_W), lambda b, n: (prev_idx(b, n), kcol)),
            pl.BlockSpec((SWA_BLOCK, SWA_KV_W), lambda b, n: (prev_idx(b, n), vcol)),
            pl.BlockSpec((SWA_HEADS, SWA_BLOCK, 2 * SWA_BLOCK), lambda b, n: (0, 0, 0)),
            pl.BlockSpec((1, SWA_BLOCK, 2 * SWA_BLOCK), lambda b, n: (jnp.where(n == 0, 1, 0), 0, 0)),
        ],
        out_specs=pl.BlockSpec((rows, SWA_W), lambda b, n: (b * nblk + n, 0)),
        compiler_params=_cparams(("parallel", "arbitrary")),
        name="swa_sink_attention",
    )(sinks, proj, proj, proj, proj, proj, bias, first)


LN_GROUP_ROWS = 256


def _proj_res_ln_kernel(a1_ref, a2_ref, w1_ref, w2_ref, x_ref, g_ref, b_ref, o_ref):
    for r in range(x_ref.shape[0] // LN_GROUP_ROWS):
        rows = slice(r * LN_GROUP_ROWS, (r + 1) * LN_GROUP_ROWS)
        y = _dot(a1_ref[rows, :], w1_ref[...]) + _dot(a2_ref[rows, :], w2_ref[...])
        o_ref[rows, :] = _layer_norm(DN_ALPHA * x_ref[rows, :] + y, g_ref[...], b_ref[...])


def _proj_res_ln(a1, a2, col1, col2, w_bf, x2d, g, b, *, tm=1024):
    m, d = x2d.shape
    kh = w_bf.shape[0] // 2
    return pl.pallas_call(
        _proj_res_ln_kernel,
        out_shape=jax.ShapeDtypeStruct((m, d), F32),
        grid=(m // tm,),
        in_specs=[
            pl.BlockSpec((tm, kh), lambda i: (i, col1)),
            pl.BlockSpec((tm, kh), lambda i: (i, col2)),
            pl.BlockSpec((kh, d), lambda i: (0, 0), pipeline_mode=pl.Buffered(1)),
            pl.BlockSpec((kh, d), lambda i: (1, 0), pipeline_mode=pl.Buffered(1)),
            pl.BlockSpec((tm, d), lambda i: (i, 0)),
            pl.BlockSpec((1, d), lambda i: (0, 0)),
            pl.BlockSpec((1, d), lambda i: (0, 0)),
        ],
        out_specs=pl.BlockSpec((tm, d), lambda i: (i, 0)),
        compiler_params=_cparams(("parallel",)),
        name="proj_residual_ln",
    )(a1, a2, w_bf, w_bf, x2d, g, b)


def _matmul_kernel(a_ref, w_ref, o_ref):
    o_ref[...] = _dot(a_ref[...], w_ref[...]).astype(o_ref.dtype)


def _matmul(a_bf, w_bf, *, tm, tn):
    m, k = a_bf.shape
    n = w_bf.shape[1]
    return pl.pallas_call(
        _matmul_kernel,
        out_shape=jax.ShapeDtypeStruct((m, n), BF16),
        grid=(m // tm, n // tn),
        in_specs=[pl.BlockSpec((tm, k), lambda i, j: (i, 0)),
                  pl.BlockSpec((k, tn), lambda i, j: (0, j))],
        out_specs=pl.BlockSpec((tm, tn), lambda i, j: (i, j)),
        compiler_params=_cparams(("parallel", "arbitrary")),
        name="matmul",
    )(a_bf, w_bf)


XA_GROUP_ROWS = 256


def _xattn_kernel(x_ref, wq_ref, k_ref, v_ref, o_ref, *, hd):
    groups = x_ref.shape[0] // XA_GROUP_ROWS
    tasks = [(r, h) for r in range(groups) for h in range(XA_HEADS)]
    xb = {}

    def qproj(task):
        r, h = task
        if r not in xb:
            xb[r] = x_ref[r * XA_GROUP_ROWS:(r + 1) * XA_GROUP_ROWS, :].astype(BF16)
        return _dot(xb[r], wq_ref[:, h * hd:(h + 1) * hd]).astype(BF16)

    q_next = qproj(tasks[0])
    for n, (r, h) in enumerate(tasks):
        q = q_next
        if n + 1 < len(tasks):
            q_next = qproj(tasks[n + 1])
        cols = slice(h * hd, (h + 1) * hd)
        logits = _dot_nt(q, k_ref[:, cols]) * (hd ** -0.5)
        mx = jnp.max(logits, axis=-1, keepdims=True)
        p = jnp.exp(logits - mx)
        den = jnp.sum(p, axis=-1, keepdims=True)
        rows = slice(r * XA_GROUP_ROWS, (r + 1) * XA_GROUP_ROWS)
        o_ref[rows, cols] = (_dot(p.astype(BF16), v_ref[:, cols]) / den).astype(o_ref.dtype)


def _xattn(x2d, wq_bf, kv, seq, mem_len, *, tm=1024):
    m, d = x2d.shape
    hd = d // XA_HEADS
    per_b = seq // tm
    return pl.pallas_call(
        functools.partial(_xattn_kernel, hd=hd),
        out_shape=jax.ShapeDtypeStruct((m, d), BF16),
        grid=(m // tm,),
        in_specs=[
            pl.BlockSpec((tm, d), lambda i: (i, 0)),
            pl.BlockSpec((d, d), lambda i: (0, 0), pipeline_mode=pl.Buffered(1)),
            pl.BlockSpec((mem_len, d), lambda i: (i // per_b, 0)),
            pl.BlockSpec((mem_len, d), lambda i: (i // per_b, 1)),
        ],
        out_specs=pl.BlockSpec((tm, d), lambda i: (i, 0)),
        compiler_params=_cparams(("parallel",)),
        name="memory_cross_attention",
    )(x2d, wq_bf, kv, kv)


FFN_HALO = 16
FFN_GROUP_ROWS = 256
FFN_CHUNK = 256


def _ffn_kernel(x_ref, xh_ref, wu_ref, wg_ref, cw_ref, cb_ref, wd_ref, g_ref, b_ref, o_ref,
                xb_ref, *, tm, per_b):
    i = pl.program_id(0)
    f = pl.program_id(1)
    nf = pl.num_programs(1)

    @pl.when(f == 0)
    def _():
        halo = jnp.where(i % per_b == 0, 0.0, xh_ref[...])
        xb_ref[0:FFN_HALO, :] = halo.astype(BF16)
        xb_ref[FFN_HALO:, :] = x_ref[...].astype(BF16)
        o_ref[...] = jnp.zeros_like(o_ref)

    cw = cw_ref[...]
    cb = cb_ref[...]
    def up(r):
        r0 = r * FFN_GROUP_ROWS
        u = _dot(xb_ref[FFN_HALO + r0:FFN_HALO + r0 + FFN_GROUP_ROWS, :], wu_ref[...])
        ge = _dot(xb_ref[r0:r0 + FFN_GROUP_ROWS + FFN_HALO, :], wg_ref[...])
        return u, ge

    groups = tm // FFN_GROUP_ROWS
    nxt = up(0)
    for r in range(groups):
        u, ge = nxt
        if r + 1 < groups:
            nxt = up(r + 1)
        gc = cb
        for tap in range(CONV_W):
            lo = FFN_HALO - (CONV_W - 1) + tap
            gc = gc + ge[lo:lo + FFN_GROUP_ROWS, :] * cw[tap:tap + 1, :]
        hcur = (gc * jax.nn.sigmoid(gc) * u).astype(BF16)
        r0 = r * FFN_GROUP_ROWS
        o_ref[r0:r0 + FFN_GROUP_ROWS, :] += _dot(hcur, wd_ref[...])

    @pl.when(f == nf - 1)
    def _():
        o_ref[...] = _layer_norm(DN_ALPHA * x_ref[...] + o_ref[...], g_ref[...], b_ref[...])


def _ffn(x2d, wup_bf, conv_w, conv_b, wdown_bf, g, b, seq, *, tm=1024, fc=FFN_CHUNK):
    m, d = x2d.shape
    nf = wdown_bf.shape[0] // fc
    per_b = seq // tm
    halo_per_tile = tm // FFN_HALO
    return pl.pallas_call(
        functools.partial(_ffn_kernel, tm=tm, per_b=per_b),
        out_shape=jax.ShapeDtypeStruct((m, d), F32),
        grid=(m // tm, nf),
        in_specs=[
            pl.BlockSpec((tm, d), lambda i, f: (i, 0)),
            pl.BlockSpec((FFN_HALO, d), lambda i, f: (jnp.maximum(i * halo_per_tile - 1, 0), 0)),
            pl.BlockSpec((d, fc), lambda i, f: (0, f)),
            pl.BlockSpec((d, fc), lambda i, f: (0, nf + f)),
            pl.BlockSpec((CONV_W, fc), lambda i, f: (0, f)),
            pl.BlockSpec((1, fc), lambda i, f: (0, f)),
            pl.BlockSpec((fc, d), lambda i, f: (f, 0)),
            pl.BlockSpec((1, d), lambda i, f: (0, 0)),
            pl.BlockSpec((1, d), lambda i, f: (0, 0)),
        ],
        out_specs=pl.BlockSpec((tm, d), lambda i, f: (i, 0)),
        scratch_shapes=[pltpu.VMEM((tm + FFN_HALO, d), BF16)],
        compiler_params=_cparams(("parallel", "arbitrary")),
        name="conv_ffn_ln",
    )(x2d, x2d, wup_bf, wup_bf, conv_w, conv_b, wdown_bf, g, b)


def _rope_tables(seq):
    half = RET_D // 2
    inv = 1.0 / (ROPE_BASE ** (jnp.arange(half, dtype=F32) / half))
    ang = jnp.arange(seq).astype(F32)[:, None] * inv[None, :]
    cos, sin = jnp.cos(ang), jnp.sin(ang)
    return jnp.concatenate([cos, cos], axis=-1), jnp.concatenate([-sin, sin], axis=-1)


def _retention_tables():
    c = RET_CHUNK
    log_gamma = jnp.log1p(-jnp.exp2(-5.0 - jnp.arange(RET_HEADS, dtype=F32)))
    idx = jnp.arange(c, dtype=F32)
    diff = idx[:, None] - idx[None, :]
    dmat = jnp.where(diff[None] >= 0,
                     jnp.exp(log_gamma[:, None, None] * jnp.maximum(diff, 0.0)[None]), 0.0)
    kdec = jnp.exp(log_gamma[:, None] * (c - 1 - idx)[None, :])
    qdec = jnp.exp(log_gamma[:, None] * (idx + 1.0)[None, :])
    cdec = jnp.exp(log_gamma * c)
    bc = lambda t: jnp.broadcast_to(t[:, :, None], (RET_HEADS, c, RET_D))
    return dmat, bc(kdec), bc(qdec), cdec


def _t5_bucket_table():
    l = SWA_BLOCK
    dist = jnp.maximum(jnp.arange(l)[:, None] + l - jnp.arange(2 * l)[None, :], 0)
    max_exact = REL_BUCKETS // 2
    nf = jnp.maximum(dist, 1).astype(F32)
    large = max_exact + (jnp.log(nf / max_exact) / math.log(REL_MAX_DIST / max_exact)
                         * (REL_BUCKETS - max_exact)).astype(jnp.int32)
    large = jnp.minimum(large, REL_BUCKETS - 1)
    return jnp.where(dist < max_exact, dist, large).astype(jnp.int32)


def kernel(x, mem, w_in, ret_gn_g, swa_sinks, rel_bias, w_o, ln1_g, ln1_b, xa_wq, xa_wkv, xa_wo,
           ln2_g, ln2_b, ffn_w_up, ffn_conv_w, ffn_conv_b, ffn_w_down, ln3_g, ln3_b):
    batch, seq, d = x.shape
    mem_len = mem.shape[1]
    assert w_in.shape[0] == DEPTH and w_in.shape[2] == IN_W
    assert seq % 1024 == 0 and d % LANES == 0

    cosf, sinf = _rope_tables(seq)
    dmat, kdec, qdec, cdec = _retention_tables()
    bucket = _t5_bucket_table()
    bias = _bias_table(rel_bias, bucket)
    row = lambda t: t.reshape(1, -1)

    x2d = x.reshape(batch * seq, d)
    mem_bf = mem.reshape(batch * mem_len, d).astype(BF16)
    for l in range(DEPTH):
        proj = _in_proj(x2d, w_in[l].astype(BF16), cosf, sinf, seq)
        later_ws = [w_o[l], xa_wkv[l], xa_wq[l], xa_wo[l], ffn_w_up[l], ffn_w_down[l]]
        o_r, (wo_bf, wkv_bf, wq_bf, xwo_bf, wup_bf, wdown_bf) = _retention(
            proj, cdec, dmat, kdec, qdec, row(ret_gn_g[l]), batch, seq, later_ws)
        o_s = _swa(proj, swa_sinks[l], bias, batch, seq)
        x2d = _proj_res_ln(o_r, o_s, 0, 0, wo_bf, x2d, row(ln1_g[l]), row(ln1_b[l]))

        kv = _matmul(mem_bf, wkv_bf, tm=batch * mem_len, tn=1024)
        xa = _xattn(x2d, wq_bf, kv, seq, mem_len)
        x2d = _proj_res_ln(xa, xa, 0, 1, xwo_bf, x2d, row(ln2_g[l]), row(ln2_b[l]))

        x2d = _ffn(x2d, wup_bf, ffn_conv_w[l], row(ffn_conv_b[l]),
                   wdown_bf, row(ln3_g[l]), row(ln3_b[l]), seq)
    return x2d.reshape(batch, seq, d)
```

```python
import functools
import math

import jax
import jax.numpy as jnp
import numpy as np
from jax import lax
from jax.experimental import pallas as pl
from jax.experimental.pallas import tpu as pltpu

RET_HEADS = 8
RET_D = 128
RET_CHUNK = 128
ROPE_BASE = 10000.0
SWA_HEADS = 16
SWA_KV_HEADS = 4
SWA_HD = 64
SWA_WINDOW = 128
SWA_BLOCK = 128
REL_BUCKETS = 32
REL_MAX_DIST = 128
XA_HEADS = 4
CONV_W = 3
LN_EPS = 1e-5
DEPTH = 1
DN_ALPHA = (2 * DEPTH) ** 0.25

RET_W = RET_HEADS * RET_D
SWA_W = SWA_HEADS * SWA_HD
SWA_KV_W = SWA_KV_HEADS * SWA_HD
OFF_RQ, OFF_RK, OFF_RV, OFF_RG = 0, RET_W, 2 * RET_W, 3 * RET_W
OFF_SQ = 4 * RET_W
OFF_SK = OFF_SQ + SWA_W
OFF_SV = OFF_SK + SWA_KV_W
IN_W = OFF_SV + SWA_KV_W

LANES = 128
MXU_COLS = 256
V7X_VMEM_BYTES = 64 * 1024 * 1024
VMEM_LIMIT = 56 * 1024 * 1024

BF16 = jnp.bfloat16
F32 = jnp.float32


def _cparams(sem, vmem=VMEM_LIMIT):
    return pltpu.CompilerParams(dimension_semantics=sem, vmem_limit_bytes=vmem)


def _layer_norm(y, g, b):
    mu = jnp.mean(y, axis=-1, keepdims=True)
    d = y - mu
    var = jnp.mean(d * d, axis=-1, keepdims=True)
    return d * lax.rsqrt(var + LN_EPS) * g + b


def _dot(a, b):
    return jnp.dot(a, b, preferred_element_type=F32)


def _dot_nt(a, b):
    return lax.dot_general(a, b, (((1,), (1,)), ((), ())), preferred_element_type=F32)


def _dot_tn(a, b):
    return lax.dot_general(a, b, (((0,), (0,)), ((), ())), preferred_element_type=F32)


IN_PROJ_GROUP_ROWS = 256


def _in_proj_kernel(x_ref, w_ref, cos_ref, sin_ref, o_ref, xb_ref, *, tn):
    j = pl.program_id(1)

    @pl.when(j == 0)
    def _():
        xb_ref[...] = x_ref[...].astype(BF16)

    n_q = OFF_RK // tn
    n_rope = OFF_RV // tn
    scale = jnp.where((j >= n_q) & (j < n_rope), RET_D ** -0.5, 1.0).astype(F32)
    tm = x_ref.shape[0]
    half = IN_PROJ_GROUP_ROWS
    for mh in range(tm // IN_PROJ_GROUP_ROWS):
        rows = slice(mh * half, (mh + 1) * half)
        cos = cos_ref[rows, :]
        sin = sin_ref[rows, :]
        for s in range(tn // MXU_COLS):
            acc = _dot(xb_ref[rows, :], w_ref[:, s * MXU_COLS:(s + 1) * MXU_COLS])
            for c in range(MXU_COLS // RET_D):
                a = acc[:, c * RET_D:(c + 1) * RET_D]
                r = pltpu.roll(a, RET_D // 2, 1)
                lo = s * MXU_COLS + c * RET_D
                o_ref[rows, lo:lo + RET_D] = ((a * cos + r * sin) * scale).astype(o_ref.dtype)


def _in_proj(x2d, w_bf, cosf, sinf, seq, *, tm=2048, tn=512):
    m, d = x2d.shape
    n = w_bf.shape[1]
    pos_blocks = seq // tm
    n_rope = OFF_RV // tn
    cosf = jnp.concatenate([cosf, jnp.ones((tm, RET_D), F32)], axis=0)
    sinf = jnp.concatenate([sinf, jnp.zeros((tm, RET_D), F32)], axis=0)
    tbl_idx = lambda i, j: (jnp.where(j < n_rope, i % pos_blocks, pos_blocks), 0)
    return pl.pallas_call(
        functools.partial(_in_proj_kernel, tn=tn),
        out_shape=jax.ShapeDtypeStruct((m, n), BF16),
        grid=(m // tm, n // tn),
        in_specs=[
            pl.BlockSpec((tm, d), lambda i, j: (i, 0)),
            pl.BlockSpec((d, tn), lambda i, j: (0, j)),
            pl.BlockSpec((tm, RET_D), tbl_idx),
            pl.BlockSpec((tm, RET_D), tbl_idx),
        ],
        out_specs=pl.BlockSpec((tm, tn), lambda i, j: (i, j)),
        scratch_shapes=[pltpu.VMEM((tm, d), BF16)],
        compiler_params=_cparams(("parallel", "arbitrary")),
        name="in_proj_rope",
    )(x2d, w_bf, cosf, sinf)


def _retention_kernel(cdec_ref, q_ref, k_ref, v_ref, g_ref, dmat_ref, kdec_ref, qdec_ref,
                      gn_ref, o_ref, state_ref, *, chunks):
    n = pl.program_id(1)
    c_sz = RET_CHUNK

    @pl.when(n == 0)
    def _():
        state_ref[...] = jnp.zeros_like(state_ref)

    units = [(h, c) for h in range(RET_HEADS) for c in range(chunks)]
    rows_of = lambda c: slice(c * c_sz, (c + 1) * c_sz)
    cols_of = lambda h: slice(h * RET_D, (h + 1) * RET_D)
    scores = {(h, c): (_dot_nt(q_ref[rows_of(c), cols_of(h)], k_ref[rows_of(c), cols_of(h)])
                       * dmat_ref[h]).astype(BF16) for h, c in units}
    kvs = {(h, c): _dot_tn((k_ref[rows_of(c), cols_of(h)].astype(F32) * kdec_ref[h]).astype(BF16),
                           v_ref[rows_of(c), cols_of(h)]) for h, c in units}
    prevs = {}
    for h in range(RET_HEADS):
        state = state_ref[h]
        cdec = cdec_ref[h]
        for c in range(chunks):
            prevs[(h, c)] = state.astype(BF16)
            state = state * cdec + kvs[(h, c)]
        state_ref[h] = state
    for h, c in units:
        rows, cols = rows_of(c), cols_of(h)
        o = _dot(scores[(h, c)], v_ref[rows, cols]) + _dot(q_ref[rows, cols], prevs[(h, c)]) * qdec_ref[h]
        mu = jnp.mean(o, axis=-1, keepdims=True)
        dlt = o - mu
        var = jnp.mean(dlt * dlt, axis=-1, keepdims=True)
        on = dlt * lax.rsqrt(var + LN_EPS) * gn_ref[:, cols]
        g = g_ref[rows, cols].astype(F32)
        o_ref[rows, cols] = (g * jax.nn.sigmoid(g) * on).astype(o_ref.dtype)


def _retention(proj, cdec, dmat, kdec, qdec, gn, batch, seq, *, rows=512):
    m = proj.shape[0]
    nblk = seq // rows
    spec = lambda off: pl.BlockSpec((rows, RET_W), lambda b, n: (b * nblk + n, off // RET_W))
    table = lambda t: pl.BlockSpec(t.shape, lambda b, n: (0,) * t.ndim)
    return pl.pallas_call(
        functools.partial(_retention_kernel, chunks=rows // RET_CHUNK),
        out_shape=jax.ShapeDtypeStruct((m, RET_W), BF16),
        grid=(batch, nblk),
        in_specs=[
            pl.BlockSpec(memory_space=pltpu.SMEM),
            spec(OFF_RQ), spec(OFF_RK), spec(OFF_RV), spec(OFF_RG),
            table(dmat), table(kdec), table(qdec), table(gn),
        ],
        out_specs=pl.BlockSpec((rows, RET_W), lambda b, n: (b * nblk + n, 0)),
        scratch_shapes=[pltpu.VMEM((RET_HEADS, RET_D, RET_D), F32)],
        compiler_params=_cparams(("parallel", "arbitrary")),
        name="retention",
    )(cdec, proj, proj, proj, proj, dmat, kdec, qdec, gn)


def _bias_kernel(rel_ref, bucket_ref, o_ref):
    h = pl.program_id(0)
    bucket = bucket_ref[...]
    l = bucket.shape[0]

    def body(b, acc):
        return jnp.where(bucket == b, rel_ref[b, h], acc)

    bias = lax.fori_loop(0, REL_BUCKETS, body, jnp.zeros(bucket.shape, F32))
    i = lax.broadcasted_iota(jnp.int32, bucket.shape, 0)
    j = lax.broadcasted_iota(jnp.int32, bucket.shape, 1)
    dist = i + l - j
    o_ref[0] = jnp.where((dist >= 0) & (dist < SWA_WINDOW), bias, -jnp.inf)


def _bias_table(rel_bias, bucket):
    l, l2 = bucket.shape
    return pl.pallas_call(
        _bias_kernel,
        out_shape=jax.ShapeDtypeStruct((SWA_HEADS, l, l2), F32),
        grid=(SWA_HEADS,),
        in_specs=[pl.BlockSpec(memory_space=pltpu.SMEM),
                  pl.BlockSpec((l, l2), lambda h: (0, 0))],
        out_specs=pl.BlockSpec((1, l, l2), lambda h: (h, 0, 0)),
        compiler_params=pltpu.CompilerParams(dimension_semantics=("arbitrary",)),
        name="t5_bias_table",
    )(rel_bias, bucket)


SWA_QK_AHEAD = 2


def _swap_lane_halves(x):
    u = pltpu.bitcast(x, jnp.uint32)
    return pltpu.bitcast(pltpu.roll(u, LANES // 2, 1), x.dtype)


def _swa_kernel(sink_ref, q_ref, kc_ref, vc_ref, kp_ref, vp_ref, bias_ref, first_ref, o_ref, *,
                blocks):
    l = SWA_BLOCK
    grp = SWA_HEADS // SWA_KV_HEADS
    low = lax.broadcasted_iota(jnp.int32, (2 * l, LANES), 1) < SWA_HD
    low_q = lax.broadcasted_iota(jnp.int32, (l, LANES), 1) < SWA_HD
    zero = jnp.zeros((2 * l, LANES), BF16)
    kv2 = {}

    def kv_pair(blk, kh):
        if (blk, kh) in kv2:
            return kv2[(blk, kh)]
        r0 = blk * l
        t, sub = divmod(kh, LANES // SWA_HD)
        cols = slice(t * LANES, (t + 1) * LANES)
        if blk == 0:
            kt = jnp.concatenate([kp_ref[:, cols], kc_ref[0:l, cols]], axis=0)
            vt = jnp.concatenate([vp_ref[:, cols], vc_ref[0:l, cols]], axis=0)
        else:
            kt = kc_ref[r0 - l:r0 + l, cols]
            vt = vc_ref[r0 - l:r0 + l, cols]
        kt = kt * (SWA_HD ** -0.5)
        kr = _swap_lane_halves(kt)
        vr = _swap_lane_halves(vt)
        k_lo, k_hi = (kt, kr) if sub == 0 else (kr, kt)
        v_lo, v_hi = (vt, vr) if sub == 0 else (vr, vt)
        k2 = jnp.concatenate([jnp.where(low, k_lo, zero), jnp.where(low, zero, k_hi)], axis=0)
        v2 = jnp.concatenate([jnp.where(low, v_lo, zero), jnp.where(low, zero, v_hi)], axis=0)
        kv2[(blk, kh)] = (k2, v2)
        return k2, v2

    tasks = [(blk, pair) for blk in range(blocks) for pair in range(SWA_HEADS // 2)]

    def qk(task):
        blk, pair = task
        k2, _ = kv_pair(blk, pair // (grp // 2))
        q2 = q_ref[blk * l:(blk + 1) * l, pair * LANES:(pair + 1) * LANES]
        return _dot_nt(q2, k2)

    def finish(task, logits2):
        blk, pair = task
        _, v2 = kv_pair(blk, pair // (grp // 2))
        ps, dens = [], []
        for s in range(2):
            hd = 2 * pair + s
            lg = logits2[:, s * 2 * l:(s + 1) * 2 * l] + bias_ref[hd]
            if blk == 0:
                lg = lg + first_ref[0]
            sink = sink_ref[hd]
            mx = jnp.maximum(jnp.max(lg, axis=-1, keepdims=True), sink)
            p = jnp.exp(lg - mx)
            dens.append(jnp.sum(p, axis=-1, keepdims=True) + jnp.exp(sink - mx))
            ps.append(p.astype(BF16))
        out2 = _dot(jnp.concatenate(ps, axis=1), v2)
        den2 = jnp.where(low_q, dens[0], dens[1])
        o_ref[blk * l:(blk + 1) * l, pair * LANES:(pair + 1) * LANES] = (out2 / den2).astype(o_ref.dtype)

    pending = [qk(t) for t in tasks[:SWA_QK_AHEAD]]
    for n, task in enumerate(tasks):
        if n + SWA_QK_AHEAD < len(tasks):
            pending.append(qk(tasks[n + SWA_QK_AHEAD]))
        finish(task, pending.pop(0))


def _swa(proj, sinks, bias, batch, seq, *, rows=256):
    m = proj.shape[0]
    nblk = seq // rows
    per = rows // SWA_BLOCK
    kcol, vcol = OFF_SK // SWA_KV_W, OFF_SV // SWA_KV_W
    l = SWA_BLOCK
    first = jnp.stack([jnp.zeros((l, 2 * l), F32),
                       jnp.where(jnp.arange(2 * l)[None, :] < l, -jnp.inf, 0.0)
                       * jnp.ones((l, 1), F32)])

    def prev_idx(b, n):
        return jnp.maximum((b * nblk + n) * per - 1, 0)

    return pl.pallas_call(
        functools.partial(_swa_kernel, blocks=per),
        out_shape=jax.ShapeDtypeStruct((m, SWA_W), BF16),
        grid=(batch, nblk),
        in_specs=[
            pl.BlockSpec(memory_space=pltpu.SMEM),
            pl.BlockSpec((rows, SWA_W), lambda b, n: (b * nblk + n, OFF_SQ // SWA_W)),
            pl.BlockSpec((rows, SWA_KV_W), lambda b, n: (b * nblk + n, kcol)),
            pl.BlockSpec((rows, SWA_KV_W), lambda b, n: (b * nblk + n, vcol)),
            pl.BlockSpec((SWA_BLOCK, SWA_KV_W), lambda b, n: (prev_idx(b, n), kcol)),
            pl.BlockSpec((SWA_BLOCK, SWA_KV_W), lambda b, n: (prev_idx(b, n), vcol)),
            pl.BlockSpec((SWA_HEADS, SWA_BLOCK, 2 * SWA_BLOCK), lambda b, n: (0, 0, 0)),
            pl.BlockSpec((1, SWA_BLOCK, 2 * SWA_BLOCK), lambda b, n: (jnp.where(n == 0, 1, 0), 0, 0)),
        ],
        out_specs=pl.BlockSpec((rows, SWA_W), lambda b, n: (b * nblk + n, 0)),
        compiler_params=_cparams(("parallel", "arbitrary")),
        name="swa_sink_attention",
    )(sinks, proj, proj, proj, proj, proj, bias, first)


LN_GROUP_ROWS = 256


def _proj_res_ln_kernel(a1_ref, a2_ref, w1_ref, w2_ref, x_ref, g_ref, b_ref, o_ref):
    for r in range(x_ref.shape[0] // LN_GROUP_ROWS):
        rows = slice(r * LN_GROUP_ROWS, (r + 1) * LN_GROUP_ROWS)
        y = _dot(a1_ref[rows, :], w1_ref[...]) + _dot(a2_ref[rows, :], w2_ref[...])
        o_ref[rows, :] = _layer_norm(DN_ALPHA * x_ref[rows, :] + y, g_ref[...], b_ref[...])


def _proj_res_ln(a1, a2, col1, col2, w_bf, x2d, g, b, *, tm=1024):
    m, d = x2d.shape
    kh = w_bf.shape[0] // 2
    return pl.pallas_call(
        _proj_res_ln_kernel,
        out_shape=jax.ShapeDtypeStruct((m, d), F32),
        grid=(m // tm,),
        in_specs=[
            pl.BlockSpec((tm, kh), lambda i: (i, col1)),
            pl.BlockSpec((tm, kh), lambda i: (i, col2)),
            pl.BlockSpec((kh, d), lambda i: (0, 0), pipeline_mode=pl.Buffered(1)),
            pl.BlockSpec((kh, d), lambda i: (1, 0), pipeline_mode=pl.Buffered(1)),
            pl.BlockSpec((tm, d), lambda i: (i, 0)),
            pl.BlockSpec((1, d), lambda i: (0, 0)),
            pl.BlockSpec((1, d), lambda i: (0, 0)),
        ],
        out_specs=pl.BlockSpec((tm, d), lambda i: (i, 0)),
        compiler_params=_cparams(("parallel",)),
        name="proj_residual_ln",
    )(a1, a2, w_bf, w_bf, x2d, g, b)


def _matmul_kernel(a_ref, w_ref, o_ref):
    o_ref[...] = _dot(a_ref[...], w_ref[...]).astype(o_ref.dtype)


def _matmul(a_bf, w_bf, *, tm, tn):
    m, k = a_bf.shape
    n = w_bf.shape[1]
    return pl.pallas_call(
        _matmul_kernel,
        out_shape=jax.ShapeDtypeStruct((m, n), BF16),
        grid=(m // tm, n // tn),
        in_specs=[pl.BlockSpec((tm, k), lambda i, j: (i, 0)),
                  pl.BlockSpec((k, tn), lambda i, j: (0, j))],
        out_specs=pl.BlockSpec((tm, tn), lambda i, j: (i, j)),
        compiler_params=_cparams(("parallel", "arbitrary")),
        name="matmul",
    )(a_bf, w_bf)


XA_GROUP_ROWS = 256


def _xattn_kernel(x_ref, wq_ref, k_ref, v_ref, o_ref, *, hd):
    groups = x_ref.shape[0] // XA_GROUP_ROWS
    tasks = [(r, h) for r in range(groups) for h in range(XA_HEADS)]
    xb = {}

    def qproj(task):
        r, h = task
        if r not in xb:
            xb[r] = x_ref[r * XA_GROUP_ROWS:(r + 1) * XA_GROUP_ROWS, :].astype(BF16)
        return _dot(xb[r], wq_ref[:, h * hd:(h + 1) * hd]).astype(BF16)

    q_next = qproj(tasks[0])
    for n, (r, h) in enumerate(tasks):
        q = q_next
        if n + 1 < len(tasks):
            q_next = qproj(tasks[n + 1])
        cols = slice(h * hd, (h + 1) * hd)
        logits = _dot_nt(q, k_ref[:, cols]) * (hd ** -0.5)
        mx = jnp.max(logits, axis=-1, keepdims=True)
        p = jnp.exp(logits - mx)
        den = jnp.sum(p, axis=-1, keepdims=True)
        rows = slice(r * XA_GROUP_ROWS, (r + 1) * XA_GROUP_ROWS)
        o_ref[rows, cols] = (_dot(p.astype(BF16), v_ref[:, cols]) / den).astype(o_ref.dtype)


def _xattn(x2d, wq_bf, kv, seq, mem_len, *, tm=1024):
    m, d = x2d.shape
    hd = d // XA_HEADS
    per_b = seq // tm
    return pl.pallas_call(
        functools.partial(_xattn_kernel, hd=hd),
        out_shape=jax.ShapeDtypeStruct((m, d), BF16),
        grid=(m // tm,),
        in_specs=[
            pl.BlockSpec((tm, d), lambda i: (i, 0)),
            pl.BlockSpec((d, d), lambda i: (0, 0), pipeline_mode=pl.Buffered(1)),
            pl.BlockSpec((mem_len, d), lambda i: (i // per_b, 0)),
            pl.BlockSpec((mem_len, d), lambda i: (i // per_b, 1)),
        ],
        out_specs=pl.BlockSpec((tm, d), lambda i: (i, 0)),
        compiler_params=_cparams(("parallel",)),
        name="memory_cross_attention",
    )(x2d, wq_bf, kv, kv)


FFN_HALO = 16
FFN_GROUP_ROWS = 256
FFN_CHUNK = 256


def _ffn_kernel(x_ref, xh_ref, wu_ref, wg_ref, cw_ref, cb_ref, wd_ref, g_ref, b_ref, o_ref,
                xb_ref, *, tm, per_b):
    i = pl.program_id(0)
    f = pl.program_id(1)
    nf = pl.num_programs(1)

    groups = tm // FFN_GROUP_ROWS

    def body(first, last):
        cw = cw_ref[...]
        cb = cb_ref[...]

        def up(r):
            r0 = r * FFN_GROUP_ROWS
            if first:
                xb_ref[FFN_HALO + r0:FFN_HALO + r0 + FFN_GROUP_ROWS, :] = (
                    x_ref[r0:r0 + FFN_GROUP_ROWS, :].astype(BF16))
            u = _dot(xb_ref[FFN_HALO + r0:FFN_HALO + r0 + FFN_GROUP_ROWS, :], wu_ref[...])
            ge = _dot(xb_ref[r0:r0 + FFN_GROUP_ROWS + FFN_HALO, :], wg_ref[...])
            return u, ge

        if first:
            halo = jnp.where(i % per_b == 0, 0.0, xh_ref[...])
            xb_ref[0:FFN_HALO, :] = halo.astype(BF16)
        nxt = up(0)
        for r in range(groups):
            u, ge = nxt
            if r + 1 < groups:
                nxt = up(r + 1)
            gc = cb
            for tap in range(CONV_W):
                lo = FFN_HALO - (CONV_W - 1) + tap
                gc = gc + ge[lo:lo + FFN_GROUP_ROWS, :] * cw[tap:tap + 1, :]
            hcur = (gc * jax.nn.sigmoid(gc) * u).astype(BF16)
            rows = slice(r * FFN_GROUP_ROWS, (r + 1) * FFN_GROUP_ROWS)
            acc = _dot(hcur, wd_ref[...])
            if not first:
                acc = o_ref[rows, :] + acc
            if last:
                acc = _layer_norm(DN_ALPHA * x_ref[rows, :] + acc, g_ref[...], b_ref[...])
            o_ref[rows, :] = acc

    pl.when(f == 0)(functools.partial(body, True, False))
    pl.when((f > 0) & (f < nf - 1))(functools.partial(body, False, False))
    pl.when(f == nf - 1)(functools.partial(body, False, True))


def _ffn(x2d, wup_bf, conv_w, conv_b, wdown_bf, g, b, seq, *, tm=1024, fc=FFN_CHUNK):
    m, d = x2d.shape
    nf = wdown_bf.shape[0] // fc
    assert nf >= 2, "the first and the last hidden chunk are separate code paths"
    per_b = seq // tm
    halo_per_tile = tm // FFN_HALO
    return pl.pallas_call(
        functools.partial(_ffn_kernel, tm=tm, per_b=per_b),
        out_shape=jax.ShapeDtypeStruct((m, d), F32),
        grid=(m // tm, nf),
        in_specs=[
            pl.BlockSpec((tm, d), lambda i, f: (i, 0)),
            pl.BlockSpec((FFN_HALO, d), lambda i, f: (jnp.maximum(i * halo_per_tile - 1, 0), 0)),
            pl.BlockSpec((d, fc), lambda i, f: (0, f)),
            pl.BlockSpec((d, fc), lambda i, f: (0, nf + f)),
            pl.BlockSpec((CONV_W, fc), lambda i, f: (0, f)),
            pl.BlockSpec((1, fc), lambda i, f: (0, f)),
            pl.BlockSpec((fc, d), lambda i, f: (f, 0)),
            pl.BlockSpec((1, d), lambda i, f: (0, 0)),
            pl.BlockSpec((1, d), lambda i, f: (0, 0)),
        ],
        out_specs=pl.BlockSpec((tm, d), lambda i, f: (i, 0)),
        scratch_shapes=[pltpu.VMEM((tm + FFN_HALO, d), BF16)],
        compiler_params=_cparams(("parallel", "arbitrary")),
        name="conv_ffn_ln",
    )(x2d, x2d, wup_bf, wup_bf, conv_w, conv_b, wdown_bf, g, b)


def _rope_tables(seq):
    half = RET_D // 2
    inv = 1.0 / (ROPE_BASE ** (jnp.arange(half, dtype=F32) / half))
    ang = jnp.arange(seq).astype(F32)[:, None] * inv[None, :]
    cos, sin = jnp.cos(ang), jnp.sin(ang)
    return jnp.concatenate([cos, cos], axis=-1), jnp.concatenate([-sin, sin], axis=-1)


def _retention_tables():
    c = RET_CHUNK
    log_gamma = jnp.log1p(-jnp.exp2(-5.0 - jnp.arange(RET_HEADS, dtype=F32)))
    idx = jnp.arange(c, dtype=F32)
    diff = idx[:, None] - idx[None, :]
    dmat = jnp.where(diff[None] >= 0,
                     jnp.exp(log_gamma[:, None, None] * jnp.maximum(diff, 0.0)[None]), 0.0)
    kdec = jnp.exp(log_gamma[:, None] * (c - 1 - idx)[None, :])
    qdec = jnp.exp(log_gamma[:, None] * (idx + 1.0)[None, :])
    cdec = jnp.exp(log_gamma * c)
    bc = lambda t: jnp.broadcast_to(t[:, :, None], (RET_HEADS, c, RET_D))
    return dmat, bc(kdec), bc(qdec), cdec


def _t5_bucket_table():
    l = SWA_BLOCK
    dist = jnp.maximum(jnp.arange(l)[:, None] + l - jnp.arange(2 * l)[None, :], 0)
    max_exact = REL_BUCKETS // 2
    nf = jnp.maximum(dist, 1).astype(F32)
    large = max_exact + (jnp.log(nf / max_exact) / math.log(REL_MAX_DIST / max_exact)
                         * (REL_BUCKETS - max_exact)).astype(jnp.int32)
    large = jnp.minimum(large, REL_BUCKETS - 1)
    return jnp.where(dist < max_exact, dist, large).astype(jnp.int32)


def kernel(x, mem, w_in, ret_gn_g, swa_sinks, rel_bias, w_o, ln1_g, ln1_b, xa_wq, xa_wkv, xa_wo,
           ln2_g, ln2_b, ffn_w_up, ffn_conv_w, ffn_conv_b, ffn_w_down, ln3_g, ln3_b):
    batch, seq, d = x.shape
    mem_len = mem.shape[1]
    assert w_in.shape[0] == DEPTH and w_in.shape[2] == IN_W
    assert seq % 1024 == 0 and d % LANES == 0

    cosf, sinf = _rope_tables(seq)
    dmat, kdec, qdec, cdec = _retention_tables()
    bucket = _t5_bucket_table()
    bias = _bias_table(rel_bias, bucket)
    row = lambda t: t.reshape(1, -1)

    x2d = x.reshape(batch * seq, d)
    mem_bf = mem.reshape(batch * mem_len, d).astype(BF16)
    for l in range(DEPTH):
        proj = _in_proj(x2d, w_in[l].astype(BF16), cosf, sinf, seq)
        o_r = _retention(proj, cdec, dmat, kdec, qdec, row(ret_gn_g[l]), batch, seq)
        o_s = _swa(proj, swa_sinks[l], bias, batch, seq)
        x2d = _proj_res_ln(o_r, o_s, 0, 0, w_o[l].astype(BF16), x2d, row(ln1_g[l]), row(ln1_b[l]))

        kv = _matmul(mem_bf, xa_wkv[l].astype(BF16), tm=batch * mem_len, tn=1024)
        xa = _xattn(x2d, xa_wq[l].astype(BF16), kv, seq, mem_len)
        x2d = _proj_res_ln(xa, xa, 0, 1, xa_wo[l].astype(BF16), x2d, row(ln2_g[l]), row(ln2_b[l]))

        x2d = _ffn(x2d, ffn_w_up[l].astype(BF16), ffn_conv_w[l], row(ffn_conv_b[l]),
                   ffn_w_down[l].astype(BF16), row(ln3_g[l]), row(ln3_b[l]), seq)
    return x2d.reshape(batch, seq, d)
```

```python
import functools
import math

import jax
import jax.numpy as jnp
import numpy as np
from jax import lax
from jax.experimental import pallas as pl
from jax.experimental.pallas import tpu as pltpu

RET_HEADS = 8
RET_D = 128
RET_CHUNK = 128
ROPE_BASE = 10000.0
SWA_HEADS = 16
SWA_KV_HEADS = 4
SWA_HD = 64
SWA_WINDOW = 128
SWA_BLOCK = 128
REL_BUCKETS = 32
REL_MAX_DIST = 128
XA_HEADS = 4
CONV_W = 3
LN_EPS = 1e-5
DEPTH = 1
DN_ALPHA = (2 * DEPTH) ** 0.25

RET_W = RET_HEADS * RET_D
SWA_W = SWA_HEADS * SWA_HD
SWA_KV_W = SWA_KV_HEADS * SWA_HD
OFF_RQ, OFF_RK, OFF_RV, OFF_RG = 0, RET_W, 2 * RET_W, 3 * RET_W
OFF_SQ = 4 * RET_W
OFF_SK = OFF_SQ + SWA_W
OFF_SV = OFF_SK + SWA_KV_W
IN_W = OFF_SV + SWA_KV_W

LANES = 128
MXU_COLS = 256
V7X_VMEM_BYTES = 64 * 1024 * 1024
VMEM_LIMIT = 56 * 1024 * 1024

BF16 = jnp.bfloat16
F32 = jnp.float32


def _cparams(sem, vmem=VMEM_LIMIT):
    return pltpu.CompilerParams(dimension_semantics=sem, vmem_limit_bytes=vmem)


def _layer_norm(y, g, b):
    mu = jnp.mean(y, axis=-1, keepdims=True)
    d = y - mu
    var = jnp.mean(d * d, axis=-1, keepdims=True)
    return d * lax.rsqrt(var + LN_EPS) * g + b


def _dot(a, b):
    return jnp.dot(a, b, preferred_element_type=F32)


def _dot_nt(a, b):
    return lax.dot_general(a, b, (((1,), (1,)), ((), ())), preferred_element_type=F32)


class _CastPlan:
    def __init__(self, weights, steps, step_of):
        self.weights = list(weights)
        self.ranges, self.specs = [], []
        per_weight = steps // len(self.weights)
        start = 0
        for w in self.weights:
            rows = w.shape[0]
            nblocks = next(nb for nb in range(per_weight, 0, -1)
                           if rows % nb == 0 and (rows // nb) % 16 == 0)
            self.ranges.append((start, nblocks))
            self.specs.append(self._spec(w, rows // nblocks, start, nblocks, step_of))
            start += nblocks
        assert start <= steps

    @staticmethod
    def _spec(w, block_rows, start, nblocks, step_of):
        idx = lambda *g: (jnp.clip(step_of(*g) - start, 0, nblocks - 1), 0)
        return pl.BlockSpec((block_rows, w.shape[1]), idx)

    def out_shapes(self):
        return [jax.ShapeDtypeStruct(w.shape, BF16) for w in self.weights]

    def emit(self, step, w_refs, wb_refs):
        for (start, nblocks), w_ref, wb_ref in zip(self.ranges, w_refs, wb_refs):
            @pl.when((step >= start) & (step < start + nblocks))
            def _(w_ref=w_ref, wb_ref=wb_ref):
                wb_ref[...] = w_ref[...].astype(BF16)


def _dot_tn(a, b):
    return lax.dot_general(a, b, (((0,), (0,)), ((), ())), preferred_element_type=F32)


IN_PROJ_GROUP_ROWS = 256


def _in_proj_kernel(x_ref, w_ref, cos_ref, sin_ref, o_ref, xb_ref, *, tn):
    j = pl.program_id(1)

    @pl.when(j == 0)
    def _():
        xb_ref[...] = x_ref[...].astype(BF16)

    n_q = OFF_RK // tn
    n_rope = OFF_RV // tn
    scale = jnp.where((j >= n_q) & (j < n_rope), RET_D ** -0.5, 1.0).astype(F32)
    tm = x_ref.shape[0]
    half = IN_PROJ_GROUP_ROWS
    for mh in range(tm // IN_PROJ_GROUP_ROWS):
        rows = slice(mh * half, (mh + 1) * half)
        cos = cos_ref[rows, :]
        sin = sin_ref[rows, :]
        for s in range(tn // MXU_COLS):
            acc = _dot(xb_ref[rows, :], w_ref[:, s * MXU_COLS:(s + 1) * MXU_COLS])
            for c in range(MXU_COLS // RET_D):
                a = acc[:, c * RET_D:(c + 1) * RET_D]
                r = pltpu.roll(a, RET_D // 2, 1)
                lo = s * MXU_COLS + c * RET_D
                o_ref[rows, lo:lo + RET_D] = ((a * cos + r * sin) * scale).astype(o_ref.dtype)


def _in_proj(x2d, w_bf, cosf, sinf, seq, *, tm=2048, tn=512):
    m, d = x2d.shape
    n = w_bf.shape[1]
    pos_blocks = seq // tm
    n_rope = OFF_RV // tn
    cosf = jnp.concatenate([cosf, jnp.ones((tm, RET_D), F32)], axis=0)
    sinf = jnp.concatenate([sinf, jnp.zeros((tm, RET_D), F32)], axis=0)
    tbl_idx = lambda i, j: (jnp.where(j < n_rope, i % pos_blocks, pos_blocks), 0)
    return pl.pallas_call(
        functools.partial(_in_proj_kernel, tn=tn),
        out_shape=jax.ShapeDtypeStruct((m, n), BF16),
        grid=(m // tm, n // tn),
        in_specs=[
            pl.BlockSpec((tm, d), lambda i, j: (i, 0)),
            pl.BlockSpec((d, tn), lambda i, j: (0, j)),
            pl.BlockSpec((tm, RET_D), tbl_idx),
            pl.BlockSpec((tm, RET_D), tbl_idx),
        ],
        out_specs=pl.BlockSpec((tm, tn), lambda i, j: (i, j)),
        scratch_shapes=[pltpu.VMEM((tm, d), BF16)],
        compiler_params=_cparams(("parallel", "arbitrary")),
        name="in_proj_rope",
    )(x2d, w_bf, cosf, sinf)


def _retention_kernel(cdec_ref, q_ref, k_ref, v_ref, g_ref, dmat_ref, kdec_ref, qdec_ref,
                      gn_ref, *rest, chunks, casts):
    nw = len(casts.weights)
    w_refs, o_ref, wb_refs, state_ref = rest[:nw], rest[nw], rest[nw + 1:2 * nw + 1], rest[2 * nw + 1]
    n = pl.program_id(1)
    casts.emit(pl.program_id(0) * pl.num_programs(1) + n, w_refs, wb_refs)
    c_sz = RET_CHUNK

    @pl.when(n == 0)
    def _():
        state_ref[...] = jnp.zeros_like(state_ref)

    units = [(h, c) for h in range(RET_HEADS) for c in range(chunks)]
    rows_of = lambda c: slice(c * c_sz, (c + 1) * c_sz)
    cols_of = lambda h: slice(h * RET_D, (h + 1) * RET_D)
    scores = {(h, c): (_dot_nt(q_ref[rows_of(c), cols_of(h)], k_ref[rows_of(c), cols_of(h)])
                       * dmat_ref[h]).astype(BF16) for h, c in units}
    kvs = {(h, c): _dot_tn((k_ref[rows_of(c), cols_of(h)].astype(F32) * kdec_ref[h]).astype(BF16),
                           v_ref[rows_of(c), cols_of(h)]) for h, c in units}
    prevs = {}
    for h in range(RET_HEADS):
        state = state_ref[h]
        cdec = cdec_ref[h]
        for c in range(chunks):
            prevs[(h, c)] = state.astype(BF16)
            state = state * cdec + kvs[(h, c)]
        state_ref[h] = state
    for h, c in units:
        rows, cols = rows_of(c), cols_of(h)
        o = _dot(scores[(h, c)], v_ref[rows, cols]) + _dot(q_ref[rows, cols], prevs[(h, c)]) * qdec_ref[h]
        mu = jnp.mean(o, axis=-1, keepdims=True)
        dlt = o - mu
        var = jnp.mean(dlt * dlt, axis=-1, keepdims=True)
        on = dlt * lax.rsqrt(var + LN_EPS) * gn_ref[:, cols]
        g = g_ref[rows, cols].astype(F32)
        o_ref[rows, cols] = (g * jax.nn.sigmoid(g) * on).astype(o_ref.dtype)


def _retention(proj, cdec, dmat, kdec, qdec, gn, batch, seq, cast_ws, *, rows=512):
    m = proj.shape[0]
    nblk = seq // rows
    spec = lambda off: pl.BlockSpec((rows, RET_W), lambda b, n: (b * nblk + n, off // RET_W))
    table = lambda t: pl.BlockSpec(t.shape, lambda b, n: (0,) * t.ndim)
    casts = _CastPlan(cast_ws, batch * nblk, lambda b, n: b * nblk + n)
    outs = pl.pallas_call(
        functools.partial(_retention_kernel, chunks=rows // RET_CHUNK, casts=casts),
        out_shape=[jax.ShapeDtypeStruct((m, RET_W), BF16)] + casts.out_shapes(),
        grid=(batch, nblk),
        in_specs=[
            pl.BlockSpec(memory_space=pltpu.SMEM),
            spec(OFF_RQ), spec(OFF_RK), spec(OFF_RV), spec(OFF_RG),
            table(dmat), table(kdec), table(qdec), table(gn),
        ] + casts.specs,
        out_specs=[pl.BlockSpec((rows, RET_W), lambda b, n: (b * nblk + n, 0))] + casts.specs,
        scratch_shapes=[pltpu.VMEM((RET_HEADS, RET_D, RET_D), F32)],
        compiler_params=_cparams(("arbitrary", "arbitrary")),
        name="retention",
    )(cdec, proj, proj, proj, proj, dmat, kdec, qdec, gn, *casts.weights)
    return outs[0], outs[1:]


def _bias_kernel(rel_ref, bucket_ref, o_ref):
    h = pl.program_id(0)
    bucket = bucket_ref[...]
    l = bucket.shape[0]

    def body(b, acc):
        return jnp.where(bucket == b, rel_ref[b, h], acc)

    bias = lax.fori_loop(0, REL_BUCKETS, body, jnp.zeros(bucket.shape, F32))
    i = lax.broadcasted_iota(jnp.int32, bucket.shape, 0)
    j = lax.broadcasted_iota(jnp.int32, bucket.shape, 1)
    dist = i + l - j
    o_ref[0] = jnp.where((dist >= 0) & (dist < SWA_WINDOW), bias, -jnp.inf)


def _bias_table(rel_bias, bucket):
    l, l2 = bucket.shape
    return pl.pallas_call(
        _bias_kernel,
        out_shape=jax.ShapeDtypeStruct((SWA_HEADS, l, l2), F32),
        grid=(SWA_HEADS,),
        in_specs=[pl.BlockSpec(memory_space=pltpu.SMEM),
                  pl.BlockSpec((l, l2), lambda h: (0, 0))],
        out_specs=pl.BlockSpec((1, l, l2), lambda h: (h, 0, 0)),
        compiler_params=pltpu.CompilerParams(dimension_semantics=("arbitrary",)),
        name="t5_bias_table",
    )(rel_bias, bucket)


SWA_QK_AHEAD = 2


def _swap_lane_halves(x):
    u = pltpu.bitcast(x, jnp.uint32)
    return pltpu.bitcast(pltpu.roll(u, LANES // 2, 1), x.dtype)


def _swa_kernel(sink_ref, q_ref, kc_ref, vc_ref, kp_ref, vp_ref, bias_ref, first_ref, *rest,
                blocks, casts):
    nw = len(casts.weights)
    w_refs, o_ref, wb_refs = rest[:nw], rest[nw], rest[nw + 1:]
    casts.emit(pl.program_id(0) * pl.num_programs(1) + pl.program_id(1), w_refs, wb_refs)
    l = SWA_BLOCK
    grp = SWA_HEADS // SWA_KV_HEADS
    low = lax.broadcasted_iota(jnp.int32, (2 * l, LANES), 1) < SWA_HD
    low_q = lax.broadcasted_iota(jnp.int32, (l, LANES), 1) < SWA_HD
    zero = jnp.zeros((2 * l, LANES), BF16)
    kv2 = {}

    def kv_pair(blk, kh):
        if (blk, kh) in kv2:
            return kv2[(blk, kh)]
        r0 = blk * l
        t, sub = divmod(kh, LANES // SWA_HD)
        cols = slice(t * LANES, (t + 1) * LANES)
        if blk == 0:
            kt = jnp.concatenate([kp_ref[:, cols], kc_ref[0:l, cols]], axis=0)
            vt = jnp.concatenate([vp_ref[:, cols], vc_ref[0:l, cols]], axis=0)
        else:
            kt = kc_ref[r0 - l:r0 + l, cols]
            vt = vc_ref[r0 - l:r0 + l, cols]
        kt = kt * (SWA_HD ** -0.5)
        kr = _swap_lane_halves(kt)
        vr = _swap_lane_halves(vt)
        k_lo, k_hi = (kt, kr) if sub == 0 else (kr, kt)
        v_lo, v_hi = (vt, vr) if sub == 0 else (vr, vt)
        k2 = jnp.concatenate([jnp.where(low, k_lo, zero), jnp.where(low, zero, k_hi)], axis=0)
        v2 = jnp.concatenate([jnp.where(low, v_lo, zero), jnp.where(low, zero, v_hi)], axis=0)
        kv2[(blk, kh)] = (k2, v2)
        return k2, v2

    tasks = [(blk, pair) for blk in range(blocks) for pair in range(SWA_HEADS // 2)]

    def qk(task):
        blk, pair = task
        k2, _ = kv_pair(blk, pair // (grp // 2))
        q2 = q_ref[blk * l:(blk + 1) * l, pair * LANES:(pair + 1) * LANES]
        return _dot_nt(q2, k2)

    def finish(task, logits2):
        blk, pair = task
        _, v2 = kv_pair(blk, pair // (grp // 2))
        ps, dens = [], []
        for s in range(2):
            hd = 2 * pair + s
            lg = logits2[:, s * 2 * l:(s + 1) * 2 * l] + bias_ref[hd]
            if blk == 0:
                lg = lg + first_ref[0]
            sink = sink_ref[hd]
            mx = jnp.maximum(jnp.max(lg, axis=-1, keepdims=True), sink)
            p = jnp.exp(lg - mx)
            dens.append(jnp.sum(p, axis=-1, keepdims=True) + jnp.exp(sink - mx))
            ps.append(p.astype(BF16))
        out2 = _dot(jnp.concatenate(ps, axis=1), v2)
        den2 = jnp.where(low_q, dens[0], dens[1])
        o_ref[blk * l:(blk + 1) * l, pair * LANES:(pair + 1) * LANES] = (out2 / den2).astype(o_ref.dtype)

    pending = [qk(t) for t in tasks[:SWA_QK_AHEAD]]
    for n, task in enumerate(tasks):
        if n + SWA_QK_AHEAD < len(tasks):
            pending.append(qk(tasks[n + SWA_QK_AHEAD]))
        finish(task, pending.pop(0))


def _swa(proj, sinks, bias, batch, seq, cast_ws, *, rows=256):
    m = proj.shape[0]
    nblk = seq // rows
    per = rows // SWA_BLOCK
    kcol, vcol = OFF_SK // SWA_KV_W, OFF_SV // SWA_KV_W
    l = SWA_BLOCK
    first = jnp.stack([jnp.zeros((l, 2 * l), F32),
                       jnp.where(jnp.arange(2 * l)[None, :] < l, -jnp.inf, 0.0)
                       * jnp.ones((l, 1), F32)])

    def prev_idx(b, n):
        return jnp.maximum((b * nblk + n) * per - 1, 0)

    casts = _CastPlan(cast_ws, batch * nblk, lambda b, n: b * nblk + n)
    outs = pl.pallas_call(
        functools.partial(_swa_kernel, blocks=per, casts=casts),
        out_shape=[jax.ShapeDtypeStruct((m, SWA_W), BF16)] + casts.out_shapes(),
        grid=(batch, nblk),
        in_specs=[
            pl.BlockSpec(memory_space=pltpu.SMEM),
            pl.BlockSpec((rows, SWA_W), lambda b, n: (b * nblk + n, OFF_SQ // SWA_W)),
            pl.BlockSpec((rows, SWA_KV_W), lambda b, n: (b * nblk + n, kcol)),
            pl.BlockSpec((rows, SWA_KV_W), lambda b, n: (b * nblk + n, vcol)),
            pl.BlockSpec((SWA_BLOCK, SWA_KV_W), lambda b, n: (prev_idx(b, n), kcol)),
            pl.BlockSpec((SWA_BLOCK, SWA_KV_W), lambda b, n: (prev_idx(b, n), vcol)),
            pl.BlockSpec((SWA_HEADS, SWA_BLOCK, 2 * SWA_BLOCK), lambda b, n: (0, 0, 0)),
            pl.BlockSpec((1, SWA_BLOCK, 2 * SWA_BLOCK), lambda b, n: (jnp.where(n == 0, 1, 0), 0, 0)),
        ] + casts.specs,
        out_specs=[pl.BlockSpec((rows, SWA_W), lambda b, n: (b * nblk + n, 0))] + casts.specs,
        compiler_params=_cparams(("arbitrary", "arbitrary")),
        name="swa_sink_attention",
    )(sinks, proj, proj, proj, proj, proj, bias, first, *casts.weights)
    return outs[0], outs[1:]


LN_GROUP_ROWS = 256


def _proj_res_ln_kernel(a1_ref, a2_ref, w1_ref, w2_ref, x_ref, g_ref, b_ref, o_ref):
    for r in range(x_ref.shape[0] // LN_GROUP_ROWS):
        rows = slice(r * LN_GROUP_ROWS, (r + 1) * LN_GROUP_ROWS)
        y = _dot(a1_ref[rows, :], w1_ref[...]) + _dot(a2_ref[rows, :], w2_ref[...])
        o_ref[rows, :] = _layer_norm(DN_ALPHA * x_ref[rows, :] + y, g_ref[...], b_ref[...])


def _proj_res_ln(a1, a2, col1, col2, w_bf, x2d, g, b, *, tm=1024):
    m, d = x2d.shape
    kh = w_bf.shape[0] // 2
    return pl.pallas_call(
        _proj_res_ln_kernel,
        out_shape=jax.ShapeDtypeStruct((m, d), F32),
        grid=(m // tm,),
        in_specs=[
            pl.BlockSpec((tm, kh), lambda i: (i, col1)),
            pl.BlockSpec((tm, kh), lambda i: (i, col2)),
            pl.BlockSpec((kh, d), lambda i: (0, 0), pipeline_mode=pl.Buffered(1)),
            pl.BlockSpec((kh, d), lambda i: (1, 0), pipeline_mode=pl.Buffered(1)),
            pl.BlockSpec((tm, d), lambda i: (i, 0)),
            pl.BlockSpec((1, d), lambda i: (0, 0)),
            pl.BlockSpec((1, d), lambda i: (0, 0)),
        ],
        out_specs=pl.BlockSpec((tm, d), lambda i: (i, 0)),
        compiler_params=_cparams(("parallel",)),
        name="proj_residual_ln",
    )(a1, a2, w_bf, w_bf, x2d, g, b)


def _matmul_kernel(a_ref, w_ref, o_ref):
    o_ref[...] = _dot(a_ref[...], w_ref[...]).astype(o_ref.dtype)


def _matmul(a_bf, w_bf, *, tm, tn):
    m, k = a_bf.shape
    n = w_bf.shape[1]
    return pl.pallas_call(
        _matmul_kernel,
        out_shape=jax.ShapeDtypeStruct((m, n), BF16),
        grid=(m // tm, n // tn),
        in_specs=[pl.BlockSpec((tm, k), lambda i, j: (i, 0)),
                  pl.BlockSpec((k, tn), lambda i, j: (0, j))],
        out_specs=pl.BlockSpec((tm, tn), lambda i, j: (i, j)),
        compiler_params=_cparams(("parallel", "arbitrary")),
        name="matmul",
    )(a_bf, w_bf)


XA_GROUP_ROWS = 256


def _xattn_kernel(x_ref, wq_ref, k_ref, v_ref, o_ref, *, hd):
    groups = x_ref.shape[0] // XA_GROUP_ROWS
    tasks = [(r, h) for r in range(groups) for h in range(XA_HEADS)]
    xb = {}

    def qproj(task):
        r, h = task
        if r not in xb:
            xb[r] = x_ref[r * XA_GROUP_ROWS:(r + 1) * XA_GROUP_ROWS, :].astype(BF16)
        return _dot(xb[r], wq_ref[:, h * hd:(h + 1) * hd]).astype(BF16)

    q_next = qproj(tasks[0])
    for n, (r, h) in enumerate(tasks):
        q = q_next
        if n + 1 < len(tasks):
            q_next = qproj(tasks[n + 1])
        cols = slice(h * hd, (h + 1) * hd)
        logits = _dot_nt(q, k_ref[:, cols]) * (hd ** -0.5)
        mx = jnp.max(logits, axis=-1, keepdims=True)
        p = jnp.exp(logits - mx)
        den = jnp.sum(p, axis=-1, keepdims=True)
        rows = slice(r * XA_GROUP_ROWS, (r + 1) * XA_GROUP_ROWS)
        o_ref[rows, cols] = (_dot(p.astype(BF16), v_ref[:, cols]) / den).astype(o_ref.dtype)


def _xattn(x2d, wq_bf, kv, seq, mem_len, *, tm=1024):
    m, d = x2d.shape
    hd = d // XA_HEADS
    per_b = seq // tm
    return pl.pallas_call(
        functools.partial(_xattn_kernel, hd=hd),
        out_shape=jax.ShapeDtypeStruct((m, d), BF16),
        grid=(m // tm,),
        in_specs=[
            pl.BlockSpec((tm, d), lambda i: (i, 0)),
            pl.BlockSpec((d, d), lambda i: (0, 0), pipeline_mode=pl.Buffered(1)),
            pl.BlockSpec((mem_len, d), lambda i: (i // per_b, 0)),
            pl.BlockSpec((mem_len, d), lambda i: (i // per_b, 1)),
        ],
        out_specs=pl.BlockSpec((tm, d), lambda i: (i, 0)),
        compiler_params=_cparams(("parallel",)),
        name="memory_cross_attention",
    )(x2d, wq_bf, kv, kv)


FFN_HALO = 16
FFN_GROUP_ROWS = 256
FFN_CHUNK = 256


def _ffn_kernel(x_ref, xh_ref, wu_ref, wg_ref, cw_ref, cb_ref, wd_ref, g_ref, b_ref, o_ref,
                xb_ref, *, tm, per_b):
    i = pl.program_id(0)
    f = pl.program_id(1)
    nf = pl.num_programs(1)

    groups = tm // FFN_GROUP_ROWS

    def body(first, last):
        cw = cw_ref[...]
        cb = cb_ref[...]

        def up(r):
            r0 = r * FFN_GROUP_ROWS
            if first:
                xb_ref[FFN_HALO + r0:FFN_HALO + r0 + FFN_GROUP_ROWS, :] = (
                    x_ref[r0:r0 + FFN_GROUP_ROWS, :].astype(BF16))
            u = _dot(xb_ref[FFN_HALO + r0:FFN_HALO + r0 + FFN_GROUP_ROWS, :], wu_ref[...])
            ge = _dot(xb_ref[r0:r0 + FFN_GROUP_ROWS + FFN_HALO, :], wg_ref[...])
            return u, ge

        if first:
            halo = jnp.where(i % per_b == 0, 0.0, xh_ref[...])
            xb_ref[0:FFN_HALO, :] = halo.astype(BF16)
        nxt = up(0)
        for r in range(groups):
            u, ge = nxt
            if r + 1 < groups:
                nxt = up(r + 1)
            gc = cb
            for tap in range(CONV_W):
                lo = FFN_HALO - (CONV_W - 1) + tap
                gc = gc + ge[lo:lo + FFN_GROUP_ROWS, :] * cw[tap:tap + 1, :]
            hcur = (gc * jax.nn.sigmoid(gc) * u).astype(BF16)
            rows = slice(r * FFN_GROUP_ROWS, (r + 1) * FFN_GROUP_ROWS)
            acc = _dot(hcur, wd_ref[...])
            if not first:
                acc = o_ref[rows, :] + acc
            if last:
                acc = _layer_norm(DN_ALPHA * x_ref[rows, :] + acc, g_ref[...], b_ref[...])
            o_ref[rows, :] = acc

    pl.when(f == 0)(functools.partial(body, True, False))
    pl.when((f > 0) & (f < nf - 1))(functools.partial(body, False, False))
    pl.when(f == nf - 1)(functools.partial(body, False, True))


def _ffn(x2d, wup_bf, conv_w, conv_b, wdown_bf, g, b, seq, *, tm=1024, fc=FFN_CHUNK):
    m, d = x2d.shape
    nf = wdown_bf.shape[0] // fc
    assert nf >= 2, "the first and the last hidden chunk are separate code paths"
    per_b = seq // tm
    halo_per_tile = tm // FFN_HALO
    return pl.pallas_call(
        functools.partial(_ffn_kernel, tm=tm, per_b=per_b),
        out_shape=jax.ShapeDtypeStruct((m, d), F32),
        grid=(m // tm, nf),
        in_specs=[
            pl.BlockSpec((tm, d), lambda i, f: (i, 0)),
            pl.BlockSpec((FFN_HALO, d), lambda i, f: (jnp.maximum(i * halo_per_tile - 1, 0), 0)),
            pl.BlockSpec((d, fc), lambda i, f: (0, f)),
            pl.BlockSpec((d, fc), lambda i, f: (0, nf + f)),
            pl.BlockSpec((CONV_W, fc), lambda i, f: (0, f)),
            pl.BlockSpec((1, fc), lambda i, f: (0, f)),
            pl.BlockSpec((fc, d), lambda i, f: (f, 0)),
            pl.BlockSpec((1, d), lambda i, f: (0, 0)),
            pl.BlockSpec((1, d), lambda i, f: (0, 0)),
        ],
        out_specs=pl.BlockSpec((tm, d), lambda i, f: (i, 0)),
        scratch_shapes=[pltpu.VMEM((tm + FFN_HALO, d), BF16)],
        compiler_params=_cparams(("parallel", "arbitrary")),
        name="conv_ffn_ln",
    )(x2d, x2d, wup_bf, wup_bf, conv_w, conv_b, wdown_bf, g, b)


def _rope_tables(seq):
    half = RET_D // 2
    inv = 1.0 / (ROPE_BASE ** (jnp.arange(half, dtype=F32) / half))
    ang = jnp.arange(seq).astype(F32)[:, None] * inv[None, :]
    cos, sin = jnp.cos(ang), jnp.sin(ang)
    return jnp.concatenate([cos, cos], axis=-1), jnp.concatenate([-sin, sin], axis=-1)


def _retention_tables():
    c = RET_CHUNK
    log_gamma = jnp.log1p(-jnp.exp2(-5.0 - jnp.arange(RET_HEADS, dtype=F32)))
    idx = jnp.arange(c, dtype=F32)
    diff = idx[:, None] - idx[None, :]
    dmat = jnp.where(diff[None] >= 0,
                     jnp.exp(log_gamma[:, None, None] * jnp.maximum(diff, 0.0)[None]), 0.0)
    kdec = jnp.exp(log_gamma[:, None] * (c - 1 - idx)[None, :])
    qdec = jnp.exp(log_gamma[:, None] * (idx + 1.0)[None, :])
    cdec = jnp.exp(log_gamma * c)
    bc = lambda t: jnp.broadcast_to(t[:, :, None], (RET_HEADS, c, RET_D))
    return dmat, bc(kdec), bc(qdec), cdec


def _t5_bucket_table():
    l = SWA_BLOCK
    dist = jnp.maximum(jnp.arange(l)[:, None] + l - jnp.arange(2 * l)[None, :], 0)
    max_exact = REL_BUCKETS // 2
    nf = jnp.maximum(dist, 1).astype(F32)
    large = max_exact + (jnp.log(nf / max_exact) / math.log(REL_MAX_DIST / max_exact)
                         * (REL_BUCKETS - max_exact)).astype(jnp.int32)
    large = jnp.minimum(large, REL_BUCKETS - 1)
    return jnp.where(dist < max_exact, dist, large).astype(jnp.int32)


def kernel(x, mem, w_in, ret_gn_g, swa_sinks, rel_bias, w_o, ln1_g, ln1_b, xa_wq, xa_wkv, xa_wo,
           ln2_g, ln2_b, ffn_w_up, ffn_conv_w, ffn_conv_b, ffn_w_down, ln3_g, ln3_b):
    batch, seq, d = x.shape
    mem_len = mem.shape[1]
    assert w_in.shape[0] == DEPTH and w_in.shape[2] == IN_W
    assert seq % 1024 == 0 and d % LANES == 0

    cosf, sinf = _rope_tables(seq)
    dmat, kdec, qdec, cdec = _retention_tables()
    bucket = _t5_bucket_table()
    bias = _bias_table(rel_bias, bucket)
    row = lambda t: t.reshape(1, -1)

    x2d = x.reshape(batch * seq, d)
    mem_bf = mem.reshape(batch * mem_len, d).astype(BF16)
    for l in range(DEPTH):
        proj = _in_proj(x2d, w_in[l].astype(BF16), cosf, sinf, seq)
        o_r, (wo_bf, wkv_bf, wq_bf, xwo_bf) = _retention(
            proj, cdec, dmat, kdec, qdec, row(ret_gn_g[l]), batch, seq,
            [w_o[l], xa_wkv[l], xa_wq[l], xa_wo[l]])
        o_s, (wup_bf, wdown_bf) = _swa(proj, swa_sinks[l], bias, batch, seq,
                                       [ffn_w_up[l], ffn_w_down[l]])
        x2d = _proj_res_ln(o_r, o_s, 0, 0, wo_bf, x2d, row(ln1_g[l]), row(ln1_b[l]))

        kv = _matmul(mem_bf, wkv_bf, tm=batch * mem_len, tn=1024)
        xa = _xattn(x2d, wq_bf, kv, seq, mem_len)
        x2d = _proj_res_ln(xa, xa, 0, 1, xwo_bf, x2d, row(ln2_g[l]), row(ln2_b[l]))

        x2d = _ffn(x2d, wup_bf, ffn_conv_w[l], row(ffn_conv_b[l]),
                   wdown_bf, row(ln3_g[l]), row(ln3_b[l]), seq)
    return x2d.reshape(batch, seq, d)
```

```python
import functools
import math

import jax
import jax.numpy as jnp
import numpy as np
from jax import lax
from jax.experimental import pallas as pl
from jax.experimental.pallas import tpu as pltpu

RET_HEADS = 8
RET_D = 128
RET_CHUNK = 128
ROPE_BASE = 10000.0
SWA_HEADS = 16
SWA_KV_HEADS = 4
SWA_HD = 64
SWA_WINDOW = 128
SWA_BLOCK = 128
REL_BUCKETS = 32
REL_MAX_DIST = 128
XA_HEADS = 4
CONV_W = 3
LN_EPS = 1e-5
DEPTH = 1
DN_ALPHA = (2 * DEPTH) ** 0.25

RET_W = RET_HEADS * RET_D
SWA_W = SWA_HEADS * SWA_HD
SWA_KV_W = SWA_KV_HEADS * SWA_HD
OFF_RQ, OFF_RK, OFF_RV, OFF_RG = 0, RET_W, 2 * RET_W, 3 * RET_W
OFF_SQ = 4 * RET_W
OFF_SK = OFF_SQ + SWA_W
OFF_SV = OFF_SK + SWA_KV_W
IN_W = OFF_SV + SWA_KV_W

LANES = 128
MXU_COLS = 256
V7X_VMEM_BYTES = 64 * 1024 * 1024
VMEM_LIMIT = 56 * 1024 * 1024

BF16 = jnp.bfloat16
F32 = jnp.float32


def _cparams(sem, vmem=VMEM_LIMIT):
    return pltpu.CompilerParams(dimension_semantics=sem, vmem_limit_bytes=vmem)


def _layer_norm(y, g, b):
    mu = jnp.mean(y, axis=-1, keepdims=True)
    d = y - mu
    var = jnp.mean(d * d, axis=-1, keepdims=True)
    return d * lax.rsqrt(var + LN_EPS) * g + b


def _dot(a, b):
    return jnp.dot(a, b, preferred_element_type=F32)


def _dot_nt(a, b):
    return lax.dot_general(a, b, (((1,), (1,)), ((), ())), preferred_element_type=F32)


class _CastPlan:
    def __init__(self, weights, steps, step_of):
        self.weights = list(weights)
        self.ranges, self.specs = [], []
        per_weight = steps // len(self.weights)
        start = 0
        for w in self.weights:
            rows = w.shape[0]
            nblocks = next(nb for nb in range(per_weight, 0, -1)
                           if rows % nb == 0 and (rows // nb) % 16 == 0)
            self.ranges.append((start, nblocks))
            self.specs.append(self._spec(w, rows // nblocks, start, nblocks, step_of))
            start += nblocks
        assert start <= steps

    @staticmethod
    def _spec(w, block_rows, start, nblocks, step_of):
        idx = lambda *g: (jnp.clip(step_of(*g) - start, 0, nblocks - 1), 0)
        return pl.BlockSpec((block_rows, w.shape[1]), idx)

    def out_shapes(self):
        return [jax.ShapeDtypeStruct(w.shape, BF16) for w in self.weights]

    def emit(self, step, w_refs, wb_refs):
        for (start, nblocks), w_ref, wb_ref in zip(self.ranges, w_refs, wb_refs):
            @pl.when((step >= start) & (step < start + nblocks))
            def _(w_ref=w_ref, wb_ref=wb_ref):
                wb_ref[...] = w_ref[...].astype(BF16)


def _dot_tn(a, b):
    return lax.dot_general(a, b, (((0,), (0,)), ((), ())), preferred_element_type=F32)


IN_PROJ_GROUP_ROWS = 256


def _in_proj_kernel(x_ref, w_ref, cos_ref, sin_ref, o_ref, xb_ref, *, tn):
    j = pl.program_id(1)

    @pl.when(j == 0)
    def _():
        xb_ref[...] = x_ref[...].astype(BF16)

    n_q = OFF_RK // tn
    n_rope = OFF_RV // tn
    scale = jnp.where((j >= n_q) & (j < n_rope), RET_D ** -0.5, 1.0).astype(F32)
    tm = x_ref.shape[0]
    half = IN_PROJ_GROUP_ROWS
    for mh in range(tm // IN_PROJ_GROUP_ROWS):
        rows = slice(mh * half, (mh + 1) * half)
        cos = cos_ref[rows, :]
        sin = sin_ref[rows, :]
        for s in range(tn // MXU_COLS):
            acc = _dot(xb_ref[rows, :], w_ref[:, s * MXU_COLS:(s + 1) * MXU_COLS])
            for c in range(MXU_COLS // RET_D):
                a = acc[:, c * RET_D:(c + 1) * RET_D]
                r = pltpu.roll(a, RET_D // 2, 1)
                lo = s * MXU_COLS + c * RET_D
                o_ref[rows, lo:lo + RET_D] = ((a * cos + r * sin) * scale).astype(o_ref.dtype)


def _in_proj(x2d, w_bf, cosf, sinf, seq, *, tm=2048, tn=512):
    m, d = x2d.shape
    n = w_bf.shape[1]
    pos_blocks = seq // tm
    n_rope = OFF_RV // tn
    cosf = jnp.concatenate([cosf, jnp.ones((tm, RET_D), F32)], axis=0)
    sinf = jnp.concatenate([sinf, jnp.zeros((tm, RET_D), F32)], axis=0)
    tbl_idx = lambda i, j: (jnp.where(j < n_rope, i % pos_blocks, pos_blocks), 0)
    return pl.pallas_call(
        functools.partial(_in_proj_kernel, tn=tn),
        out_shape=jax.ShapeDtypeStruct((m, n), BF16),
        grid=(m // tm, n // tn),
        in_specs=[
            pl.BlockSpec((tm, d), lambda i, j: (i, 0)),
            pl.BlockSpec((d, tn), lambda i, j: (0, j)),
            pl.BlockSpec((tm, RET_D), tbl_idx),
            pl.BlockSpec((tm, RET_D), tbl_idx),
        ],
        out_specs=pl.BlockSpec((tm, tn), lambda i, j: (i, j)),
        scratch_shapes=[pltpu.VMEM((tm, d), BF16)],
        compiler_params=_cparams(("parallel", "arbitrary")),
        name="in_proj_rope",
    )(x2d, w_bf, cosf, sinf)


def _retention_kernel(cdec_ref, q_ref, k_ref, v_ref, g_ref, dmat_ref, kdec_ref, qdec_ref,
                      gn_ref, *rest, chunks, casts):
    nw = len(casts.weights)
    w_refs, o_ref, wb_refs, state_ref = rest[:nw], rest[nw], rest[nw + 1:2 * nw + 1], rest[2 * nw + 1]
    n = pl.program_id(1)
    casts.emit(pl.program_id(0) * pl.num_programs(1) + n, w_refs, wb_refs)
    c_sz = RET_CHUNK

    @pl.when(n == 0)
    def _():
        state_ref[...] = jnp.zeros_like(state_ref)

    units = [(h, c) for h in range(RET_HEADS) for c in range(chunks)]
    rows_of = lambda c: slice(c * c_sz, (c + 1) * c_sz)
    cols_of = lambda h: slice(h * RET_D, (h + 1) * RET_D)
    scores = {(h, c): (_dot_nt(q_ref[rows_of(c), cols_of(h)], k_ref[rows_of(c), cols_of(h)])
                       * dmat_ref[h]).astype(BF16) for h, c in units}
    kvs = {(h, c): _dot_tn((k_ref[rows_of(c), cols_of(h)].astype(F32) * kdec_ref[h]).astype(BF16),
                           v_ref[rows_of(c), cols_of(h)]) for h, c in units}
    prevs = {}
    for h in range(RET_HEADS):
        state = state_ref[h]
        cdec = cdec_ref[h]
        for c in range(chunks):
            prevs[(h, c)] = state.astype(BF16)
            state = state * cdec + kvs[(h, c)]
        state_ref[h] = state
    for h, c in units:
        rows, cols = rows_of(c), cols_of(h)
        o = _dot(scores[(h, c)], v_ref[rows, cols]) + _dot(q_ref[rows, cols], prevs[(h, c)]) * qdec_ref[h]
        mu = jnp.mean(o, axis=-1, keepdims=True)
        dlt = o - mu
        var = jnp.mean(dlt * dlt, axis=-1, keepdims=True)
        on = dlt * lax.rsqrt(var + LN_EPS) * gn_ref[:, cols]
        g = g_ref[rows, cols].astype(F32)
        o_ref[rows, cols] = (g * jax.nn.sigmoid(g) * on).astype(o_ref.dtype)


def _retention(proj, cdec, dmat, kdec, qdec, gn, batch, seq, cast_ws, *, rows=512):
    m = proj.shape[0]
    nblk = seq // rows
    spec = lambda off: pl.BlockSpec((rows, RET_W), lambda b, n: (b * nblk + n, off // RET_W))
    table = lambda t: pl.BlockSpec(t.shape, lambda b, n: (0,) * t.ndim)
    casts = _CastPlan(cast_ws, batch * nblk, lambda b, n: b * nblk + n)
    outs = pl.pallas_call(
        functools.partial(_retention_kernel, chunks=rows // RET_CHUNK, casts=casts),
        out_shape=[jax.ShapeDtypeStruct((m, RET_W), BF16)] + casts.out_shapes(),
        grid=(batch, nblk),
        in_specs=[
            pl.BlockSpec(memory_space=pltpu.SMEM),
            spec(OFF_RQ), spec(OFF_RK), spec(OFF_RV), spec(OFF_RG),
            table(dmat), table(kdec), table(qdec), table(gn),
        ] + casts.specs,
        out_specs=[pl.BlockSpec((rows, RET_W), lambda b, n: (b * nblk + n, 0))] + casts.specs,
        scratch_shapes=[pltpu.VMEM((RET_HEADS, RET_D, RET_D), F32)],
        compiler_params=_cparams(("arbitrary", "arbitrary")),
        name="retention",
    )(cdec, proj, proj, proj, proj, dmat, kdec, qdec, gn, *casts.weights)
    return outs[0], outs[1:]


def _bias_kernel(rel_ref, bucket_ref, o_ref):
    h = pl.program_id(0)
    bucket = bucket_ref[...]
    l = bucket.shape[1]

    def body(b, acc):
        return jnp.where(bucket == b, rel_ref[b, h], acc)

    bias = lax.fori_loop(0, REL_BUCKETS, body, jnp.zeros(bucket.shape, F32))
    j = lax.broadcasted_iota(jnp.int32, bucket.shape, 0)
    i = lax.broadcasted_iota(jnp.int32, bucket.shape, 1)
    dist = i + l - j
    o_ref[0] = jnp.where((dist >= 0) & (dist < SWA_WINDOW), bias, -jnp.inf)


def _bias_table(rel_bias, bucket):
    l2, l = bucket.shape
    return pl.pallas_call(
        _bias_kernel,
        out_shape=jax.ShapeDtypeStruct((SWA_HEADS, l2, l), F32),
        grid=(SWA_HEADS,),
        in_specs=[pl.BlockSpec(memory_space=pltpu.SMEM),
                  pl.BlockSpec((l2, l), lambda h: (0, 0))],
        out_specs=pl.BlockSpec((1, l2, l), lambda h: (h, 0, 0)),
        compiler_params=pltpu.CompilerParams(dimension_semantics=("arbitrary",)),
        name="t5_bias_table",
    )(rel_bias, bucket)


SWA_QK_AHEAD = 2


def _swap_lane_halves(x):
    u = pltpu.bitcast(x, jnp.uint32)
    return pltpu.bitcast(pltpu.roll(u, LANES // 2, 1), x.dtype)


def _swa_kernel(sink_ref, q_ref, kc_ref, vc_ref, kp_ref, vp_ref, bias_ref, first_ref, *rest,
                blocks, casts):
    nw = len(casts.weights)
    w_refs, o_ref, wb_refs = rest[:nw], rest[nw], rest[nw + 1:]
    casts.emit(pl.program_id(0) * pl.num_programs(1) + pl.program_id(1), w_refs, wb_refs)
    l = SWA_BLOCK
    grp = SWA_HEADS // SWA_KV_HEADS
    per_tile = LANES // SWA_HD
    low = lax.broadcasted_iota(jnp.int32, (2 * l, LANES), 1) < SWA_HD
    zero_k = jnp.zeros((2 * l, LANES), BF16)
    zero_v = jnp.zeros((SWA_HD, 2 * l), BF16)
    tiles = {}

    def kv_tile(blk, t):
        if (blk, t) not in tiles:
            r0 = blk * l
            cols = slice(t * LANES, (t + 1) * LANES)
            if blk == 0:
                kt = jnp.concatenate([kp_ref[:, cols], kc_ref[0:l, cols]], axis=0)
                vt = jnp.concatenate([vp_ref[:, cols], vc_ref[0:l, cols]], axis=0)
            else:
                kt = kc_ref[r0 - l:r0 + l, cols]
                vt = vc_ref[r0 - l:r0 + l, cols]
            kt = kt * (SWA_HD ** -0.5)
            tiles[(blk, t)] = (kt, _swap_lane_halves(kt), vt.astype(F32).T.astype(BF16))
        return tiles[(blk, t)]

    tasks = [(blk, kh) for blk in range(blocks) for kh in range(SWA_KV_HEADS)]

    def qk(task):
        blk, kh = task
        t, sub = divmod(kh, per_tile)
        kt, kr, _ = kv_tile(blk, t)
        k_lo, k_hi = (kt, kr) if sub == 0 else (kr, kt)
        k2 = jnp.concatenate([jnp.where(low, k_lo, zero_k), jnp.where(low, zero_k, k_hi)], axis=0)
        rows = slice(blk * l, (blk + 1) * l)
        qcat = jnp.concatenate([q_ref[rows, (kh * grp // 2 + pp) * LANES:(kh * grp // 2 + pp + 1) * LANES]
                                for pp in range(grp // 2)], axis=0)
        return _dot_nt(k2, qcat)

    def finish(task, st):
        blk, kh = task
        t, sub = divmod(kh, per_tile)
        vh = kv_tile(blk, t)[2][sub * SWA_HD:(sub + 1) * SWA_HD, :]
        v2t = jnp.concatenate([jnp.concatenate([vh, zero_v], axis=1),
                               jnp.concatenate([zero_v, vh], axis=1)], axis=0)
        p_rows, den_rows = [], []
        for s in range(2):
            p_cols, den_cols = [], []
            for pp in range(grp // 2):
                hd = kh * grp + 2 * pp + s
                lg = st[s * 2 * l:(s + 1) * 2 * l, pp * l:(pp + 1) * l] + bias_ref[hd]
                if blk == 0:
                    lg = lg + first_ref[0]
                sink = sink_ref[hd]
                mx = jnp.maximum(jnp.max(lg, axis=0, keepdims=True), sink)
                p = jnp.exp(lg - mx)
                den = jnp.sum(p, axis=0, keepdims=True) + jnp.exp(sink - mx)
                p_cols.append(p.astype(BF16))
                den_cols.append(jnp.broadcast_to(den, (SWA_HD, l)))
            p_rows.append(jnp.concatenate(p_cols, axis=1))
            den_rows.append(jnp.concatenate(den_cols, axis=1))
        out_t = _dot(v2t, jnp.concatenate(p_rows, axis=0))
        out_t = out_t / jnp.concatenate(den_rows, axis=0)
        out = jnp.concatenate([out_t[:, pp * l:(pp + 1) * l].T for pp in range(grp // 2)], axis=1)
        o_ref[blk * l:(blk + 1) * l, kh * grp * SWA_HD:(kh + 1) * grp * SWA_HD] = out.astype(o_ref.dtype)

    pending = [qk(t) for t in tasks[:SWA_QK_AHEAD]]
    for n, task in enumerate(tasks):
        if n + SWA_QK_AHEAD < len(tasks):
            pending.append(qk(tasks[n + SWA_QK_AHEAD]))
        finish(task, pending.pop(0))


def _swa(proj, sinks, bias, batch, seq, cast_ws, *, rows=256):
    m = proj.shape[0]
    nblk = seq // rows
    per = rows // SWA_BLOCK
    kcol, vcol = OFF_SK // SWA_KV_W, OFF_SV // SWA_KV_W
    l = SWA_BLOCK
    first = jnp.stack([jnp.zeros((2 * l, l), F32),
                       jnp.where(jnp.arange(2 * l)[:, None] < l, -jnp.inf, 0.0)
                       * jnp.ones((1, l), F32)])

    def prev_idx(b, n):
        return jnp.maximum((b * nblk + n) * per - 1, 0)

    casts = _CastPlan(cast_ws, batch * nblk, lambda b, n: b * nblk + n)
    outs = pl.pallas_call(
        functools.partial(_swa_kernel, blocks=per, casts=casts),
        out_shape=[jax.ShapeDtypeStruct((m, SWA_W), BF16)] + casts.out_shapes(),
        grid=(batch, nblk),
        in_specs=[
            pl.BlockSpec(memory_space=pltpu.SMEM),
            pl.BlockSpec((rows, SWA_W), lambda b, n: (b * nblk + n, OFF_SQ // SWA_W)),
            pl.BlockSpec((rows, SWA_KV_W), lambda b, n: (b * nblk + n, kcol)),
            pl.BlockSpec((rows, SWA_KV_W), lambda b, n: (b * nblk + n, vcol)),
            pl.BlockSpec((SWA_BLOCK, SWA_KV_W), lambda b, n: (prev_idx(b, n), kcol)),
            pl.BlockSpec((SWA_BLOCK, SWA_KV_W), lambda b, n: (prev_idx(b, n), vcol)),
            pl.BlockSpec((SWA_HEADS, 2 * SWA_BLOCK, SWA_BLOCK), lambda b, n: (0, 0, 0)),
            pl.BlockSpec((1, 2 * SWA_BLOCK, SWA_BLOCK), lambda b, n: (jnp.where(n == 0, 1, 0), 0, 0)),
        ] + casts.specs,
        out_specs=[pl.BlockSpec((rows, SWA_W), lambda b, n: (b * nblk + n, 0))] + casts.specs,
        compiler_params=_cparams(("arbitrary", "arbitrary")),
        name="swa_sink_attention",
    )(sinks, proj, proj, proj, proj, proj, bias, first, *casts.weights)
    return outs[0], outs[1:]


LN_GROUP_ROWS = 256


def _proj_res_ln_kernel(a1_ref, a2_ref, w1_ref, w2_ref, x_ref, g_ref, b_ref, o_ref):
    for r in range(x_ref.shape[0] // LN_GROUP_ROWS):
        rows = slice(r * LN_GROUP_ROWS, (r + 1) * LN_GROUP_ROWS)
        y = _dot(a1_ref[rows, :], w1_ref[...]) + _dot(a2_ref[rows, :], w2_ref[...])
        o_ref[rows, :] = _layer_norm(DN_ALPHA * x_ref[rows, :] + y, g_ref[...], b_ref[...])


def _proj_res_ln(a1, a2, col1, col2, w_bf, x2d, g, b, *, tm=1024):
    m, d = x2d.shape
    kh = w_bf.shape[0] // 2
    return pl.pallas_call(
        _proj_res_ln_kernel,
        out_shape=jax.ShapeDtypeStruct((m, d), F32),
        grid=(m // tm,),
        in_specs=[
            pl.BlockSpec((tm, kh), lambda i: (i, col1)),
            pl.BlockSpec((tm, kh), lambda i: (i, col2)),
            pl.BlockSpec((kh, d), lambda i: (0, 0), pipeline_mode=pl.Buffered(1)),
            pl.BlockSpec((kh, d), lambda i: (1, 0), pipeline_mode=pl.Buffered(1)),
            pl.BlockSpec((tm, d), lambda i: (i, 0)),
            pl.BlockSpec((1, d), lambda i: (0, 0)),
            pl.BlockSpec((1, d), lambda i: (0, 0)),
        ],
        out_specs=pl.BlockSpec((tm, d), lambda i: (i, 0)),
        compiler_params=_cparams(("parallel",)),
        name="proj_residual_ln",
    )(a1, a2, w_bf, w_bf, x2d, g, b)


def _matmul_kernel(a_ref, w_ref, o_ref):
    o_ref[...] = _dot(a_ref[...], w_ref[...]).astype(o_ref.dtype)


def _matmul(a_bf, w_bf, *, tm, tn):
    m, k = a_bf.shape
    n = w_bf.shape[1]
    return pl.pallas_call(
        _matmul_kernel,
        out_shape=jax.ShapeDtypeStruct((m, n), BF16),
        grid=(m // tm, n // tn),
        in_specs=[pl.BlockSpec((tm, k), lambda i, j: (i, 0)),
                  pl.BlockSpec((k, tn), lambda i, j: (0, j))],
        out_specs=pl.BlockSpec((tm, tn), lambda i, j: (i, j)),
        compiler_params=_cparams(("parallel", "arbitrary")),
        name="matmul",
    )(a_bf, w_bf)


XA_GROUP_ROWS = 256


def _xattn_kernel(x_ref, wq_ref, k_ref, v_ref, o_ref, *, hd):
    groups = x_ref.shape[0] // XA_GROUP_ROWS
    tasks = [(r, h) for r in range(groups) for h in range(XA_HEADS)]
    xb = {}

    def qproj(task):
        r, h = task
        if r not in xb:
            xb[r] = x_ref[r * XA_GROUP_ROWS:(r + 1) * XA_GROUP_ROWS, :].astype(BF16)
        return _dot(xb[r], wq_ref[:, h * hd:(h + 1) * hd]).astype(BF16)

    q_next = qproj(tasks[0])
    for n, (r, h) in enumerate(tasks):
        q = q_next
        if n + 1 < len(tasks):
            q_next = qproj(tasks[n + 1])
        cols = slice(h * hd, (h + 1) * hd)
        logits = _dot_nt(q, k_ref[:, cols]) * (hd ** -0.5)
        mx = jnp.max(logits, axis=-1, keepdims=True)
        p = jnp.exp(logits - mx)
        den = jnp.sum(p, axis=-1, keepdims=True)
        rows = slice(r * XA_GROUP_ROWS, (r + 1) * XA_GROUP_ROWS)
        o_ref[rows, cols] = (_dot(p.astype(BF16), v_ref[:, cols]) / den).astype(o_ref.dtype)


def _xattn(x2d, wq_bf, kv, seq, mem_len, *, tm=1024):
    m, d = x2d.shape
    hd = d // XA_HEADS
    per_b = seq // tm
    return pl.pallas_call(
        functools.partial(_xattn_kernel, hd=hd),
        out_shape=jax.ShapeDtypeStruct((m, d), BF16),
        grid=(m // tm,),
        in_specs=[
            pl.BlockSpec((tm, d), lambda i: (i, 0)),
            pl.BlockSpec((d, d), lambda i: (0, 0), pipeline_mode=pl.Buffered(1)),
            pl.BlockSpec((mem_len, d), lambda i: (i // per_b, 0)),
            pl.BlockSpec((mem_len, d), lambda i: (i // per_b, 1)),
        ],
        out_specs=pl.BlockSpec((tm, d), lambda i: (i, 0)),
        compiler_params=_cparams(("parallel",)),
        name="memory_cross_attention",
    )(x2d, wq_bf, kv, kv)


FFN_HALO = 16
FFN_GROUP_ROWS = 256
FFN_CHUNK = 256


def _ffn_kernel(x_ref, xh_ref, wu_ref, wg_ref, cw_ref, cb_ref, wd_ref, g_ref, b_ref, o_ref,
                xb_ref, *, tm, per_b):
    i = pl.program_id(0)
    f = pl.program_id(1)
    nf = pl.num_programs(1)

    groups = tm // FFN_GROUP_ROWS

    def body(first, last):
        cw = cw_ref[...]
        cb = cb_ref[...]

        def up(r):
            r0 = r * FFN_GROUP_ROWS
            if first:
                xb_ref[FFN_HALO + r0:FFN_HALO + r0 + FFN_GROUP_ROWS, :] = (
                    x_ref[r0:r0 + FFN_GROUP_ROWS, :].astype(BF16))
            u = _dot(xb_ref[FFN_HALO + r0:FFN_HALO + r0 + FFN_GROUP_ROWS, :], wu_ref[...])
            ge = _dot(xb_ref[r0:r0 + FFN_GROUP_ROWS + FFN_HALO, :], wg_ref[...])
            return u, ge

        if first:
            halo = jnp.where(i % per_b == 0, 0.0, xh_ref[...])
            xb_ref[0:FFN_HALO, :] = halo.astype(BF16)
        nxt = up(0)
        for r in range(groups):
            u, ge = nxt
            if r + 1 < groups:
                nxt = up(r + 1)
            gc = cb
            for tap in range(CONV_W):
                lo = FFN_HALO - (CONV_W - 1) + tap
                gc = gc + ge[lo:lo + FFN_GROUP_ROWS, :] * cw[tap:tap + 1, :]
            hcur = (gc * jax.nn.sigmoid(gc) * u).astype(BF16)
            rows = slice(r * FFN_GROUP_ROWS, (r + 1) * FFN_GROUP_ROWS)
            acc = _dot(hcur, wd_ref[...])
            if not first:
                acc = o_ref[rows, :] + acc
            if last:
                acc = _layer_norm(DN_ALPHA * x_ref[rows, :] + acc, g_ref[...], b_ref[...])
            o_ref[rows, :] = acc

    pl.when(f == 0)(functools.partial(body, True, False))
    pl.when((f > 0) & (f < nf - 1))(functools.partial(body, False, False))
    pl.when(f == nf - 1)(functools.partial(body, False, True))


def _ffn(x2d, wup_bf, conv_w, conv_b, wdown_bf, g, b, seq, *, tm=1024, fc=FFN_CHUNK):
    m, d = x2d.shape
    nf = wdown_bf.shape[0] // fc
    assert nf >= 2, "the first and the last hidden chunk are separate code paths"
    per_b = seq // tm
    halo_per_tile = tm // FFN_HALO
    return pl.pallas_call(
        functools.partial(_ffn_kernel, tm=tm, per_b=per_b),
        out_shape=jax.ShapeDtypeStruct((m, d), F32),
        grid=(m // tm, nf),
        in_specs=[
            pl.BlockSpec((tm, d), lambda i, f: (i, 0)),
            pl.BlockSpec((FFN_HALO, d), lambda i, f: (jnp.maximum(i * halo_per_tile - 1, 0), 0)),
            pl.BlockSpec((d, fc), lambda i, f: (0, f)),
            pl.BlockSpec((d, fc), lambda i, f: (0, nf + f)),
            pl.BlockSpec((CONV_W, fc), lambda i, f: (0, f)),
            pl.BlockSpec((1, fc), lambda i, f: (0, f)),
            pl.BlockSpec((fc, d), lambda i, f: (f, 0)),
            pl.BlockSpec((1, d), lambda i, f: (0, 0)),
            pl.BlockSpec((1, d), lambda i, f: (0, 0)),
        ],
        out_specs=pl.BlockSpec((tm, d), lambda i, f: (i, 0)),
        scratch_shapes=[pltpu.VMEM((tm + FFN_HALO, d), BF16)],
        compiler_params=_cparams(("parallel", "arbitrary")),
        name="conv_ffn_ln",
    )(x2d, x2d, wup_bf, wup_bf, conv_w, conv_b, wdown_bf, g, b)


def _rope_tables(seq):
    half = RET_D // 2
    inv = 1.0 / (ROPE_BASE ** (jnp.arange(half, dtype=F32) / half))
    ang = jnp.arange(seq).astype(F32)[:, None] * inv[None, :]
    cos, sin = jnp.cos(ang), jnp.sin(ang)
    return jnp.concatenate([cos, cos], axis=-1), jnp.concatenate([-sin, sin], axis=-1)


def _retention_tables():
    c = RET_CHUNK
    log_gamma = jnp.log1p(-jnp.exp2(-5.0 - jnp.arange(RET_HEADS, dtype=F32)))
    idx = jnp.arange(c, dtype=F32)
    diff = idx[:, None] - idx[None, :]
    dmat = jnp.where(diff[None] >= 0,
                     jnp.exp(log_gamma[:, None, None] * jnp.maximum(diff, 0.0)[None]), 0.0)
    kdec = jnp.exp(log_gamma[:, None] * (c - 1 - idx)[None, :])
    qdec = jnp.exp(log_gamma[:, None] * (idx + 1.0)[None, :])
    cdec = jnp.exp(log_gamma * c)
    bc = lambda t: jnp.broadcast_to(t[:, :, None], (RET_HEADS, c, RET_D))
    return dmat, bc(kdec), bc(qdec), cdec


def _t5_bucket_table():
    l = SWA_BLOCK
    dist = jnp.maximum(jnp.arange(l)[:, None] + l - jnp.arange(2 * l)[None, :], 0)
    max_exact = REL_BUCKETS // 2
    nf = jnp.maximum(dist, 1).astype(F32)
    large = max_exact + (jnp.log(nf / max_exact) / math.log(REL_MAX_DIST / max_exact)
                         * (REL_BUCKETS - max_exact)).astype(jnp.int32)
    large = jnp.minimum(large, REL_BUCKETS - 1)
    return jnp.where(dist < max_exact, dist, large).astype(jnp.int32).T


def kernel(x, mem, w_in, ret_gn_g, swa_sinks, rel_bias, w_o, ln1_g, ln1_b, xa_wq, xa_wkv, xa_wo,
           ln2_g, ln2_b, ffn_w_up, ffn_conv_w, ffn_conv_b, ffn_w_down, ln3_g, ln3_b):
    batch, seq, d = x.shape
    mem_len = mem.shape[1]
    assert w_in.shape[0] == DEPTH and w_in.shape[2] == IN_W
    assert seq % 1024 == 0 and d % LANES == 0

    cosf, sinf = _rope_tables(seq)
    dmat, kdec, qdec, cdec = _retention_tables()
    bucket = _t5_bucket_table()
    bias = _bias_table(rel_bias, bucket)
    row = lambda t: t.reshape(1, -1)

    x2d = x.reshape(batch * seq, d)
    mem_bf = mem.reshape(batch * mem_len, d).astype(BF16)
    for l in range(DEPTH):
        proj = _in_proj(x2d, w_in[l].astype(BF16), cosf, sinf, seq)
        o_r, (wo_bf, wkv_bf, wq_bf, xwo_bf) = _retention(
            proj, cdec, dmat, kdec, qdec, row(ret_gn_g[l]), batch, seq,
            [w_o[l], xa_wkv[l], xa_wq[l], xa_wo[l]])
        o_s, (wup_bf, wdown_bf) = _swa(proj, swa_sinks[l], bias, batch, seq,
                                       [ffn_w_up[l], ffn_w_down[l]])
        x2d = _proj_res_ln(o_r, o_s, 0, 0, wo_bf, x2d, row(ln1_g[l]), row(ln1_b[l]))

        kv = _matmul(mem_bf, wkv_bf, tm=batch * mem_len, tn=1024)
        xa = _xattn(x2d, wq_bf, kv, seq, mem_len)
        x2d = _proj_res_ln(xa, xa, 0, 1, xwo_bf, x2d, row(ln2_g[l]), row(ln2_b[l]))

        x2d = _ffn(x2d, wup_bf, ffn_conv_w[l], row(ffn_conv_b[l]),
                   wdown_bf, row(ln3_g[l]), row(ln3_b[l]), seq)
    return x2d.reshape(batch, seq, d)
```

```python
import functools
import math

import jax
import jax.numpy as jnp
import numpy as np
from jax import lax
from jax.experimental import pallas as pl
from jax.experimental.pallas import tpu as pltpu

RET_HEADS = 8
RET_D = 128
RET_CHUNK = 128
ROPE_BASE = 10000.0
SWA_HEADS = 16
SWA_KV_HEADS = 4
SWA_HD = 64
SWA_WINDOW = 128
SWA_BLOCK = 128
REL_BUCKETS = 32
REL_MAX_DIST = 128
XA_HEADS = 4
CONV_W = 3
LN_EPS = 1e-5
DEPTH = 1
DN_ALPHA = (2 * DEPTH) ** 0.25

RET_W = RET_HEADS * RET_D
SWA_W = SWA_HEADS * SWA_HD
SWA_KV_W = SWA_KV_HEADS * SWA_HD
OFF_RQ, OFF_RK, OFF_RV, OFF_RG = 0, RET_W, 2 * RET_W, 3 * RET_W
OFF_SQ = 4 * RET_W
OFF_SK = OFF_SQ + SWA_W
OFF_SV = OFF_SK + SWA_KV_W
IN_W = OFF_SV + SWA_KV_W

LANES = 128
MXU_COLS = 256
V7X_VMEM_BYTES = 64 * 1024 * 1024
VMEM_LIMIT = 56 * 1024 * 1024

BF16 = jnp.bfloat16
F32 = jnp.float32


def _cparams(sem, vmem=VMEM_LIMIT):
    return pltpu.CompilerParams(dimension_semantics=sem, vmem_limit_bytes=vmem)


def _layer_norm(y, g, b):
    mu = jnp.mean(y, axis=-1, keepdims=True)
    d = y - mu
    var = jnp.mean(d * d, axis=-1, keepdims=True)
    return d * lax.rsqrt(var + LN_EPS) * g + b


def _dot(a, b):
    return jnp.dot(a, b, preferred_element_type=F32)


def _dot_nt(a, b):
    return lax.dot_general(a, b, (((1,), (1,)), ((), ())), preferred_element_type=F32)


class _CastPlan:
    def __init__(self, weights, steps, step_of):
        self.weights = list(weights)
        self.ranges, self.specs = [], []
        per_weight = steps // len(self.weights)
        start = 0
        for w in self.weights:
            rows = w.shape[0]
            nblocks = next(nb for nb in range(per_weight, 0, -1)
                           if rows % nb == 0 and (rows // nb) % 16 == 0)
            self.ranges.append((start, nblocks))
            self.specs.append(self._spec(w, rows // nblocks, start, nblocks, step_of))
            start += nblocks
        assert start <= steps

    @staticmethod
    def _spec(w, block_rows, start, nblocks, step_of):
        idx = lambda *g: (jnp.clip(step_of(*g) - start, 0, nblocks - 1), 0)
        return pl.BlockSpec((block_rows, w.shape[1]), idx)

    def out_shapes(self):
        return [jax.ShapeDtypeStruct(w.shape, BF16) for w in self.weights]

    def emit(self, step, w_refs, wb_refs):
        for (start, nblocks), w_ref, wb_ref in zip(self.ranges, w_refs, wb_refs):
            @pl.when((step >= start) & (step < start + nblocks))
            def _(w_ref=w_ref, wb_ref=wb_ref):
                wb_ref[...] = w_ref[...].astype(BF16)


def _dot_tn(a, b):
    return lax.dot_general(a, b, (((0,), (0,)), ((), ())), preferred_element_type=F32)


IN_PROJ_GROUP_ROWS = 256
IN_PROJ_VMEM_LIMIT = 60 * 1024 * 1024


def _in_proj_kernel(x_ref, w_ref, cos_ref, sin_ref, o_ref, xb_ref, wb_ref, *, tn):
    j = pl.program_id(1)

    @pl.when(j == 0)
    def _():
        xb_ref[...] = x_ref[...].astype(BF16)

    wb_ref[...] = w_ref[...].astype(BF16)

    n_q = OFF_RK // tn
    n_rope = OFF_RV // tn
    scale = jnp.where((j >= n_q) & (j < n_rope), RET_D ** -0.5, 1.0).astype(F32)
    tm = x_ref.shape[0]
    half = IN_PROJ_GROUP_ROWS
    for mh in range(tm // IN_PROJ_GROUP_ROWS):
        rows = slice(mh * half, (mh + 1) * half)
        cos = cos_ref[rows, :]
        sin = sin_ref[rows, :]
        for s in range(tn // MXU_COLS):
            acc = _dot(xb_ref[rows, :], wb_ref[:, s * MXU_COLS:(s + 1) * MXU_COLS])
            for c in range(MXU_COLS // RET_D):
                a = acc[:, c * RET_D:(c + 1) * RET_D]
                r = pltpu.roll(a, RET_D // 2, 1)
                lo = s * MXU_COLS + c * RET_D
                o_ref[rows, lo:lo + RET_D] = ((a * cos + r * sin) * scale).astype(o_ref.dtype)


def _in_proj(x2d, w, cosf, sinf, seq, *, tm=2048, tn=512):
    m, d = x2d.shape
    n = w.shape[1]
    pos_blocks = seq // tm
    n_rope = OFF_RV // tn
    cosf = jnp.concatenate([cosf, jnp.ones((tm, RET_D), F32)], axis=0)
    sinf = jnp.concatenate([sinf, jnp.zeros((tm, RET_D), F32)], axis=0)
    tbl_idx = lambda i, j: (jnp.where(j < n_rope, i % pos_blocks, pos_blocks), 0)
    return pl.pallas_call(
        functools.partial(_in_proj_kernel, tn=tn),
        out_shape=jax.ShapeDtypeStruct((m, n), BF16),
        grid=(m // tm, n // tn),
        in_specs=[
            pl.BlockSpec((tm, d), lambda i, j: (i, 0)),
            pl.BlockSpec((d, tn), lambda i, j: (0, j)),
            pl.BlockSpec((tm, RET_D), tbl_idx),
            pl.BlockSpec((tm, RET_D), tbl_idx),
        ],
        out_specs=pl.BlockSpec((tm, tn), lambda i, j: (i, j)),
        scratch_shapes=[pltpu.VMEM((tm, d), BF16), pltpu.VMEM((d, tn), BF16)],
        compiler_params=_cparams(("parallel", "arbitrary"), IN_PROJ_VMEM_LIMIT),
        name="in_proj_rope",
    )(x2d, w, cosf, sinf)


def _retention_kernel(cdec_ref, q_ref, k_ref, v_ref, g_ref, dmat_ref, kdec_ref, qdec_ref,
                      gn_ref, *rest, chunks, casts):
    nw = len(casts.weights)
    w_refs, o_ref, wb_refs, state_ref = rest[:nw], rest[nw], rest[nw + 1:2 * nw + 1], rest[2 * nw + 1]
    n = pl.program_id(1)
    casts.emit(pl.program_id(0) * pl.num_programs(1) + n, w_refs, wb_refs)
    c_sz = RET_CHUNK

    @pl.when(n == 0)
    def _():
        state_ref[...] = jnp.zeros_like(state_ref)

    units = [(h, c) for h in range(RET_HEADS) for c in range(chunks)]
    rows_of = lambda c: slice(c * c_sz, (c + 1) * c_sz)
    cols_of = lambda h: slice(h * RET_D, (h + 1) * RET_D)
    scores = {(h, c): (_dot_nt(q_ref[rows_of(c), cols_of(h)], k_ref[rows_of(c), cols_of(h)])
                       * dmat_ref[h]).astype(BF16) for h, c in units}
    kvs = {(h, c): _dot_tn((k_ref[rows_of(c), cols_of(h)].astype(F32) * kdec_ref[h]).astype(BF16),
                           v_ref[rows_of(c), cols_of(h)]) for h, c in units}
    prevs = {}
    for h in range(RET_HEADS):
        state = state_ref[h]
        cdec = cdec_ref[h]
        for c in range(chunks):
            prevs[(h, c)] = state.astype(BF16)
            state = state * cdec + kvs[(h, c)]
        state_ref[h] = state
    for h, c in units:
        rows, cols = rows_of(c), cols_of(h)
        o = _dot(scores[(h, c)], v_ref[rows, cols]) + _dot(q_ref[rows, cols], prevs[(h, c)]) * qdec_ref[h]
        mu = jnp.mean(o, axis=-1, keepdims=True)
        dlt = o - mu
        var = jnp.mean(dlt * dlt, axis=-1, keepdims=True)
        on = dlt * lax.rsqrt(var + LN_EPS) * gn_ref[:, cols]
        g = g_ref[rows, cols].astype(F32)
        o_ref[rows, cols] = (g * jax.nn.sigmoid(g) * on).astype(o_ref.dtype)


def _retention(proj, cdec, dmat, kdec, qdec, gn, batch, seq, cast_ws, *, rows=512):
    m = proj.shape[0]
    nblk = seq // rows
    spec = lambda off: pl.BlockSpec((rows, RET_W), lambda b, n: (b * nblk + n, off // RET_W))
    table = lambda t: pl.BlockSpec(t.shape, lambda b, n: (0,) * t.ndim)
    casts = _CastPlan(cast_ws, batch * nblk, lambda b, n: b * nblk + n)
    outs = pl.pallas_call(
        functools.partial(_retention_kernel, chunks=rows // RET_CHUNK, casts=casts),
        out_shape=[jax.ShapeDtypeStruct((m, RET_W), BF16)] + casts.out_shapes(),
        grid=(batch, nblk),
        in_specs=[
            pl.BlockSpec(memory_space=pltpu.SMEM),
            spec(OFF_RQ), spec(OFF_RK), spec(OFF_RV), spec(OFF_RG),
            table(dmat), table(kdec), table(qdec), table(gn),
        ] + casts.specs,
        out_specs=[pl.BlockSpec((rows, RET_W), lambda b, n: (b * nblk + n, 0))] + casts.specs,
        scratch_shapes=[pltpu.VMEM((RET_HEADS, RET_D, RET_D), F32)],
        compiler_params=_cparams(("arbitrary", "arbitrary")),
        name="retention",
    )(cdec, proj, proj, proj, proj, dmat, kdec, qdec, gn, *casts.weights)
    return outs[0], outs[1:]


def _bias_kernel(rel_ref, bucket_ref, o_ref):
    h = pl.program_id(0)
    bucket = bucket_ref[...]
    l = bucket.shape[1]

    def body(b, acc):
        return jnp.where(bucket == b, rel_ref[b, h], acc)

    bias = lax.fori_loop(0, REL_BUCKETS, body, jnp.zeros(bucket.shape, F32))
    j = lax.broadcasted_iota(jnp.int32, bucket.shape, 0)
    i = lax.broadcasted_iota(jnp.int32, bucket.shape, 1)
    dist = i + l - j
    o_ref[0] = jnp.where((dist >= 0) & (dist < SWA_WINDOW), bias, -jnp.inf)


def _bias_table(rel_bias, bucket):
    l2, l = bucket.shape
    return pl.pallas_call(
        _bias_kernel,
        out_shape=jax.ShapeDtypeStruct((SWA_HEADS, l2, l), F32),
        grid=(SWA_HEADS,),
        in_specs=[pl.BlockSpec(memory_space=pltpu.SMEM),
                  pl.BlockSpec((l2, l), lambda h: (0, 0))],
        out_specs=pl.BlockSpec((1, l2, l), lambda h: (h, 0, 0)),
        compiler_params=pltpu.CompilerParams(dimension_semantics=("arbitrary",)),
        name="t5_bias_table",
    )(rel_bias, bucket)


SWA_QK_AHEAD = 2


def _swap_lane_halves(x):
    u = pltpu.bitcast(x, jnp.uint32)
    return pltpu.bitcast(pltpu.roll(u, LANES // 2, 1), x.dtype)


def _swa_kernel(sink_ref, q_ref, kc_ref, vc_ref, kp_ref, vp_ref, bias_ref, first_ref, *rest,
                blocks, casts):
    nw = len(casts.weights)
    w_refs, o_ref, wb_refs = rest[:nw], rest[nw], rest[nw + 1:]
    casts.emit(pl.program_id(0) * pl.num_programs(1) + pl.program_id(1), w_refs, wb_refs)
    l = SWA_BLOCK
    grp = SWA_HEADS // SWA_KV_HEADS
    per_tile = LANES // SWA_HD
    low = lax.broadcasted_iota(jnp.int32, (2 * l, LANES), 1) < SWA_HD
    zero_k = jnp.zeros((2 * l, LANES), BF16)
    zero_v = jnp.zeros((SWA_HD, 2 * l), BF16)
    tiles = {}

    def kv_tile(blk, t):
        if (blk, t) not in tiles:
            r0 = blk * l
            cols = slice(t * LANES, (t + 1) * LANES)
            if blk == 0:
                kt = jnp.concatenate([kp_ref[:, cols], kc_ref[0:l, cols]], axis=0)
                vt = jnp.concatenate([vp_ref[:, cols], vc_ref[0:l, cols]], axis=0)
            else:
                kt = kc_ref[r0 - l:r0 + l, cols]
                vt = vc_ref[r0 - l:r0 + l, cols]
            kt = kt * (SWA_HD ** -0.5)
            tiles[(blk, t)] = (kt, _swap_lane_halves(kt), vt.astype(F32).T.astype(BF16))
        return tiles[(blk, t)]

    tasks = [(blk, kh) for blk in range(blocks) for kh in range(SWA_KV_HEADS)]

    def qk(task):
        blk, kh = task
        t, sub = divmod(kh, per_tile)
        kt, kr, _ = kv_tile(blk, t)
        k_lo, k_hi = (kt, kr) if sub == 0 else (kr, kt)
        k2 = jnp.concatenate([jnp.where(low, k_lo, zero_k), jnp.where(low, zero_k, k_hi)], axis=0)
        rows = slice(blk * l, (blk + 1) * l)
        qcat = jnp.concatenate([q_ref[rows, (kh * grp // 2 + pp) * LANES:(kh * grp // 2 + pp + 1) * LANES]
                                for pp in range(grp // 2)], axis=0)
        return _dot_nt(k2, qcat)

    def finish(task, st):
        blk, kh = task
        t, sub = divmod(kh, per_tile)
        vh = kv_tile(blk, t)[2][sub * SWA_HD:(sub + 1) * SWA_HD, :]
        v2t = jnp.concatenate([jnp.concatenate([vh, zero_v], axis=1),
                               jnp.concatenate([zero_v, vh], axis=1)], axis=0)
        p_rows, den_rows = [], []
        for s in range(2):
            p_cols, den_cols = [], []
            for pp in range(grp // 2):
                hd = kh * grp + 2 * pp + s
                lg = st[s * 2 * l:(s + 1) * 2 * l, pp * l:(pp + 1) * l] + bias_ref[hd]
                if blk == 0:
                    lg = lg + first_ref[0]
                sink = sink_ref[hd]
                mx = jnp.maximum(jnp.max(lg, axis=0, keepdims=True), sink)
                p = jnp.exp(lg - mx)
                den = jnp.sum(p, axis=0, keepdims=True) + jnp.exp(sink - mx)
                p_cols.append(p.astype(BF16))
                den_cols.append(jnp.broadcast_to(den, (SWA_HD, l)))
            p_rows.append(jnp.concatenate(p_cols, axis=1))
            den_rows.append(jnp.concatenate(den_cols, axis=1))
        out_t = _dot(v2t, jnp.concatenate(p_rows, axis=0))
        out_t = out_t / jnp.concatenate(den_rows, axis=0)
        out = jnp.concatenate([out_t[:, pp * l:(pp + 1) * l].T for pp in range(grp // 2)], axis=1)
        o_ref[blk * l:(blk + 1) * l, kh * grp * SWA_HD:(kh + 1) * grp * SWA_HD] = out.astype(o_ref.dtype)

    pending = [qk(t) for t in tasks[:SWA_QK_AHEAD]]
    for n, task in enumerate(tasks):
        if n + SWA_QK_AHEAD < len(tasks):
            pending.append(qk(tasks[n + SWA_QK_AHEAD]))
        finish(task, pending.pop(0))


def _swa(proj, sinks, bias, batch, seq, cast_ws, *, rows=256):
    m = proj.shape[0]
    nblk = seq // rows
    per = rows // SWA_BLOCK
    kcol, vcol = OFF_SK // SWA_KV_W, OFF_SV // SWA_KV_W
    l = SWA_BLOCK
    first = jnp.stack([jnp.zeros((2 * l, l), F32),
                       jnp.where(jnp.arange(2 * l)[:, None] < l, -jnp.inf, 0.0)
                       * jnp.ones((1, l), F32)])

    def prev_idx(b, n):
        return jnp.maximum((b * nblk + n) * per - 1, 0)

    casts = _CastPlan(cast_ws, batch * nblk, lambda b, n: b * nblk + n)
    outs = pl.pallas_call(
        functools.partial(_swa_kernel, blocks=per, casts=casts),
        out_shape=[jax.ShapeDtypeStruct((m, SWA_W), BF16)] + casts.out_shapes(),
        grid=(batch, nblk),
        in_specs=[
            pl.BlockSpec(memory_space=pltpu.SMEM),
            pl.BlockSpec((rows, SWA_W), lambda b, n: (b * nblk + n, OFF_SQ // SWA_W)),
            pl.BlockSpec((rows, SWA_KV_W), lambda b, n: (b * nblk + n, kcol)),
            pl.BlockSpec((rows, SWA_KV_W), lambda b, n: (b * nblk + n, vcol)),
            pl.BlockSpec((SWA_BLOCK, SWA_KV_W), lambda b, n: (prev_idx(b, n), kcol)),
            pl.BlockSpec((SWA_BLOCK, SWA_KV_W), lambda b, n: (prev_idx(b, n), vcol)),
            pl.BlockSpec((SWA_HEADS, 2 * SWA_BLOCK, SWA_BLOCK), lambda b, n: (0, 0, 0)),
            pl.BlockSpec((1, 2 * SWA_BLOCK, SWA_BLOCK), lambda b, n: (jnp.where(n == 0, 1, 0), 0, 0)),
        ] + casts.specs,
        out_specs=[pl.BlockSpec((rows, SWA_W), lambda b, n: (b * nblk + n, 0))] + casts.specs,
        compiler_params=_cparams(("arbitrary", "arbitrary")),
        name="swa_sink_attention",
    )(sinks, proj, proj, proj, proj, proj, bias, first, *casts.weights)
    return outs[0], outs[1:]


LN_GROUP_ROWS = 256


def _proj_res_ln_kernel(a1_ref, a2_ref, w1_ref, w2_ref, x_ref, g_ref, b_ref, o_ref):
    for r in range(x_ref.shape[0] // LN_GROUP_ROWS):
        rows = slice(r * LN_GROUP_ROWS, (r + 1) * LN_GROUP_ROWS)
        y = _dot(a1_ref[rows, :], w1_ref[...]) + _dot(a2_ref[rows, :], w2_ref[...])
        o_ref[rows, :] = _layer_norm(DN_ALPHA * x_ref[rows, :] + y, g_ref[...], b_ref[...])


def _proj_res_ln(a1, a2, col1, col2, w_bf, x2d, g, b, *, tm=1024):
    m, d = x2d.shape
    kh = w_bf.shape[0] // 2
    return pl.pallas_call(
        _proj_res_ln_kernel,
        out_shape=jax.ShapeDtypeStruct((m, d), F32),
        grid=(m // tm,),
        in_specs=[
            pl.BlockSpec((tm, kh), lambda i: (i, col1)),
            pl.BlockSpec((tm, kh), lambda i: (i, col2)),
            pl.BlockSpec((kh, d), lambda i: (0, 0), pipeline_mode=pl.Buffered(1)),
            pl.BlockSpec((kh, d), lambda i: (1, 0), pipeline_mode=pl.Buffered(1)),
            pl.BlockSpec((tm, d), lambda i: (i, 0)),
            pl.BlockSpec((1, d), lambda i: (0, 0)),
            pl.BlockSpec((1, d), lambda i: (0, 0)),
        ],
        out_specs=pl.BlockSpec((tm, d), lambda i: (i, 0)),
        compiler_params=_cparams(("parallel",)),
        name="proj_residual_ln",
    )(a1, a2, w_bf, w_bf, x2d, g, b)


def _matmul_kernel(a_ref, w_ref, o_ref):
    o_ref[...] = _dot(a_ref[...], w_ref[...]).astype(o_ref.dtype)


def _matmul(a_bf, w_bf, *, tm, tn):
    m, k = a_bf.shape
    n = w_bf.shape[1]
    return pl.pallas_call(
        _matmul_kernel,
        out_shape=jax.ShapeDtypeStruct((m, n), BF16),
        grid=(m // tm, n // tn),
        in_specs=[pl.BlockSpec((tm, k), lambda i, j: (i, 0)),
                  pl.BlockSpec((k, tn), lambda i, j: (0, j))],
        out_specs=pl.BlockSpec((tm, tn), lambda i, j: (i, j)),
        compiler_params=_cparams(("parallel", "arbitrary")),
        name="matmul",
    )(a_bf, w_bf)


XA_GROUP_ROWS = 256


def _xattn_kernel(x_ref, wq_ref, k_ref, v_ref, o_ref, *, hd):
    groups = x_ref.shape[0] // XA_GROUP_ROWS
    tasks = [(r, h) for r in range(groups) for h in range(XA_HEADS)]
    xb = {}

    def qproj(task):
        r, h = task
        if r not in xb:
            xb[r] = x_ref[r * XA_GROUP_ROWS:(r + 1) * XA_GROUP_ROWS, :].astype(BF16)
        return _dot(xb[r], wq_ref[:, h * hd:(h + 1) * hd]).astype(BF16)

    q_next = qproj(tasks[0])
    for n, (r, h) in enumerate(tasks):
        q = q_next
        if n + 1 < len(tasks):
            q_next = qproj(tasks[n + 1])
        cols = slice(h * hd, (h + 1) * hd)
        logits = _dot_nt(q, k_ref[:, cols]) * (hd ** -0.5)
        mx = jnp.max(logits, axis=-1, keepdims=True)
        p = jnp.exp(logits - mx)
        den = jnp.sum(p, axis=-1, keepdims=True)
        rows = slice(r * XA_GROUP_ROWS, (r + 1) * XA_GROUP_ROWS)
        o_ref[rows, cols] = (_dot(p.astype(BF16), v_ref[:, cols]) / den).astype(o_ref.dtype)


def _xattn(x2d, wq_bf, kv, seq, mem_len, *, tm=1024):
    m, d = x2d.shape
    hd = d // XA_HEADS
    per_b = seq // tm
    return pl.pallas_call(
        functools.partial(_xattn_kernel, hd=hd),
        out_shape=jax.ShapeDtypeStruct((m, d), BF16),
        grid=(m // tm,),
        in_specs=[
            pl.BlockSpec((tm, d), lambda i: (i, 0)),
            pl.BlockSpec((d, d), lambda i: (0, 0), pipeline_mode=pl.Buffered(1)),
            pl.BlockSpec((mem_len, d), lambda i: (i // per_b, 0)),
            pl.BlockSpec((mem_len, d), lambda i: (i // per_b, 1)),
        ],
        out_specs=pl.BlockSpec((tm, d), lambda i: (i, 0)),
        compiler_params=_cparams(("parallel",)),
        name="memory_cross_attention",
    )(x2d, wq_bf, kv, kv)


FFN_HALO = 16
FFN_GROUP_ROWS = 256
FFN_CHUNK = 256


def _ffn_kernel(x_ref, xh_ref, wu_ref, wg_ref, cw_ref, cb_ref, wd_ref, g_ref, b_ref, o_ref,
                xb_ref, *, tm, per_b):
    i = pl.program_id(0)
    f = pl.program_id(1)
    nf = pl.num_programs(1)

    groups = tm // FFN_GROUP_ROWS

    def body(first, last):
        cw = cw_ref[...]
        cb = cb_ref[...]

        def up(r):
            r0 = r * FFN_GROUP_ROWS
            if first:
                xb_ref[FFN_HALO + r0:FFN_HALO + r0 + FFN_GROUP_ROWS, :] = (
                    x_ref[r0:r0 + FFN_GROUP_ROWS, :].astype(BF16))
            u = _dot(xb_ref[FFN_HALO + r0:FFN_HALO + r0 + FFN_GROUP_ROWS, :], wu_ref[...])
            ge = _dot(xb_ref[r0:r0 + FFN_GROUP_ROWS + FFN_HALO, :], wg_ref[...])
            return u, ge

        if first:
            halo = jnp.where(i % per_b == 0, 0.0, xh_ref[...])
            xb_ref[0:FFN_HALO, :] = halo.astype(BF16)
        nxt = up(0)
        for r in range(groups):
            u, ge = nxt
            if r + 1 < groups:
                nxt = up(r + 1)
            gc = cb
            for tap in range(CONV_W):
                lo = FFN_HALO - (CONV_W - 1) + tap
                gc = gc + ge[lo:lo + FFN_GROUP_ROWS, :] * cw[tap:tap + 1, :]
            hcur = (gc * jax.nn.sigmoid(gc) * u).astype(BF16)
            rows = slice(r * FFN_GROUP_ROWS, (r + 1) * FFN_GROUP_ROWS)
            acc = _dot(hcur, wd_ref[...])
            if not first:
                acc = o_ref[rows, :] + acc
            if last:
                acc = _layer_norm(DN_ALPHA * x_ref[rows, :] + acc, g_ref[...], b_ref[...])
            o_ref[rows, :] = acc

    pl.when(f == 0)(functools.partial(body, True, False))
    pl.when((f > 0) & (f < nf - 1))(functools.partial(body, False, False))
    pl.when(f == nf - 1)(functools.partial(body, False, True))


def _ffn(x2d, wup_bf, conv_w, conv_b, wdown_bf, g, b, seq, *, tm=1024, fc=FFN_CHUNK):
    m, d = x2d.shape
    nf = wdown_bf.shape[0] // fc
    assert nf >= 2, "the first and the last hidden chunk are separate code paths"
    per_b = seq // tm
    halo_per_tile = tm // FFN_HALO
    return pl.pallas_call(
        functools.partial(_ffn_kernel, tm=tm, per_b=per_b),
        out_shape=jax.ShapeDtypeStruct((m, d), F32),
        grid=(m // tm, nf),
        in_specs=[
            pl.BlockSpec((tm, d), lambda i, f: (i, 0)),
            pl.BlockSpec((FFN_HALO, d), lambda i, f: (jnp.maximum(i * halo_per_tile - 1, 0), 0)),
            pl.BlockSpec((d, fc), lambda i, f: (0, f)),
            pl.BlockSpec((d, fc), lambda i, f: (0, nf + f)),
            pl.BlockSpec((CONV_W, fc), lambda i, f: (0, f)),
            pl.BlockSpec((1, fc), lambda i, f: (0, f)),
            pl.BlockSpec((fc, d), lambda i, f: (f, 0)),
            pl.BlockSpec((1, d), lambda i, f: (0, 0)),
            pl.BlockSpec((1, d), lambda i, f: (0, 0)),
        ],
        out_specs=pl.BlockSpec((tm, d), lambda i, f: (i, 0)),
        scratch_shapes=[pltpu.VMEM((tm + FFN_HALO, d), BF16)],
        compiler_params=_cparams(("parallel", "arbitrary")),
        name="conv_ffn_ln",
    )(x2d, x2d, wup_bf, wup_bf, conv_w, conv_b, wdown_bf, g, b)


def _rope_tables(seq):
    half = RET_D // 2
    inv = 1.0 / (ROPE_BASE ** (jnp.arange(half, dtype=F32) / half))
    ang = jnp.arange(seq).astype(F32)[:, None] * inv[None, :]
    cos, sin = jnp.cos(ang), jnp.sin(ang)
    return jnp.concatenate([cos, cos], axis=-1), jnp.concatenate([-sin, sin], axis=-1)


def _retention_tables():
    c = RET_CHUNK
    log_gamma = jnp.log1p(-jnp.exp2(-5.0 - jnp.arange(RET_HEADS, dtype=F32)))
    idx = jnp.arange(c, dtype=F32)
    diff = idx[:, None] - idx[None, :]
    dmat = jnp.where(diff[None] >= 0,
                     jnp.exp(log_gamma[:, None, None] * jnp.maximum(diff, 0.0)[None]), 0.0)
    kdec = jnp.exp(log_gamma[:, None] * (c - 1 - idx)[None, :])
    qdec = jnp.exp(log_gamma[:, None] * (idx + 1.0)[None, :])
    cdec = jnp.exp(log_gamma * c)
    bc = lambda t: jnp.broadcast_to(t[:, :, None], (RET_HEADS, c, RET_D))
    return dmat, bc(kdec), bc(qdec), cdec


def _t5_bucket_table():
    l = SWA_BLOCK
    dist = jnp.maximum(jnp.arange(l)[:, None] + l - jnp.arange(2 * l)[None, :], 0)
    max_exact = REL_BUCKETS // 2
    nf = jnp.maximum(dist, 1).astype(F32)
    large = max_exact + (jnp.log(nf / max_exact) / math.log(REL_MAX_DIST / max_exact)
                         * (REL_BUCKETS - max_exact)).astype(jnp.int32)
    large = jnp.minimum(large, REL_BUCKETS - 1)
    return jnp.where(dist < max_exact, dist, large).astype(jnp.int32).T


def kernel(x, mem, w_in, ret_gn_g, swa_sinks, rel_bias, w_o, ln1_g, ln1_b, xa_wq, xa_wkv, xa_wo,
           ln2_g, ln2_b, ffn_w_up, ffn_conv_w, ffn_conv_b, ffn_w_down, ln3_g, ln3_b):
    batch, seq, d = x.shape
    mem_len = mem.shape[1]
    assert w_in.shape[0] == DEPTH and w_in.shape[2] == IN_W
    assert seq % 1024 == 0 and d % LANES == 0

    cosf, sinf = _rope_tables(seq)
    dmat, kdec, qdec, cdec = _retention_tables()
    bucket = _t5_bucket_table()
    bias = _bias_table(rel_bias, bucket)
    row = lambda t: t.reshape(1, -1)

    x2d = x.reshape(batch * seq, d)
    mem_bf = mem.reshape(batch * mem_len, d).astype(BF16)
    for l in range(DEPTH):
        proj = _in_proj(x2d, w_in[l], cosf, sinf, seq)
        o_r, (wo_bf, wkv_bf, wq_bf, xwo_bf) = _retention(
            proj, cdec, dmat, kdec, qdec, row(ret_gn_g[l]), batch, seq,
            [w_o[l], xa_wkv[l], xa_wq[l], xa_wo[l]])
        o_s, (wup_bf, wdown_bf) = _swa(proj, swa_sinks[l], bias, batch, seq,
                                       [ffn_w_up[l], ffn_w_down[l]])
        x2d = _proj_res_ln(o_r, o_s, 0, 0, wo_bf, x2d, row(ln1_g[l]), row(ln1_b[l]))

        kv = _matmul(mem_bf, wkv_bf, tm=batch * mem_len, tn=1024)
        xa = _xattn(x2d, wq_bf, kv, seq, mem_len)
        x2d = _proj_res_ln(xa, xa, 0, 1, xwo_bf, x2d, row(ln2_g[l]), row(ln2_b[l]))

        x2d = _ffn(x2d, wup_bf, ffn_conv_w[l], row(ffn_conv_b[l]),
                   wdown_bf, row(ln3_g[l]), row(ln3_b[l]), seq)
    return x2d.reshape(batch, seq, d)
```

```python
import functools
import math

import jax
import jax.numpy as jnp
from jax import lax
from jax.experimental import pallas as pl
from jax.experimental.pallas import tpu as pltpu

RET_HEADS = 8
RET_D = 128
RET_CHUNK = 128
ROPE_BASE = 10000.0
SWA_HEADS = 16
SWA_KV_HEADS = 4
SWA_HD = 64
SWA_WINDOW = 128
SWA_BLOCK = 128
REL_BUCKETS = 32
REL_MAX_DIST = 128
XA_HEADS = 4
CONV_W = 3
LN_EPS = 1e-5
DEPTH = 1
DN_ALPHA = (2 * DEPTH) ** 0.25

RET_W = RET_HEADS * RET_D
SWA_W = SWA_HEADS * SWA_HD
SWA_KV_W = SWA_KV_HEADS * SWA_HD
OFF_RQ, OFF_RK, OFF_RV, OFF_RG = 0, RET_W, 2 * RET_W, 3 * RET_W
OFF_SQ = 4 * RET_W
OFF_SK = OFF_SQ + SWA_W
OFF_SV = OFF_SK + SWA_KV_W
IN_W = OFF_SV + SWA_KV_W

LANES = 128
MXU_COLS = 256
MIB = 1024 * 1024
V7X_VMEM_BYTES = 64 * MIB
VMEM_LIMIT = V7X_VMEM_BYTES - 8 * MIB

BF16 = jnp.bfloat16
F32 = jnp.float32


def _cparams(sem, vmem=VMEM_LIMIT):
    return pltpu.CompilerParams(dimension_semantics=sem, vmem_limit_bytes=vmem)


def _layer_norm(y, g, b):
    mu = jnp.mean(y, axis=-1, keepdims=True)
    d = y - mu
    var = jnp.mean(d * d, axis=-1, keepdims=True)
    return d * lax.rsqrt(var + LN_EPS) * g + b


def _dot(a, b):
    return jnp.dot(a, b, preferred_element_type=F32)


def _dot_nt(a, b):
    return lax.dot_general(a, b, (((1,), (1,)), ((), ())), preferred_element_type=F32)


class _CastPlan:
    def __init__(self, weights, steps, step_of):
        self.weights = list(weights)
        self.ranges, self.specs = [], []
        per_weight = steps // len(self.weights)
        start = 0
        for w in self.weights:
            rows = w.shape[0]
            nblocks = next(nb for nb in range(per_weight, 0, -1)
                           if rows % nb == 0 and (rows // nb) % 16 == 0)
            self.ranges.append((start, nblocks))
            self.specs.append(self._spec(w, rows // nblocks, start, nblocks, step_of))
            start += nblocks
        assert start <= steps

    @staticmethod
    def _spec(w, block_rows, start, nblocks, step_of):
        idx = lambda *g: (jnp.clip(step_of(*g) - start, 0, nblocks - 1), 0)
        return pl.BlockSpec((block_rows, w.shape[1]), idx)

    def out_shapes(self):
        return [jax.ShapeDtypeStruct(w.shape, BF16) for w in self.weights]

    def emit(self, step, w_refs, wb_refs):
        for (start, nblocks), w_ref, wb_ref in zip(self.ranges, w_refs, wb_refs):
            @pl.when((step >= start) & (step < start + nblocks))
            def _(w_ref=w_ref, wb_ref=wb_ref):
                wb_ref[...] = w_ref[...].astype(BF16)


def _dot_tn(a, b):
    return lax.dot_general(a, b, (((0,), (0,)), ((), ())), preferred_element_type=F32)


IN_PROJ_GROUP_ROWS = 256
IN_PROJ_VMEM_LIMIT = V7X_VMEM_BYTES - 4 * MIB


def _in_proj_kernel(x_ref, w_ref, cos_ref, sin_ref, o_ref, xb_ref, wb_ref, *, tn):
    j = pl.program_id(1)
    n_q = OFF_RK // tn
    n_rope = OFF_RV // tn
    scale = jnp.where(j >= n_q, RET_D ** -0.5, 1.0).astype(F32)
    tm = x_ref.shape[0]
    half = IN_PROJ_GROUP_ROWS

    def body(first, rope):
        wb_ref[...] = w_ref[...].astype(BF16)
        for mh in range(tm // IN_PROJ_GROUP_ROWS):
            rows = slice(mh * half, (mh + 1) * half)
            if first:
                xb_ref[rows, :] = x_ref[rows, :].astype(BF16)
            for s in range(tn // MXU_COLS):
                cols = slice(s * MXU_COLS, (s + 1) * MXU_COLS)
                acc = _dot(xb_ref[rows, :], wb_ref[:, cols])
                if not rope:
                    o_ref[rows, cols] = acc.astype(o_ref.dtype)
                    continue
                cos = cos_ref[rows, :]
                sin = sin_ref[rows, :]
                for c in range(MXU_COLS // RET_D):
                    a = acc[:, c * RET_D:(c + 1) * RET_D]
                    r = pltpu.roll(a, RET_D // 2, 1)
                    lo = s * MXU_COLS + c * RET_D
                    o_ref[rows, lo:lo + RET_D] = ((a * cos + r * sin) * scale).astype(o_ref.dtype)

    pl.when(j == 0)(functools.partial(body, True, True))
    pl.when((j > 0) & (j < n_rope))(functools.partial(body, False, True))
    pl.when(j >= n_rope)(functools.partial(body, False, False))


def _in_proj(x2d, w, cosf, sinf, seq, *, tm=2048, tn=512):
    m, d = x2d.shape
    n = w.shape[1]
    assert OFF_RK % tn == 0 and OFF_RV % tn == 0 and OFF_RK >= tn
    pos_blocks = seq // tm
    tbl_idx = lambda i, j: (i % pos_blocks, 0)
    return pl.pallas_call(
        functools.partial(_in_proj_kernel, tn=tn),
        out_shape=jax.ShapeDtypeStruct((m, n), BF16),
        grid=(m // tm, n // tn),
        in_specs=[
            pl.BlockSpec((tm, d), lambda i, j: (i, 0)),
            pl.BlockSpec((d, tn), lambda i, j: (0, j)),
            pl.BlockSpec((tm, RET_D), tbl_idx),
            pl.BlockSpec((tm, RET_D), tbl_idx),
        ],
        out_specs=pl.BlockSpec((tm, tn), lambda i, j: (i, j)),
        scratch_shapes=[pltpu.VMEM((tm, d), BF16), pltpu.VMEM((d, tn), BF16)],
        compiler_params=_cparams(("parallel", "arbitrary"), IN_PROJ_VMEM_LIMIT),
        name="in_proj_rope",
    )(x2d, w, cosf, sinf)


def _retention_kernel(cdec_ref, q_ref, k_ref, v_ref, g_ref, dmat_ref, kdec_ref, qdec_ref,
                      gn_ref, *rest, chunks, casts):
    nw = len(casts.weights)
    w_refs, o_ref, wb_refs, state_ref = rest[:nw], rest[nw], rest[nw + 1:2 * nw + 1], rest[2 * nw + 1]
    n = pl.program_id(1)
    casts.emit(pl.program_id(0) * pl.num_programs(1) + n, w_refs, wb_refs)
    c_sz = RET_CHUNK

    @pl.when(n == 0)
    def _():
        state_ref[...] = jnp.zeros_like(state_ref)

    units = [(h, c) for h in range(RET_HEADS) for c in range(chunks)]
    rows_of = lambda c: slice(c * c_sz, (c + 1) * c_sz)
    cols_of = lambda h: slice(h * RET_D, (h + 1) * RET_D)
    scores = {(h, c): (_dot_nt(q_ref[rows_of(c), cols_of(h)], k_ref[rows_of(c), cols_of(h)])
                       * dmat_ref[h]).astype(BF16) for h, c in units}
    kvs = {(h, c): _dot_tn((k_ref[rows_of(c), cols_of(h)].astype(F32) * kdec_ref[h]).astype(BF16),
                           v_ref[rows_of(c), cols_of(h)]) for h, c in units}
    prevs = {}
    for h in range(RET_HEADS):
        state = state_ref[h]
        cdec = cdec_ref[h]
        for c in range(chunks):
            prevs[(h, c)] = state.astype(BF16)
            state = state * cdec + kvs[(h, c)]
        state_ref[h] = state
    for h, c in units:
        rows, cols = rows_of(c), cols_of(h)
        o = _dot(scores[(h, c)], v_ref[rows, cols]) + _dot(q_ref[rows, cols], prevs[(h, c)]) * qdec_ref[h]
        mu = jnp.mean(o, axis=-1, keepdims=True)
        dlt = o - mu
        var = jnp.mean(dlt * dlt, axis=-1, keepdims=True)
        on = dlt * lax.rsqrt(var + LN_EPS) * gn_ref[:, cols]
        g = g_ref[rows, cols].astype(F32)
        o_ref[rows, cols] = (g * jax.nn.sigmoid(g) * on).astype(o_ref.dtype)


def _retention(proj, cdec, dmat, kdec, qdec, gn, batch, seq, cast_ws, *, rows=512):
    m = proj.shape[0]
    nblk = seq // rows
    spec = lambda off: pl.BlockSpec((rows, RET_W), lambda b, n: (b * nblk + n, off // RET_W))
    table = lambda t: pl.BlockSpec(t.shape, lambda b, n: (0,) * t.ndim)
    casts = _CastPlan(cast_ws, batch * nblk, lambda b, n: b * nblk + n)
    outs = pl.pallas_call(
        functools.partial(_retention_kernel, chunks=rows // RET_CHUNK, casts=casts),
        out_shape=[jax.ShapeDtypeStruct((m, RET_W), BF16)] + casts.out_shapes(),
        grid=(batch, nblk),
        in_specs=[
            pl.BlockSpec(memory_space=pltpu.SMEM),
            spec(OFF_RQ), spec(OFF_RK), spec(OFF_RV), spec(OFF_RG),
            table(dmat), table(kdec), table(qdec), table(gn),
        ] + casts.specs,
        out_specs=[pl.BlockSpec((rows, RET_W), lambda b, n: (b * nblk + n, 0))] + casts.specs,
        scratch_shapes=[pltpu.VMEM((RET_HEADS, RET_D, RET_D), F32)],
        compiler_params=_cparams(("arbitrary", "arbitrary")),
        name="retention",
    )(cdec, proj, proj, proj, proj, dmat, kdec, qdec, gn, *casts.weights)
    return outs[0], outs[1:]


def _bias_kernel(rel_ref, bucket_ref, o_ref):
    bucket = bucket_ref[...]
    l = bucket.shape[1]
    j = lax.broadcasted_iota(jnp.int32, bucket.shape, 0)
    i = lax.broadcasted_iota(jnp.int32, bucket.shape, 1)
    dist = i + l - j
    masked = jnp.where((dist >= 0) & (dist < SWA_WINDOW), bucket, -1)
    outside = jnp.where(masked < 0, -jnp.inf, 0.0).astype(F32)

    def head(h, carry):
        def body(b, acc):
            return jnp.where(masked == b, rel_ref[b, h], acc)

        table = lax.fori_loop(0, REL_BUCKETS, body, outside)
        o_ref[h] = table
        o_ref[SWA_HEADS + h] = jnp.where(j < l, -jnp.inf, table)
        return carry

    lax.fori_loop(0, SWA_HEADS, head, 0)


def _bias_table(rel_bias, bucket):
    l2, l = bucket.shape
    return pl.pallas_call(
        _bias_kernel,
        out_shape=jax.ShapeDtypeStruct((2 * SWA_HEADS, l2, l), F32),
        in_specs=[pl.BlockSpec(memory_space=pltpu.SMEM),
                  pl.BlockSpec(memory_space=pltpu.VMEM)],
        out_specs=pl.BlockSpec(memory_space=pltpu.VMEM),
        name="t5_bias_table",
    )(rel_bias, bucket)


SWA_QK_AHEAD = 2


def _swap_lane_halves(x):
    u = pltpu.bitcast(x, jnp.uint32)
    return pltpu.bitcast(pltpu.roll(u, LANES // 2, 1), x.dtype)


def _swa_kernel(sink_ref, q_ref, kc_ref, vc_ref, kp_ref, vp_ref, bias_ref, bias0_ref, *rest,
                blocks, casts):
    nw = len(casts.weights)
    w_refs, o_ref, wb_refs = rest[:nw], rest[nw], rest[nw + 1:]
    casts.emit(pl.program_id(0) * pl.num_programs(1) + pl.program_id(1), w_refs, wb_refs)
    l = SWA_BLOCK
    grp = SWA_HEADS // SWA_KV_HEADS
    per_tile = LANES // SWA_HD
    low = lax.broadcasted_iota(jnp.int32, (2 * l, LANES), 1) < SWA_HD
    zero_k = jnp.zeros((2 * l, LANES), BF16)
    zero_v = jnp.zeros((SWA_HD, 2 * l), BF16)
    tiles = {}

    def kv_tile(blk, t):
        if (blk, t) not in tiles:
            r0 = blk * l
            cols = slice(t * LANES, (t + 1) * LANES)
            if blk == 0:
                kt = jnp.concatenate([kp_ref[:, cols], kc_ref[0:l, cols]], axis=0)
                vt = jnp.concatenate([vp_ref[:, cols], vc_ref[0:l, cols]], axis=0)
            else:
                kt = kc_ref[r0 - l:r0 + l, cols]
                vt = vc_ref[r0 - l:r0 + l, cols]
            kt = kt * (SWA_HD ** -0.5)
            tiles[(blk, t)] = (kt, _swap_lane_halves(kt), vt.astype(F32).T.astype(BF16))
        return tiles[(blk, t)]

    tasks = [(blk, kh) for blk in range(blocks) for kh in range(SWA_KV_HEADS)]

    def qk(task):
        blk, kh = task
        t, sub = divmod(kh, per_tile)
        kt, kr, _ = kv_tile(blk, t)
        k_lo, k_hi = (kt, kr) if sub == 0 else (kr, kt)
        k2 = jnp.concatenate([jnp.where(low, k_lo, zero_k), jnp.where(low, zero_k, k_hi)], axis=0)
        rows = slice(blk * l, (blk + 1) * l)
        qcat = jnp.concatenate([q_ref[rows, (kh * grp // 2 + pp) * LANES:(kh * grp // 2 + pp + 1) * LANES]
                                for pp in range(grp // 2)], axis=0)
        return _dot_nt(k2, qcat)

    def finish(task, st):
        blk, kh = task
        t, sub = divmod(kh, per_tile)
        vh = kv_tile(blk, t)[2][sub * SWA_HD:(sub + 1) * SWA_HD, :]
        v2t = jnp.concatenate([jnp.concatenate([vh, zero_v], axis=1),
                               jnp.concatenate([zero_v, vh], axis=1)], axis=0)
        p_rows, den_rows = [], []
        for s in range(2):
            p_cols, den_cols = [], []
            for pp in range(grp // 2):
                hd = kh * grp + 2 * pp + s
                bias = bias0_ref[hd] if blk == 0 else bias_ref[hd]
                lg = st[s * 2 * l:(s + 1) * 2 * l, pp * l:(pp + 1) * l] + bias
                sink = sink_ref[hd]
                mx = jnp.maximum(jnp.max(lg, axis=0, keepdims=True), sink)
                p = jnp.exp(lg - mx)
                den = jnp.sum(p, axis=0, keepdims=True) + jnp.exp(sink - mx)
                p_cols.append(p.astype(BF16))
                den_cols.append(jnp.broadcast_to(den, (SWA_HD, l)))
            p_rows.append(jnp.concatenate(p_cols, axis=1))
            den_rows.append(jnp.concatenate(den_cols, axis=1))
        out_t = _dot(v2t, jnp.concatenate(p_rows, axis=0))
        out_t = out_t / jnp.concatenate(den_rows, axis=0)
        out = jnp.concatenate([out_t[:, pp * l:(pp + 1) * l].T for pp in range(grp // 2)], axis=1)
        o_ref[blk * l:(blk + 1) * l, kh * grp * SWA_HD:(kh + 1) * grp * SWA_HD] = out.astype(o_ref.dtype)

    pending = [qk(t) for t in tasks[:SWA_QK_AHEAD]]
    for n, task in enumerate(tasks):
        if n + SWA_QK_AHEAD < len(tasks):
            pending.append(qk(tasks[n + SWA_QK_AHEAD]))
        finish(task, pending.pop(0))


def _swa(proj, sinks, bias, batch, seq, cast_ws, *, rows=256):
    m = proj.shape[0]
    nblk = seq // rows
    per = rows // SWA_BLOCK
    kcol, vcol = OFF_SK // SWA_KV_W, OFF_SV // SWA_KV_W

    def prev_idx(b, n):
        return jnp.maximum((b * nblk + n) * per - 1, 0)

    casts = _CastPlan(cast_ws, batch * nblk, lambda b, n: b * nblk + n)
    outs = pl.pallas_call(
        functools.partial(_swa_kernel, blocks=per, casts=casts),
        out_shape=[jax.ShapeDtypeStruct((m, SWA_W), BF16)] + casts.out_shapes(),
        grid=(batch, nblk),
        in_specs=[
            pl.BlockSpec(memory_space=pltpu.SMEM),
            pl.BlockSpec((rows, SWA_W), lambda b, n: (b * nblk + n, OFF_SQ // SWA_W)),
            pl.BlockSpec((rows, SWA_KV_W), lambda b, n: (b * nblk + n, kcol)),
            pl.BlockSpec((rows, SWA_KV_W), lambda b, n: (b * nblk + n, vcol)),
            pl.BlockSpec((SWA_BLOCK, SWA_KV_W), lambda b, n: (prev_idx(b, n), kcol)),
            pl.BlockSpec((SWA_BLOCK, SWA_KV_W), lambda b, n: (prev_idx(b, n), vcol)),
            pl.BlockSpec((SWA_HEADS, 2 * SWA_BLOCK, SWA_BLOCK), lambda b, n: (0, 0, 0)),
            pl.BlockSpec((SWA_HEADS, 2 * SWA_BLOCK, SWA_BLOCK), lambda b, n: (jnp.where(n == 0, 1, 0), 0, 0)),
        ] + casts.specs,
        out_specs=[pl.BlockSpec((rows, SWA_W), lambda b, n: (b * nblk + n, 0))] + casts.specs,
        compiler_params=_cparams(("arbitrary", "arbitrary")),
        name="swa_sink_attention",
    )(sinks, proj, proj, proj, proj, proj, bias, bias, *casts.weights)
    return outs[0], outs[1:]


LN_GROUP_ROWS = 256


def _proj_res_ln_kernel(a1_ref, a2_ref, w1_ref, w2_ref, x_ref, g_ref, b_ref, o_ref):
    for r in range(x_ref.shape[0] // LN_GROUP_ROWS):
        rows = slice(r * LN_GROUP_ROWS, (r + 1) * LN_GROUP_ROWS)
        y = _dot(a1_ref[rows, :], w1_ref[...]) + _dot(a2_ref[rows, :], w2_ref[...])
        o_ref[rows, :] = _layer_norm(DN_ALPHA * x_ref[rows, :] + y, g_ref[...], b_ref[...])


def _proj_res_ln(a1, a2, col1, col2, w_bf, x2d, g, b, *, tm=1024):
    m, d = x2d.shape
    kh = w_bf.shape[0] // 2
    return pl.pallas_call(
        _proj_res_ln_kernel,
        out_shape=jax.ShapeDtypeStruct((m, d), F32),
        grid=(m // tm,),
        in_specs=[
            pl.BlockSpec((tm, kh), lambda i: (i, col1)),
            pl.BlockSpec((tm, kh), lambda i: (i, col2)),
            pl.BlockSpec((kh, d), lambda i: (0, 0), pipeline_mode=pl.Buffered(1)),
            pl.BlockSpec((kh, d), lambda i: (1, 0), pipeline_mode=pl.Buffered(1)),
            pl.BlockSpec((tm, d), lambda i: (i, 0)),
            pl.BlockSpec((1, d), lambda i: (0, 0)),
            pl.BlockSpec((1, d), lambda i: (0, 0)),
        ],
        out_specs=pl.BlockSpec((tm, d), lambda i: (i, 0)),
        compiler_params=_cparams(("parallel",)),
        name="proj_residual_ln",
    )(a1, a2, w_bf, w_bf, x2d, g, b)


def _matmul_kernel(a_ref, w_ref, o_ref):
    o_ref[...] = _dot(a_ref[...], w_ref[...]).astype(o_ref.dtype)


def _matmul(a_bf, w_bf, *, tm, tn):
    m, k = a_bf.shape
    n = w_bf.shape[1]
    return pl.pallas_call(
        _matmul_kernel,
        out_shape=jax.ShapeDtypeStruct((m, n), BF16),
        grid=(m // tm, n // tn),
        in_specs=[pl.BlockSpec((tm, k), lambda i, j: (i, 0)),
                  pl.BlockSpec((k, tn), lambda i, j: (0, j))],
        out_specs=pl.BlockSpec((tm, tn), lambda i, j: (i, j)),
        compiler_params=_cparams(("parallel", "arbitrary")),
        name="matmul",
    )(a_bf, w_bf)


XA_GROUP_ROWS = 256


def _xattn_kernel(x_ref, wq_ref, k_ref, v_ref, o_ref, *, hd):
    groups = x_ref.shape[0] // XA_GROUP_ROWS
    tasks = [(r, h) for r in range(groups) for h in range(XA_HEADS)]
    xb = {}

    def qproj(task):
        r, h = task
        if r not in xb:
            xb[r] = x_ref[r * XA_GROUP_ROWS:(r + 1) * XA_GROUP_ROWS, :].astype(BF16)
        return _dot(xb[r], wq_ref[:, h * hd:(h + 1) * hd]).astype(BF16)

    q_next = qproj(tasks[0])
    for n, (r, h) in enumerate(tasks):
        q = q_next
        if n + 1 < len(tasks):
            q_next = qproj(tasks[n + 1])
        cols = slice(h * hd, (h + 1) * hd)
        logits = _dot_nt(q, k_ref[:, cols]) * (hd ** -0.5)
        mx = jnp.max(logits, axis=-1, keepdims=True)
        p = jnp.exp(logits - mx)
        den = jnp.sum(p, axis=-1, keepdims=True)
        rows = slice(r * XA_GROUP_ROWS, (r + 1) * XA_GROUP_ROWS)
        o_ref[rows, cols] = (_dot(p.astype(BF16), v_ref[:, cols]) / den).astype(o_ref.dtype)


def _xattn(x2d, wq_bf, kv, seq, mem_len, *, tm=1024):
    m, d = x2d.shape
    hd = d // XA_HEADS
    per_b = seq // tm
    return pl.pallas_call(
        functools.partial(_xattn_kernel, hd=hd),
        out_shape=jax.ShapeDtypeStruct((m, d), BF16),
        grid=(m // tm,),
        in_specs=[
            pl.BlockSpec((tm, d), lambda i: (i, 0)),
            pl.BlockSpec((d, d), lambda i: (0, 0), pipeline_mode=pl.Buffered(1)),
            pl.BlockSpec((mem_len, d), lambda i: (i // per_b, 0)),
            pl.BlockSpec((mem_len, d), lambda i: (i // per_b, 1)),
        ],
        out_specs=pl.BlockSpec((tm, d), lambda i: (i, 0)),
        compiler_params=_cparams(("parallel",)),
        name="memory_cross_attention",
    )(x2d, wq_bf, kv, kv)


FFN_HALO = 16
FFN_GROUP_ROWS = 256
FFN_CHUNK = 256


def _ffn_kernel(x_ref, xh_ref, wu_ref, wg_ref, cw_ref, cb_ref, wd_ref, g_ref, b_ref, o_ref,
                xb_ref, *, tm, per_b):
    i = pl.program_id(0)
    f = pl.program_id(1)
    nf = pl.num_programs(1)

    groups = tm // FFN_GROUP_ROWS

    def body(first, last):
        cw = cw_ref[...]
        cb = cb_ref[...]

        def up(r):
            r0 = r * FFN_GROUP_ROWS
            if first:
                xb_ref[FFN_HALO + r0:FFN_HALO + r0 + FFN_GROUP_ROWS, :] = (
                    x_ref[r0:r0 + FFN_GROUP_ROWS, :].astype(BF16))
            u = _dot(xb_ref[FFN_HALO + r0:FFN_HALO + r0 + FFN_GROUP_ROWS, :], wu_ref[...])
            ge = _dot(xb_ref[r0:r0 + FFN_GROUP_ROWS + FFN_HALO, :], wg_ref[...])
            return u, ge

        if first:
            halo = jnp.where(i % per_b == 0, 0.0, xh_ref[...])
            xb_ref[0:FFN_HALO, :] = halo.astype(BF16)
        nxt = up(0)
        for r in range(groups):
            u, ge = nxt
            if r + 1 < groups:
                nxt = up(r + 1)
            gc = cb
            for tap in range(CONV_W):
                lo = FFN_HALO - (CONV_W - 1) + tap
                gc = gc + ge[lo:lo + FFN_GROUP_ROWS, :] * cw[tap:tap + 1, :]
            hcur = (gc * jax.nn.sigmoid(gc) * u).astype(BF16)
            rows = slice(r * FFN_GROUP_ROWS, (r + 1) * FFN_GROUP_ROWS)
            acc = _dot(hcur, wd_ref[...])
            acc = (DN_ALPHA * x_ref[rows, :] if first else o_ref[rows, :]) + acc
            if last:
                acc = _layer_norm(acc, g_ref[...], b_ref[...])
            o_ref[rows, :] = acc

    pl.when(f == 0)(functools.partial(body, True, False))
    pl.when((f > 0) & (f < nf - 1))(functools.partial(body, False, False))
    pl.when(f == nf - 1)(functools.partial(body, False, True))


def _ffn(x2d, wup_bf, conv_w, conv_b, wdown_bf, g, b, seq, *, tm=1024, fc=FFN_CHUNK):
    m, d = x2d.shape
    nf = wdown_bf.shape[0] // fc
    assert nf >= 2, "the first and the last hidden chunk are separate code paths"
    per_b = seq // tm
    halo_per_tile = tm // FFN_HALO
    return pl.pallas_call(
        functools.partial(_ffn_kernel, tm=tm, per_b=per_b),
        out_shape=jax.ShapeDtypeStruct((m, d), F32),
        grid=(m // tm, nf),
        in_specs=[
            pl.BlockSpec((tm, d), lambda i, f: (i, 0)),
            pl.BlockSpec((FFN_HALO, d), lambda i, f: (jnp.maximum(i * halo_per_tile - 1, 0), 0)),
            pl.BlockSpec((d, fc), lambda i, f: (0, f)),
            pl.BlockSpec((d, fc), lambda i, f: (0, nf + f)),
            pl.BlockSpec((CONV_W, fc), lambda i, f: (0, f)),
            pl.BlockSpec((1, fc), lambda i, f: (0, f)),
            pl.BlockSpec((fc, d), lambda i, f: (f, 0)),
            pl.BlockSpec((1, d), lambda i, f: (0, 0)),
            pl.BlockSpec((1, d), lambda i, f: (0, 0)),
        ],
        out_specs=pl.BlockSpec((tm, d), lambda i, f: (i, 0)),
        scratch_shapes=[pltpu.VMEM((tm + FFN_HALO, d), BF16)],
        compiler_params=_cparams(("parallel", "arbitrary")),
        name="conv_ffn_ln",
    )(x2d, x2d, wup_bf, wup_bf, conv_w, conv_b, wdown_bf, g, b)


def _rope_tables(seq):
    half = RET_D // 2
    inv = 1.0 / (ROPE_BASE ** (jnp.arange(half, dtype=F32) / half))
    ang = jnp.arange(seq).astype(F32)[:, None] * inv[None, :]
    cos, sin = jnp.cos(ang), jnp.sin(ang)
    return jnp.concatenate([cos, cos], axis=-1), jnp.concatenate([-sin, sin], axis=-1)


def _retention_tables():
    c = RET_CHUNK
    log_gamma = jnp.log1p(-jnp.exp2(-5.0 - jnp.arange(RET_HEADS, dtype=F32)))
    idx = jnp.arange(c, dtype=F32)
    diff = idx[:, None] - idx[None, :]
    dmat = jnp.where(diff[None] >= 0,
                     jnp.exp(log_gamma[:, None, None] * jnp.maximum(diff, 0.0)[None]), 0.0)
    kdec = jnp.exp(log_gamma[:, None] * (c - 1 - idx)[None, :])
    qdec = jnp.exp(log_gamma[:, None] * (idx + 1.0)[None, :])
    cdec = jnp.exp(log_gamma * c)
    bc = lambda t: jnp.broadcast_to(t[:, :, None], (RET_HEADS, c, RET_D))
    return dmat, bc(kdec), bc(qdec), cdec


def _t5_bucket_table():
    l = SWA_BLOCK
    dist = jnp.maximum(jnp.arange(l)[:, None] + l - jnp.arange(2 * l)[None, :], 0)
    max_exact = REL_BUCKETS // 2
    nf = jnp.maximum(dist, 1).astype(F32)
    large = max_exact + (jnp.log(nf / max_exact) / math.log(REL_MAX_DIST / max_exact)
                         * (REL_BUCKETS - max_exact)).astype(jnp.int32)
    large = jnp.minimum(large, REL_BUCKETS - 1)
    return jnp.where(dist < max_exact, dist, large).astype(jnp.int32).T


def kernel(x, mem, w_in, ret_gn_g, swa_sinks, rel_bias, w_o, ln1_g, ln1_b, xa_wq, xa_wkv, xa_wo,
           ln2_g, ln2_b, ffn_w_up, ffn_conv_w, ffn_conv_b, ffn_w_down, ln3_g, ln3_b):
    batch, seq, d = x.shape
    mem_len = mem.shape[1]
    assert w_in.shape[0] == DEPTH and w_in.shape[2] == IN_W
    assert seq % 1024 == 0 and d % LANES == 0

    cosf, sinf = _rope_tables(seq)
    dmat, kdec, qdec, cdec = _retention_tables()
    bucket = _t5_bucket_table()
    bias = _bias_table(rel_bias, bucket)
    row = lambda t: t.reshape(1, -1)

    x2d = x.reshape(batch * seq, d)
    mem_bf = mem.reshape(batch * mem_len, d).astype(BF16)
    for l in range(DEPTH):
        proj = _in_proj(x2d, w_in[l], cosf, sinf, seq)
        o_r, (wo_bf, wkv_bf, wq_bf, xwo_bf) = _retention(
            proj, cdec, dmat, kdec, qdec, row(ret_gn_g[l]), batch, seq,
            [w_o[l], xa_wkv[l], xa_wq[l], xa_wo[l]])
        o_s, (wup_bf, wdown_bf) = _swa(proj, swa_sinks[l], bias, batch, seq,
                                       [ffn_w_up[l], ffn_w_down[l]])
        x2d = _proj_res_ln(o_r, o_s, 0, 0, wo_bf, x2d, row(ln1_g[l]), row(ln1_b[l]))

        kv = _matmul(mem_bf, wkv_bf, tm=batch * mem_len, tn=1024)
        xa = _xattn(x2d, wq_bf, kv, seq, mem_len)
        x2d = _proj_res_ln(xa, xa, 0, 1, xwo_bf, x2d, row(ln2_g[l]), row(ln2_b[l]))

        x2d = _ffn(x2d, wup_bf, ffn_conv_w[l], row(ffn_conv_b[l]),
                   wdown_bf, row(ln3_g[l]), row(ln3_b[l]), seq)
    return x2d.reshape(batch, seq, d)
```

```python
import functools
import math

import jax
import jax.numpy as jnp
from jax import lax
from jax.experimental import pallas as pl
from jax.experimental.pallas import tpu as pltpu

RET_HEADS = 8
RET_D = 128
RET_CHUNK = 128
ROPE_BASE = 10000.0
SWA_HEADS = 16
SWA_KV_HEADS = 4
SWA_HD = 64
SWA_WINDOW = 128
SWA_BLOCK = 128
REL_BUCKETS = 32
REL_MAX_DIST = 128
XA_HEADS = 4
CONV_W = 3
LN_EPS = 1e-5
DEPTH = 1
DN_ALPHA = (2 * DEPTH) ** 0.25

RET_W = RET_HEADS * RET_D
SWA_W = SWA_HEADS * SWA_HD
SWA_KV_W = SWA_KV_HEADS * SWA_HD
OFF_RQ, OFF_RK, OFF_RV, OFF_RG = 0, RET_W, 2 * RET_W, 3 * RET_W
OFF_SQ = 4 * RET_W
OFF_SK = OFF_SQ + SWA_W
OFF_SV = OFF_SK + SWA_KV_W
IN_W = OFF_SV + SWA_KV_W

LANES = 128
MXU_COLS = 256
MIB = 1024 * 1024
V7X_VMEM_BYTES = 64 * MIB
VMEM_LIMIT = V7X_VMEM_BYTES - 8 * MIB

BF16 = jnp.bfloat16
F32 = jnp.float32


def _cparams(sem, vmem=VMEM_LIMIT):
    return pltpu.CompilerParams(dimension_semantics=sem, vmem_limit_bytes=vmem)


def _layer_norm(y, g, b):
    mu = jnp.mean(y, axis=-1, keepdims=True)
    d = y - mu
    var = jnp.mean(d * d, axis=-1, keepdims=True)
    return d * lax.rsqrt(var + LN_EPS) * g + b


def _dot(a, b):
    return jnp.dot(a, b, preferred_element_type=F32)


def _dot_nt(a, b):
    return lax.dot_general(a, b, (((1,), (1,)), ((), ())), preferred_element_type=F32)


class _CastPlan:
    def __init__(self, weights, steps, step_of):
        self.weights = list(weights)
        self.ranges, self.specs = [], []
        per_weight = steps // len(self.weights)
        start = 0
        for w in self.weights:
            rows = w.shape[0]
            nblocks = next(nb for nb in range(per_weight, 0, -1)
                           if rows % nb == 0 and (rows // nb) % 16 == 0)
            self.ranges.append((start, nblocks))
            self.specs.append(self._spec(w, rows // nblocks, start, nblocks, step_of))
            start += nblocks
        assert start <= steps

    @staticmethod
    def _spec(w, block_rows, start, nblocks, step_of):
        idx = lambda *g: (jnp.clip(step_of(*g) - start, 0, nblocks - 1), 0)
        return pl.BlockSpec((block_rows, w.shape[1]), idx)

    def out_shapes(self):
        return [jax.ShapeDtypeStruct(w.shape, BF16) for w in self.weights]

    def emit(self, step, w_refs, wb_refs):
        for (start, nblocks), w_ref, wb_ref in zip(self.ranges, w_refs, wb_refs):
            @pl.when((step >= start) & (step < start + nblocks))
            def _(w_ref=w_ref, wb_ref=wb_ref):
                wb_ref[...] = w_ref[...].astype(BF16)


def _dot_tn(a, b):
    return lax.dot_general(a, b, (((0,), (0,)), ((), ())), preferred_element_type=F32)


IN_PROJ_GROUP_ROWS = 256
IN_PROJ_VMEM_LIMIT = V7X_VMEM_BYTES - 4 * MIB


def _in_proj_kernel(x_ref, w_ref, cos_ref, sin_ref, o_ref, xb_ref, wb_ref, *, tn):
    j = pl.program_id(1)
    n_q = OFF_RK // tn
    n_rope = OFF_RV // tn
    scale = jnp.where(j >= n_q, RET_D ** -0.5, 1.0).astype(F32)
    tm = x_ref.shape[0]
    half = IN_PROJ_GROUP_ROWS

    def body(first, rope):
        wb_ref[...] = w_ref[...].astype(BF16)
        for mh in range(tm // IN_PROJ_GROUP_ROWS):
            rows = slice(mh * half, (mh + 1) * half)
            if first:
                xb_ref[rows, :] = x_ref[rows, :].astype(BF16)
            for s in range(tn // MXU_COLS):
                cols = slice(s * MXU_COLS, (s + 1) * MXU_COLS)
                acc = _dot(xb_ref[rows, :], wb_ref[:, cols])
                if not rope:
                    o_ref[rows, cols] = acc.astype(o_ref.dtype)
                    continue
                cos = cos_ref[rows, :]
                sin = sin_ref[rows, :]
                for c in range(MXU_COLS // RET_D):
                    a = acc[:, c * RET_D:(c + 1) * RET_D]
                    r = pltpu.roll(a, RET_D // 2, 1)
                    lo = s * MXU_COLS + c * RET_D
                    o_ref[rows, lo:lo + RET_D] = ((a * cos + r * sin) * scale).astype(o_ref.dtype)

    pl.when(j == 0)(functools.partial(body, True, True))
    pl.when((j > 0) & (j < n_rope))(functools.partial(body, False, True))
    pl.when(j >= n_rope)(functools.partial(body, False, False))


def _in_proj(x2d, w, cosf, sinf, seq, *, tm=2048, tn=512):
    m, d = x2d.shape
    n = w.shape[1]
    assert OFF_RK % tn == 0 and OFF_RV % tn == 0 and OFF_RK >= tn
    pos_blocks = seq // tm
    tbl_idx = lambda i, j: (i % pos_blocks, 0)
    return pl.pallas_call(
        functools.partial(_in_proj_kernel, tn=tn),
        out_shape=jax.ShapeDtypeStruct((m, n), BF16),
        grid=(m // tm, n // tn),
        in_specs=[
            pl.BlockSpec((tm, d), lambda i, j: (i, 0)),
            pl.BlockSpec((d, tn), lambda i, j: (0, j)),
            pl.BlockSpec((tm, RET_D), tbl_idx),
            pl.BlockSpec((tm, RET_D), tbl_idx),
        ],
        out_specs=pl.BlockSpec((tm, tn), lambda i, j: (i, j)),
        scratch_shapes=[pltpu.VMEM((tm, d), BF16), pltpu.VMEM((d, tn), BF16)],
        compiler_params=_cparams(("parallel", "arbitrary"), IN_PROJ_VMEM_LIMIT),
        name="in_proj_rope",
    )(x2d, w, cosf, sinf)


def _retention_kernel(cdec_ref, q_ref, k_ref, v_ref, g_ref, dmat_ref, kdec_ref, qdec_ref,
                      gn_ref, *rest, chunks, casts):
    nw = len(casts.weights)
    w_refs, o_ref, wb_refs, state_ref = rest[:nw], rest[nw], rest[nw + 1:2 * nw + 1], rest[2 * nw + 1]
    n = pl.program_id(1)
    casts.emit(pl.program_id(0) * pl.num_programs(1) + n, w_refs, wb_refs)
    c_sz = RET_CHUNK

    @pl.when(n == 0)
    def _():
        state_ref[...] = jnp.zeros_like(state_ref)

    units = [(h, c) for h in range(RET_HEADS) for c in range(chunks)]
    rows_of = lambda c: slice(c * c_sz, (c + 1) * c_sz)
    cols_of = lambda h: slice(h * RET_D, (h + 1) * RET_D)
    scores_t = {(h, c): (_dot_nt(k_ref[rows_of(c), cols_of(h)], q_ref[rows_of(c), cols_of(h)])
                         * dmat_ref[h]).astype(BF16) for h, c in units}
    kvs_t = {(h, c): _dot_tn(v_ref[rows_of(c), cols_of(h)],
                             (k_ref[rows_of(c), cols_of(h)].astype(F32) * kdec_ref[h]).astype(BF16))
             for h, c in units}
    prevs_t = {}
    for h in range(RET_HEADS):
        state_t = state_ref[h]
        cdec = cdec_ref[h]
        for c in range(chunks):
            prevs_t[(h, c)] = state_t.astype(BF16)
            state_t = state_t * cdec + kvs_t[(h, c)]
        state_ref[h] = state_t
    for h, c in units:
        rows, cols = rows_of(c), cols_of(h)
        o_t = (_dot_tn(v_ref[rows, cols], scores_t[(h, c)])
               + _dot_nt(prevs_t[(h, c)], q_ref[rows, cols]) * qdec_ref[h])
        mu = jnp.mean(o_t, axis=0, keepdims=True)
        dlt = o_t - mu
        var = jnp.mean(dlt * dlt, axis=0, keepdims=True)
        on = (dlt * lax.rsqrt(var + LN_EPS) * gn_ref[h]).T
        g = g_ref[rows, cols].astype(F32)
        o_ref[rows, cols] = (g * jax.nn.sigmoid(g) * on).astype(o_ref.dtype)


def _retention(proj, cdec, dmat, kdec, qdec, gn, batch, seq, cast_ws, *, rows=512):
    m = proj.shape[0]
    nblk = seq // rows
    dmat = dmat.transpose(0, 2, 1)
    qdec = qdec.transpose(0, 2, 1)
    gn = jnp.broadcast_to(gn.reshape(RET_HEADS, RET_D)[:, :, None], (RET_HEADS, RET_D, RET_CHUNK))
    spec = lambda off: pl.BlockSpec((rows, RET_W), lambda b, n: (b * nblk + n, off // RET_W))
    table = lambda t: pl.BlockSpec(t.shape, lambda b, n: (0,) * t.ndim)
    casts = _CastPlan(cast_ws, batch * nblk, lambda b, n: b * nblk + n)
    outs = pl.pallas_call(
        functools.partial(_retention_kernel, chunks=rows // RET_CHUNK, casts=casts),
        out_shape=[jax.ShapeDtypeStruct((m, RET_W), BF16)] + casts.out_shapes(),
        grid=(batch, nblk),
        in_specs=[
            pl.BlockSpec(memory_space=pltpu.SMEM),
            spec(OFF_RQ), spec(OFF_RK), spec(OFF_RV), spec(OFF_RG),
            table(dmat), table(kdec), table(qdec), table(gn),
        ] + casts.specs,
        out_specs=[pl.BlockSpec((rows, RET_W), lambda b, n: (b * nblk + n, 0))] + casts.specs,
        scratch_shapes=[pltpu.VMEM((RET_HEADS, RET_D, RET_D), F32)],
        compiler_params=_cparams(("arbitrary", "arbitrary")),
        name="retention",
    )(cdec, proj, proj, proj, proj, dmat, kdec, qdec, gn, *casts.weights)
    return outs[0], outs[1:]


def _bias_kernel(rel_ref, bucket_ref, o_ref):
    bucket = bucket_ref[...]
    l = bucket.shape[1]
    j = lax.broadcasted_iota(jnp.int32, bucket.shape, 0)
    i = lax.broadcasted_iota(jnp.int32, bucket.shape, 1)
    dist = i + l - j
    masked = jnp.where((dist >= 0) & (dist < SWA_WINDOW), bucket, -1)
    outside = jnp.where(masked < 0, -jnp.inf, 0.0).astype(F32)

    def head(h, carry):
        def body(b, acc):
            return jnp.where(masked == b, rel_ref[b, h], acc)

        table = lax.fori_loop(0, REL_BUCKETS, body, outside)
        o_ref[h] = table
        o_ref[SWA_HEADS + h] = jnp.where(j < l, -jnp.inf, table)
        return carry

    lax.fori_loop(0, SWA_HEADS, head, 0)


def _bias_table(rel_bias, bucket):
    l2, l = bucket.shape
    return pl.pallas_call(
        _bias_kernel,
        out_shape=jax.ShapeDtypeStruct((2 * SWA_HEADS, l2, l), F32),
        in_specs=[pl.BlockSpec(memory_space=pltpu.SMEM),
                  pl.BlockSpec(memory_space=pltpu.VMEM)],
        out_specs=pl.BlockSpec(memory_space=pltpu.VMEM),
        name="t5_bias_table",
    )(rel_bias, bucket)


SWA_QK_AHEAD = 2


def _swap_lane_halves(x):
    u = pltpu.bitcast(x, jnp.uint32)
    return pltpu.bitcast(pltpu.roll(u, LANES // 2, 1), x.dtype)


def _swa_kernel(sink_ref, q_ref, kc_ref, vc_ref, kp_ref, vp_ref, bias_ref, bias0_ref, *rest,
                blocks, casts):
    nw = len(casts.weights)
    w_refs, o_ref, wb_refs = rest[:nw], rest[nw], rest[nw + 1:]
    casts.emit(pl.program_id(0) * pl.num_programs(1) + pl.program_id(1), w_refs, wb_refs)
    l = SWA_BLOCK
    grp = SWA_HEADS // SWA_KV_HEADS
    per_tile = LANES // SWA_HD
    low = lax.broadcasted_iota(jnp.int32, (2 * l, LANES), 1) < SWA_HD
    zero_k = jnp.zeros((2 * l, LANES), BF16)
    zero_v = jnp.zeros((SWA_HD, 2 * l), BF16)
    tiles = {}

    def kv_tile(blk, t):
        if (blk, t) not in tiles:
            r0 = blk * l
            cols = slice(t * LANES, (t + 1) * LANES)
            if blk == 0:
                kt = jnp.concatenate([kp_ref[:, cols], kc_ref[0:l, cols]], axis=0)
                vt = jnp.concatenate([vp_ref[:, cols], vc_ref[0:l, cols]], axis=0)
            else:
                kt = kc_ref[r0 - l:r0 + l, cols]
                vt = vc_ref[r0 - l:r0 + l, cols]
            kt = kt * (SWA_HD ** -0.5)
            tiles[(blk, t)] = (kt, _swap_lane_halves(kt), vt.astype(F32).T.astype(BF16))
        return tiles[(blk, t)]

    tasks = [(blk, kh) for blk in range(blocks) for kh in range(SWA_KV_HEADS)]

    def qk(task):
        blk, kh = task
        t, sub = divmod(kh, per_tile)
        kt, kr, _ = kv_tile(blk, t)
        k_lo, k_hi = (kt, kr) if sub == 0 else (kr, kt)
        k2 = jnp.concatenate([jnp.where(low, k_lo, zero_k), jnp.where(low, zero_k, k_hi)], axis=0)
        rows = slice(blk * l, (blk + 1) * l)
        qcat = jnp.concatenate([q_ref[rows, (kh * grp // 2 + pp) * LANES:(kh * grp // 2 + pp + 1) * LANES]
                                for pp in range(grp // 2)], axis=0)
        return _dot_nt(k2, qcat)

    def finish(task, st):
        blk, kh = task
        t, sub = divmod(kh, per_tile)
        vh = kv_tile(blk, t)[2][sub * SWA_HD:(sub + 1) * SWA_HD, :]
        v2t = jnp.concatenate([jnp.concatenate([vh, zero_v], axis=1),
                               jnp.concatenate([zero_v, vh], axis=1)], axis=0)
        p_rows, den_rows = [], []
        for s in range(2):
            p_cols, den_cols = [], []
            for pp in range(grp // 2):
                hd = kh * grp + 2 * pp + s
                bias = bias0_ref[hd] if blk == 0 else bias_ref[hd]
                lg = st[s * 2 * l:(s + 1) * 2 * l, pp * l:(pp + 1) * l] + bias
                sink = sink_ref[hd]
                mx = jnp.maximum(jnp.max(lg, axis=0, keepdims=True), sink)
                p = jnp.exp(lg - mx)
                den = jnp.sum(p, axis=0, keepdims=True) + jnp.exp(sink - mx)
                p_cols.append(p.astype(BF16))
                den_cols.append(jnp.broadcast_to(den, (SWA_HD, l)))
            p_rows.append(jnp.concatenate(p_cols, axis=1))
            den_rows.append(jnp.concatenate(den_cols, axis=1))
        out_t = _dot(v2t, jnp.concatenate(p_rows, axis=0))
        out_t = out_t / jnp.concatenate(den_rows, axis=0)
        out = jnp.concatenate([out_t[:, pp * l:(pp + 1) * l].T for pp in range(grp // 2)], axis=1)
        o_ref[blk * l:(blk + 1) * l, kh * grp * SWA_HD:(kh + 1) * grp * SWA_HD] = out.astype(o_ref.dtype)

    pending = [qk(t) for t in tasks[:SWA_QK_AHEAD]]
    for n, task in enumerate(tasks):
        if n + SWA_QK_AHEAD < len(tasks):
            pending.append(qk(tasks[n + SWA_QK_AHEAD]))
        finish(task, pending.pop(0))


def _swa(proj, sinks, bias, batch, seq, cast_ws, *, rows=256):
    m = proj.shape[0]
    nblk = seq // rows
    per = rows // SWA_BLOCK
    kcol, vcol = OFF_SK // SWA_KV_W, OFF_SV // SWA_KV_W

    def prev_idx(b, n):
        return jnp.maximum((b * nblk + n) * per - 1, 0)

    casts = _CastPlan(cast_ws, batch * nblk, lambda b, n: b * nblk + n)
    outs = pl.pallas_call(
        functools.partial(_swa_kernel, blocks=per, casts=casts),
        out_shape=[jax.ShapeDtypeStruct((m, SWA_W), BF16)] + casts.out_shapes(),
        grid=(batch, nblk),
        in_specs=[
            pl.BlockSpec(memory_space=pltpu.SMEM),
            pl.BlockSpec((rows, SWA_W), lambda b, n: (b * nblk + n, OFF_SQ // SWA_W)),
            pl.BlockSpec((rows, SWA_KV_W), lambda b, n: (b * nblk + n, kcol)),
            pl.BlockSpec((rows, SWA_KV_W), lambda b, n: (b * nblk + n, vcol)),
            pl.BlockSpec((SWA_BLOCK, SWA_KV_W), lambda b, n: (prev_idx(b, n), kcol)),
            pl.BlockSpec((SWA_BLOCK, SWA_KV_W), lambda b, n: (prev_idx(b, n), vcol)),
            pl.BlockSpec((SWA_HEADS, 2 * SWA_BLOCK, SWA_BLOCK), lambda b, n: (0, 0, 0)),
            pl.BlockSpec((SWA_HEADS, 2 * SWA_BLOCK, SWA_BLOCK), lambda b, n: (jnp.where(n == 0, 1, 0), 0, 0)),
        ] + casts.specs,
        out_specs=[pl.BlockSpec((rows, SWA_W), lambda b, n: (b * nblk + n, 0))] + casts.specs,
        compiler_params=_cparams(("arbitrary", "arbitrary")),
        name="swa_sink_attention",
    )(sinks, proj, proj, proj, proj, proj, bias, bias, *casts.weights)
    return outs[0], outs[1:]


LN_GROUP_ROWS = 256


def _proj_res_ln_kernel(a1_ref, a2_ref, w1_ref, w2_ref, x_ref, g_ref, b_ref, o_ref):
    for r in range(x_ref.shape[0] // LN_GROUP_ROWS):
        rows = slice(r * LN_GROUP_ROWS, (r + 1) * LN_GROUP_ROWS)
        y = _dot(a1_ref[rows, :], w1_ref[...]) + _dot(a2_ref[rows, :], w2_ref[...])
        o_ref[rows, :] = _layer_norm(DN_ALPHA * x_ref[rows, :] + y, g_ref[...], b_ref[...])


def _proj_res_ln(a1, a2, col1, col2, w_bf, x2d, g, b, *, tm=1024):
    m, d = x2d.shape
    kh = w_bf.shape[0] // 2
    return pl.pallas_call(
        _proj_res_ln_kernel,
        out_shape=jax.ShapeDtypeStruct((m, d), F32),
        grid=(m // tm,),
        in_specs=[
            pl.BlockSpec((tm, kh), lambda i: (i, col1)),
            pl.BlockSpec((tm, kh), lambda i: (i, col2)),
            pl.BlockSpec((kh, d), lambda i: (0, 0), pipeline_mode=pl.Buffered(1)),
            pl.BlockSpec((kh, d), lambda i: (1, 0), pipeline_mode=pl.Buffered(1)),
            pl.BlockSpec((tm, d), lambda i: (i, 0)),
            pl.BlockSpec((1, d), lambda i: (0, 0)),
            pl.BlockSpec((1, d), lambda i: (0, 0)),
        ],
        out_specs=pl.BlockSpec((tm, d), lambda i: (i, 0)),
        compiler_params=_cparams(("parallel",)),
        name="proj_residual_ln",
    )(a1, a2, w_bf, w_bf, x2d, g, b)


def _matmul_kernel(a_ref, w_ref, o_ref):
    o_ref[...] = _dot(a_ref[...], w_ref[...]).astype(o_ref.dtype)


def _matmul(a_bf, w_bf, *, tm, tn):
    m, k = a_bf.shape
    n = w_bf.shape[1]
    return pl.pallas_call(
        _matmul_kernel,
        out_shape=jax.ShapeDtypeStruct((m, n), BF16),
        grid=(m // tm, n // tn),
        in_specs=[pl.BlockSpec((tm, k), lambda i, j: (i, 0)),
                  pl.BlockSpec((k, tn), lambda i, j: (0, j))],
        out_specs=pl.BlockSpec((tm, tn), lambda i, j: (i, j)),
        compiler_params=_cparams(("parallel", "arbitrary")),
        name="matmul",
    )(a_bf, w_bf)


XA_GROUP_ROWS = 256


def _xattn_kernel(x_ref, wq_ref, k_ref, v_ref, o_ref, *, hd):
    groups = x_ref.shape[0] // XA_GROUP_ROWS
    tasks = [(r, h) for r in range(groups) for h in range(XA_HEADS)]
    xb = {}

    def qproj(task):
        r, h = task
        if r not in xb:
            xb[r] = x_ref[r * XA_GROUP_ROWS:(r + 1) * XA_GROUP_ROWS, :].astype(BF16)
        return _dot(xb[r], wq_ref[:, h * hd:(h + 1) * hd]).astype(BF16)

    q_next = qproj(tasks[0])
    for n, (r, h) in enumerate(tasks):
        q = q_next
        if n + 1 < len(tasks):
            q_next = qproj(tasks[n + 1])
        cols = slice(h * hd, (h + 1) * hd)
        logits = _dot_nt(q, k_ref[:, cols]) * (hd ** -0.5)
        mx = jnp.max(logits, axis=-1, keepdims=True)
        p = jnp.exp(logits - mx)
        den = jnp.sum(p, axis=-1, keepdims=True)
        rows = slice(r * XA_GROUP_ROWS, (r + 1) * XA_GROUP_ROWS)
        o_ref[rows, cols] = (_dot(p.astype(BF16), v_ref[:, cols]) / den).astype(o_ref.dtype)


def _xattn(x2d, wq_bf, kv, seq, mem_len, *, tm=1024):
    m, d = x2d.shape
    hd = d // XA_HEADS
    per_b = seq // tm
    return pl.pallas_call(
        functools.partial(_xattn_kernel, hd=hd),
        out_shape=jax.ShapeDtypeStruct((m, d), BF16),
        grid=(m // tm,),
        in_specs=[
            pl.BlockSpec((tm, d), lambda i: (i, 0)),
            pl.BlockSpec((d, d), lambda i: (0, 0), pipeline_mode=pl.Buffered(1)),
            pl.BlockSpec((mem_len, d), lambda i: (i // per_b, 0)),
            pl.BlockSpec((mem_len, d), lambda i: (i // per_b, 1)),
        ],
        out_specs=pl.BlockSpec((tm, d), lambda i: (i, 0)),
        compiler_params=_cparams(("parallel",)),
        name="memory_cross_attention",
    )(x2d, wq_bf, kv, kv)


FFN_HALO = 16
FFN_GROUP_ROWS = 256
FFN_CHUNK = 256


def _ffn_kernel(x_ref, xh_ref, wu_ref, wg_ref, cw_ref, cb_ref, wd_ref, g_ref, b_ref, o_ref,
                xb_ref, *, tm, per_b):
    i = pl.program_id(0)
    f = pl.program_id(1)
    nf = pl.num_programs(1)

    groups = tm // FFN_GROUP_ROWS

    def body(first, last):
        cw = cw_ref[...]
        cb = cb_ref[...]

        def up(r):
            r0 = r * FFN_GROUP_ROWS
            if first:
                xb_ref[FFN_HALO + r0:FFN_HALO + r0 + FFN_GROUP_ROWS, :] = (
                    x_ref[r0:r0 + FFN_GROUP_ROWS, :].astype(BF16))
            u = _dot(xb_ref[FFN_HALO + r0:FFN_HALO + r0 + FFN_GROUP_ROWS, :], wu_ref[...])
            ge = _dot(xb_ref[r0:r0 + FFN_GROUP_ROWS + FFN_HALO, :], wg_ref[...])
            return u, ge

        if first:
            halo = jnp.where(i % per_b == 0, 0.0, xh_ref[...])
            xb_ref[0:FFN_HALO, :] = halo.astype(BF16)
        nxt = up(0)
        for r in range(groups):
            u, ge = nxt
            if r + 1 < groups:
                nxt = up(r + 1)
            gc = cb
            for tap in range(CONV_W):
                lo = FFN_HALO - (CONV_W - 1) + tap
                gc = gc + ge[lo:lo + FFN_GROUP_ROWS, :] * cw[tap:tap + 1, :]
            hcur = (gc * jax.nn.sigmoid(gc) * u).astype(BF16)
            rows = slice(r * FFN_GROUP_ROWS, (r + 1) * FFN_GROUP_ROWS)
            acc = _dot(hcur, wd_ref[...])
            acc = (DN_ALPHA * x_ref[rows, :] if first else o_ref[rows, :]) + acc
            if last:
                acc = _layer_norm(acc, g_ref[...], b_ref[...])
            o_ref[rows, :] = acc

    pl.when(f == 0)(functools.partial(body, True, False))
    pl.when((f > 0) & (f < nf - 1))(functools.partial(body, False, False))
    pl.when(f == nf - 1)(functools.partial(body, False, True))


def _ffn(x2d, wup_bf, conv_w, conv_b, wdown_bf, g, b, seq, *, tm=1024, fc=FFN_CHUNK):
    m, d = x2d.shape
    nf = wdown_bf.shape[0] // fc
    assert nf >= 2, "the first and the last hidden chunk are separate code paths"
    per_b = seq // tm
    halo_per_tile = tm // FFN_HALO
    return pl.pallas_call(
        functools.partial(_ffn_kernel, tm=tm, per_b=per_b),
        out_shape=jax.ShapeDtypeStruct((m, d), F32),
        grid=(m // tm, nf),
        in_specs=[
            pl.BlockSpec((tm, d), lambda i, f: (i, 0)),
            pl.BlockSpec((FFN_HALO, d), lambda i, f: (jnp.maximum(i * halo_per_tile - 1, 0), 0)),
            pl.BlockSpec((d, fc), lambda i, f: (0, f)),
            pl.BlockSpec((d, fc), lambda i, f: (0, nf + f)),
            pl.BlockSpec((CONV_W, fc), lambda i, f: (0, f)),
            pl.BlockSpec((1, fc), lambda i, f: (0, f)),
            pl.BlockSpec((fc, d), lambda i, f: (f, 0)),
            pl.BlockSpec((1, d), lambda i, f: (0, 0)),
            pl.BlockSpec((1, d), lambda i, f: (0, 0)),
        ],
        out_specs=pl.BlockSpec((tm, d), lambda i, f: (i, 0)),
        scratch_shapes=[pltpu.VMEM((tm + FFN_HALO, d), BF16)],
        compiler_params=_cparams(("parallel", "arbitrary")),
        name="conv_ffn_ln",
    )(x2d, x2d, wup_bf, wup_bf, conv_w, conv_b, wdown_bf, g, b)


def _rope_tables(seq):
    half = RET_D // 2
    inv = 1.0 / (ROPE_BASE ** (jnp.arange(half, dtype=F32) / half))
    ang = jnp.arange(seq).astype(F32)[:, None] * inv[None, :]
    cos, sin = jnp.cos(ang), jnp.sin(ang)
    return jnp.concatenate([cos, cos], axis=-1), jnp.concatenate([-sin, sin], axis=-1)


def _retention_tables():
    c = RET_CHUNK
    log_gamma = jnp.log1p(-jnp.exp2(-5.0 - jnp.arange(RET_HEADS, dtype=F32)))
    idx = jnp.arange(c, dtype=F32)
    diff = idx[:, None] - idx[None, :]
    dmat = jnp.where(diff[None] >= 0,
                     jnp.exp(log_gamma[:, None, None] * jnp.maximum(diff, 0.0)[None]), 0.0)
    kdec = jnp.exp(log_gamma[:, None] * (c - 1 - idx)[None, :])
    qdec = jnp.exp(log_gamma[:, None] * (idx + 1.0)[None, :])
    cdec = jnp.exp(log_gamma * c)
    bc = lambda t: jnp.broadcast_to(t[:, :, None], (RET_HEADS, c, RET_D))
    return dmat, bc(kdec), bc(qdec), cdec


def _t5_bucket_table():
    l = SWA_BLOCK
    dist = jnp.maximum(jnp.arange(l)[:, None] + l - jnp.arange(2 * l)[None, :], 0)
    max_exact = REL_BUCKETS // 2
    nf = jnp.maximum(dist, 1).astype(F32)
    large = max_exact + (jnp.log(nf / max_exact) / math.log(REL_MAX_DIST / max_exact)
                         * (REL_BUCKETS - max_exact)).astype(jnp.int32)
    large = jnp.minimum(large, REL_BUCKETS - 1)
    return jnp.where(dist < max_exact, dist, large).astype(jnp.int32).T


def kernel(x, mem, w_in, ret_gn_g, swa_sinks, rel_bias, w_o, ln1_g, ln1_b, xa_wq, xa_wkv, xa_wo,
           ln2_g, ln2_b, ffn_w_up, ffn_conv_w, ffn_conv_b, ffn_w_down, ln3_g, ln3_b):
    batch, seq, d = x.shape
    mem_len = mem.shape[1]
    assert w_in.shape[0] == DEPTH and w_in.shape[2] == IN_W
    assert seq % 1024 == 0 and d % LANES == 0

    cosf, sinf = _rope_tables(seq)
    dmat, kdec, qdec, cdec = _retention_tables()
    bucket = _t5_bucket_table()
    bias = _bias_table(rel_bias, bucket)
    row = lambda t: t.reshape(1, -1)

    x2d = x.reshape(batch * seq, d)
    mem_bf = mem.reshape(batch * mem_len, d).astype(BF16)
    for l in range(DEPTH):
        proj = _in_proj(x2d, w_in[l], cosf, sinf, seq)
        o_r, (wo_bf, wkv_bf, wq_bf, xwo_bf) = _retention(
            proj, cdec, dmat, kdec, qdec, row(ret_gn_g[l]), batch, seq,
            [w_o[l], xa_wkv[l], xa_wq[l], xa_wo[l]])
        o_s, (wup_bf, wdown_bf) = _swa(proj, swa_sinks[l], bias, batch, seq,
                                       [ffn_w_up[l], ffn_w_down[l]])
        x2d = _proj_res_ln(o_r, o_s, 0, 0, wo_bf, x2d, row(ln1_g[l]), row(ln1_b[l]))

        kv = _matmul(mem_bf, wkv_bf, tm=batch * mem_len, tn=1024)
        xa = _xattn(x2d, wq_bf, kv, seq, mem_len)
        x2d = _proj_res_ln(xa, xa, 0, 1, xwo_bf, x2d, row(ln2_g[l]), row(ln2_b[l]))

        x2d = _ffn(x2d, wup_bf, ffn_conv_w[l], row(ffn_conv_b[l]),
                   wdown_bf, row(ln3_g[l]), row(ln3_b[l]), seq)
    return x2d.reshape(batch, seq, d)
```

```python
import functools
import math

import jax
import jax.numpy as jnp
from jax import lax
from jax.experimental import pallas as pl
from jax.experimental.pallas import tpu as pltpu

RET_HEADS = 8
RET_D = 128
RET_CHUNK = 128
ROPE_BASE = 10000.0
SWA_HEADS = 16
SWA_KV_HEADS = 4
SWA_HD = 64
SWA_WINDOW = 128
SWA_BLOCK = 128
REL_BUCKETS = 32
REL_MAX_DIST = 128
XA_HEADS = 4
CONV_W = 3
LN_EPS = 1e-5
DEPTH = 1
DN_ALPHA = (2 * DEPTH) ** 0.25

RET_W = RET_HEADS * RET_D
SWA_W = SWA_HEADS * SWA_HD
SWA_KV_W = SWA_KV_HEADS * SWA_HD
OFF_RQ, OFF_RK, OFF_RV, OFF_RG = 0, RET_W, 2 * RET_W, 3 * RET_W
OFF_SQ = 4 * RET_W
OFF_SK = OFF_SQ + SWA_W
OFF_SV = OFF_SK + SWA_KV_W
IN_W = OFF_SV + SWA_KV_W

LANES = 128
MXU_COLS = 256
MIB = 1024 * 1024
V7X_VMEM_BYTES = 64 * MIB
VMEM_LIMIT = V7X_VMEM_BYTES - 8 * MIB

BF16 = jnp.bfloat16
F32 = jnp.float32


def _cparams(sem, vmem=VMEM_LIMIT):
    return pltpu.CompilerParams(dimension_semantics=sem, vmem_limit_bytes=vmem)


def _layer_norm(y, g, b):
    mu = jnp.mean(y, axis=-1, keepdims=True)
    d = y - mu
    var = jnp.mean(d * d, axis=-1, keepdims=True)
    return d * lax.rsqrt(var + LN_EPS) * g + b


def _dot(a, b):
    return jnp.dot(a, b, preferred_element_type=F32)


def _dot_nt(a, b):
    return lax.dot_general(a, b, (((1,), (1,)), ((), ())), preferred_element_type=F32)


class _CastPlan:
    def __init__(self, weights, steps, step_of):
        self.weights = list(weights)
        self.ranges, self.specs = [], []
        per_weight = steps // len(self.weights)
        start = 0
        for w in self.weights:
            rows = w.shape[0]
            nblocks = next(nb for nb in range(per_weight, 0, -1)
                           if rows % nb == 0 and (rows // nb) % 16 == 0)
            self.ranges.append((start, nblocks))
            self.specs.append(self._spec(w, rows // nblocks, start, nblocks, step_of))
            start += nblocks
        assert start <= steps

    @staticmethod
    def _spec(w, block_rows, start, nblocks, step_of):
        idx = lambda *g: (jnp.clip(step_of(*g) - start, 0, nblocks - 1), 0)
        return pl.BlockSpec((block_rows, w.shape[1]), idx)

    def out_shapes(self):
        return [jax.ShapeDtypeStruct(w.shape, BF16) for w in self.weights]

    def emit(self, step, w_refs, wb_refs):
        for (start, nblocks), w_ref, wb_ref in zip(self.ranges, w_refs, wb_refs):
            @pl.when((step >= start) & (step < start + nblocks))
            def _(w_ref=w_ref, wb_ref=wb_ref):
                wb_ref[...] = w_ref[...].astype(BF16)


def _dot_tn(a, b):
    return lax.dot_general(a, b, (((0,), (0,)), ((), ())), preferred_element_type=F32)


IN_PROJ_GROUP_ROWS = 256
IN_PROJ_VMEM_LIMIT = V7X_VMEM_BYTES - 4 * MIB


def _in_proj_kernel(x_ref, w_ref, cos_ref, sin_ref, o_ref, xb_ref, wb_ref, *, tn):
    j = pl.program_id(1)
    n_q = OFF_RK // tn
    n_rope = OFF_RV // tn
    scale = jnp.where(j >= n_q, RET_D ** -0.5, 1.0).astype(F32)
    tm = x_ref.shape[0]
    half = IN_PROJ_GROUP_ROWS

    def body(first, rope):
        wb_ref[...] = w_ref[...].astype(BF16)
        for mh in range(tm // IN_PROJ_GROUP_ROWS):
            rows = slice(mh * half, (mh + 1) * half)
            if first:
                xb_ref[rows, :] = x_ref[rows, :].astype(BF16)
            for s in range(tn // MXU_COLS):
                cols = slice(s * MXU_COLS, (s + 1) * MXU_COLS)
                acc = _dot(xb_ref[rows, :], wb_ref[:, cols])
                if not rope:
                    o_ref[rows, cols] = acc.astype(o_ref.dtype)
                    continue
                cos = cos_ref[rows, :]
                sin = sin_ref[rows, :]
                for c in range(MXU_COLS // RET_D):
                    a = acc[:, c * RET_D:(c + 1) * RET_D]
                    r = pltpu.roll(a, RET_D // 2, 1)
                    lo = s * MXU_COLS + c * RET_D
                    o_ref[rows, lo:lo + RET_D] = ((a * cos + r * sin) * scale).astype(o_ref.dtype)

    pl.when(j == 0)(functools.partial(body, True, True))
    pl.when((j > 0) & (j < n_rope))(functools.partial(body, False, True))
    pl.when(j >= n_rope)(functools.partial(body, False, False))


def _in_proj(x2d, w, cosf, sinf, seq, *, tm=2048, tn=512):
    m, d = x2d.shape
    n = w.shape[1]
    assert OFF_RK % tn == 0 and OFF_RV % tn == 0 and OFF_RK >= tn
    pos_blocks = seq // tm
    tbl_idx = lambda i, j: (i % pos_blocks, 0)
    return pl.pallas_call(
        functools.partial(_in_proj_kernel, tn=tn),
        out_shape=jax.ShapeDtypeStruct((m, n), BF16),
        grid=(m // tm, n // tn),
        in_specs=[
            pl.BlockSpec((tm, d), lambda i, j: (i, 0)),
            pl.BlockSpec((d, tn), lambda i, j: (0, j)),
            pl.BlockSpec((tm, RET_D), tbl_idx),
            pl.BlockSpec((tm, RET_D), tbl_idx),
        ],
        out_specs=pl.BlockSpec((tm, tn), lambda i, j: (i, j)),
        scratch_shapes=[pltpu.VMEM((tm, d), BF16), pltpu.VMEM((d, tn), BF16)],
        compiler_params=_cparams(("parallel", "arbitrary"), IN_PROJ_VMEM_LIMIT),
        name="in_proj_rope",
    )(x2d, w, cosf, sinf)


def _retention_kernel(cdec_ref, q_ref, k_ref, v_ref, g_ref, dmat_ref, kdec_ref, qdec_ref,
                      gn_ref, *rest, chunks, casts):
    nw = len(casts.weights)
    w_refs, o_ref, wb_refs, state_ref = rest[:nw], rest[nw], rest[nw + 1:2 * nw + 1], rest[2 * nw + 1]
    n = pl.program_id(1)
    casts.emit(pl.program_id(0) * pl.num_programs(1) + n, w_refs, wb_refs)
    c_sz = RET_CHUNK

    @pl.when(n == 0)
    def _():
        state_ref[...] = jnp.zeros_like(state_ref)

    units = [(h, c) for h in range(RET_HEADS) for c in range(chunks)]
    rows_of = lambda c: slice(c * c_sz, (c + 1) * c_sz)
    cols_of = lambda h: slice(h * RET_D, (h + 1) * RET_D)
    scores_t = {(h, c): (_dot_nt(k_ref[rows_of(c), cols_of(h)], q_ref[rows_of(c), cols_of(h)])
                         * dmat_ref[h]).astype(BF16) for h, c in units}
    kvs_t = {(h, c): _dot_tn(v_ref[rows_of(c), cols_of(h)],
                             (k_ref[rows_of(c), cols_of(h)].astype(F32) * kdec_ref[h]).astype(BF16))
             for h, c in units}
    prevs_t = {}
    for h in range(RET_HEADS):
        state_t = state_ref[h]
        cdec = cdec_ref[h]
        for c in range(chunks):
            prevs_t[(h, c)] = state_t.astype(BF16)
            state_t = state_t * cdec + kvs_t[(h, c)]
        state_ref[h] = state_t
    for h, c in units:
        rows, cols = rows_of(c), cols_of(h)
        o_t = (_dot_tn(v_ref[rows, cols], scores_t[(h, c)])
               + _dot_nt(prevs_t[(h, c)], q_ref[rows, cols]) * qdec_ref[h])
        mu = jnp.mean(o_t, axis=0, keepdims=True)
        dlt = o_t - mu
        var = jnp.mean(dlt * dlt, axis=0, keepdims=True)
        on = (dlt * lax.rsqrt(var + LN_EPS) * gn_ref[h]).T
        g = g_ref[rows, cols].astype(F32)
        o_ref[rows, cols] = (g * jax.nn.sigmoid(g) * on).astype(o_ref.dtype)


def _retention(proj, cdec, dmat, kdec, qdec, gn, batch, seq, cast_ws, *, rows=512):
    m = proj.shape[0]
    nblk = seq // rows
    dmat = dmat.transpose(0, 2, 1)
    qdec = qdec.transpose(0, 2, 1)
    gn = jnp.broadcast_to(gn.reshape(RET_HEADS, RET_D)[:, :, None], (RET_HEADS, RET_D, RET_CHUNK))
    spec = lambda off: pl.BlockSpec((rows, RET_W), lambda b, n: (b * nblk + n, off // RET_W))
    table = lambda t: pl.BlockSpec(t.shape, lambda b, n: (0,) * t.ndim)
    casts = _CastPlan(cast_ws, batch * nblk, lambda b, n: b * nblk + n)
    outs = pl.pallas_call(
        functools.partial(_retention_kernel, chunks=rows // RET_CHUNK, casts=casts),
        out_shape=[jax.ShapeDtypeStruct((m, RET_W), BF16)] + casts.out_shapes(),
        grid=(batch, nblk),
        in_specs=[
            pl.BlockSpec(memory_space=pltpu.SMEM),
            spec(OFF_RQ), spec(OFF_RK), spec(OFF_RV), spec(OFF_RG),
            table(dmat), table(kdec), table(qdec), table(gn),
        ] + casts.specs,
        out_specs=[pl.BlockSpec((rows, RET_W), lambda b, n: (b * nblk + n, 0))] + casts.specs,
        scratch_shapes=[pltpu.VMEM((RET_HEADS, RET_D, RET_D), F32)],
        compiler_params=_cparams(("arbitrary", "arbitrary")),
        name="retention",
    )(cdec, proj, proj, proj, proj, dmat, kdec, qdec, gn, *casts.weights)
    return outs[0], outs[1:]


def _bias_kernel(rel_ref, bucket_ref, o_ref):
    bucket = bucket_ref[...]
    l = bucket.shape[1]
    j = lax.broadcasted_iota(jnp.int32, bucket.shape, 0)
    i = lax.broadcasted_iota(jnp.int32, bucket.shape, 1)
    dist = i + l - j
    masked = jnp.where((dist >= 0) & (dist < SWA_WINDOW), bucket, -1)
    outside = jnp.where(masked < 0, -jnp.inf, 0.0).astype(F32)

    def head(h, carry):
        def body(b, acc):
            return jnp.where(masked == b, rel_ref[b, h], acc)

        table = lax.fori_loop(0, REL_BUCKETS, body, outside)
        o_ref[h] = table
        o_ref[SWA_HEADS + h] = jnp.where(j < l, -jnp.inf, table)
        return carry

    lax.fori_loop(0, SWA_HEADS, head, 0)


def _bias_table(rel_bias, bucket):
    l2, l = bucket.shape
    return pl.pallas_call(
        _bias_kernel,
        out_shape=jax.ShapeDtypeStruct((2 * SWA_HEADS, l2, l), F32),
        in_specs=[pl.BlockSpec(memory_space=pltpu.SMEM),
                  pl.BlockSpec(memory_space=pltpu.VMEM)],
        out_specs=pl.BlockSpec(memory_space=pltpu.VMEM),
        name="t5_bias_table",
    )(rel_bias, bucket)


SWA_QK_AHEAD = 2


def _swap_lane_halves(x):
    u = pltpu.bitcast(x, jnp.uint32)
    return pltpu.bitcast(pltpu.roll(u, LANES // 2, 1), x.dtype)


def _swa_kernel(sink_ref, q_ref, kc_ref, vc_ref, kp_ref, vp_ref, bias_ref, bias0_ref, *rest,
                blocks, casts):
    nw = len(casts.weights)
    w_refs, o_ref, wb_refs = rest[:nw], rest[nw], rest[nw + 1:]
    casts.emit(pl.program_id(0) * pl.num_programs(1) + pl.program_id(1), w_refs, wb_refs)
    l = SWA_BLOCK
    grp = SWA_HEADS // SWA_KV_HEADS
    per_tile = LANES // SWA_HD
    low = lax.broadcasted_iota(jnp.int32, (2 * l, LANES), 1) < SWA_HD
    zero_k = jnp.zeros((2 * l, LANES), BF16)
    zero_v = jnp.zeros((SWA_HD, 2 * l), BF16)
    tiles = {}

    def kv_tile(blk, t):
        if (blk, t) not in tiles:
            r0 = blk * l
            cols = slice(t * LANES, (t + 1) * LANES)
            if blk == 0:
                kt = jnp.concatenate([kp_ref[:, cols], kc_ref[0:l, cols]], axis=0)
                vt = jnp.concatenate([vp_ref[:, cols], vc_ref[0:l, cols]], axis=0)
            else:
                kt = kc_ref[r0 - l:r0 + l, cols]
                vt = vc_ref[r0 - l:r0 + l, cols]
            kt = kt * (SWA_HD ** -0.5)
            tiles[(blk, t)] = (kt, _swap_lane_halves(kt), vt.astype(F32).T.astype(BF16))
        return tiles[(blk, t)]

    tasks = [(blk, kh) for blk in range(blocks) for kh in range(SWA_KV_HEADS)]

    def qk(task):
        blk, kh = task
        t, sub = divmod(kh, per_tile)
        kt, kr, _ = kv_tile(blk, t)
        k_lo, k_hi = (kt, kr) if sub == 0 else (kr, kt)
        k2 = jnp.concatenate([jnp.where(low, k_lo, zero_k), jnp.where(low, zero_k, k_hi)], axis=0)
        rows = slice(blk * l, (blk + 1) * l)
        qcat = jnp.concatenate([q_ref[rows, (kh * grp // 2 + pp) * LANES:(kh * grp // 2 + pp + 1) * LANES]
                                for pp in range(grp // 2)], axis=0)
        return _dot_nt(k2, qcat)

    def softmax(task, st):
        blk, kh = task
        p_rows, den_rows = [], []
        for s in range(2):
            p_cols, den_cols = [], []
            for pp in range(grp // 2):
                hd = kh * grp + 2 * pp + s
                bias = bias0_ref[hd] if blk == 0 else bias_ref[hd]
                lg = st[s * 2 * l:(s + 1) * 2 * l, pp * l:(pp + 1) * l] + bias
                sink = sink_ref[hd]
                mx = jnp.maximum(jnp.max(lg, axis=0, keepdims=True), sink)
                p = jnp.exp(lg - mx)
                den = jnp.sum(p, axis=0, keepdims=True) + jnp.exp(sink - mx)
                p_cols.append(p.astype(BF16))
                den_cols.append(jnp.broadcast_to(den, (SWA_HD, l)))
            p_rows.append(jnp.concatenate(p_cols, axis=1))
            den_rows.append(jnp.concatenate(den_cols, axis=1))
        return jnp.concatenate(p_rows, axis=0), jnp.concatenate(den_rows, axis=0)

    def pv(task, p_t, den_t):
        blk, kh = task
        t, sub = divmod(kh, per_tile)
        vh = kv_tile(blk, t)[2][sub * SWA_HD:(sub + 1) * SWA_HD, :]
        v2t = jnp.concatenate([jnp.concatenate([vh, zero_v], axis=1),
                               jnp.concatenate([zero_v, vh], axis=1)], axis=0)
        out_t = _dot(v2t, p_t) / den_t
        out = jnp.concatenate([out_t[:, pp * l:(pp + 1) * l].T for pp in range(grp // 2)], axis=1)
        o_ref[blk * l:(blk + 1) * l, kh * grp * SWA_HD:(kh + 1) * grp * SWA_HD] = out.astype(o_ref.dtype)

    pending = [qk(t) for t in tasks[:SWA_QK_AHEAD]]
    held = None
    for n, task in enumerate(tasks):
        if n + SWA_QK_AHEAD < len(tasks):
            pending.append(qk(tasks[n + SWA_QK_AHEAD]))
        if held is not None:
            pv(*held)
        held = (task,) + softmax(task, pending.pop(0))
    pv(*held)


def _swa(proj, sinks, bias, batch, seq, cast_ws, *, rows=256):
    m = proj.shape[0]
    nblk = seq // rows
    per = rows // SWA_BLOCK
    kcol, vcol = OFF_SK // SWA_KV_W, OFF_SV // SWA_KV_W

    def prev_idx(b, n):
        return jnp.maximum((b * nblk + n) * per - 1, 0)

    casts = _CastPlan(cast_ws, batch * nblk, lambda b, n: b * nblk + n)
    outs = pl.pallas_call(
        functools.partial(_swa_kernel, blocks=per, casts=casts),
        out_shape=[jax.ShapeDtypeStruct((m, SWA_W), BF16)] + casts.out_shapes(),
        grid=(batch, nblk),
        in_specs=[
            pl.BlockSpec(memory_space=pltpu.SMEM),
            pl.BlockSpec((rows, SWA_W), lambda b, n: (b * nblk + n, OFF_SQ // SWA_W)),
            pl.BlockSpec((rows, SWA_KV_W), lambda b, n: (b * nblk + n, kcol)),
            pl.BlockSpec((rows, SWA_KV_W), lambda b, n: (b * nblk + n, vcol)),
            pl.BlockSpec((SWA_BLOCK, SWA_KV_W), lambda b, n: (prev_idx(b, n), kcol)),
            pl.BlockSpec((SWA_BLOCK, SWA_KV_W), lambda b, n: (prev_idx(b, n), vcol)),
            pl.BlockSpec((SWA_HEADS, 2 * SWA_BLOCK, SWA_BLOCK), lambda b, n: (0, 0, 0)),
            pl.BlockSpec((SWA_HEADS, 2 * SWA_BLOCK, SWA_BLOCK), lambda b, n: (jnp.where(n == 0, 1, 0), 0, 0)),
        ] + casts.specs,
        out_specs=[pl.BlockSpec((rows, SWA_W), lambda b, n: (b * nblk + n, 0))] + casts.specs,
        compiler_params=_cparams(("arbitrary", "arbitrary")),
        name="swa_sink_attention",
    )(sinks, proj, proj, proj, proj, proj, bias, bias, *casts.weights)
    return outs[0], outs[1:]


LN_GROUP_ROWS = 256


def _proj_res_ln_kernel(a1_ref, a2_ref, w1_ref, w2_ref, x_ref, g_ref, b_ref, o_ref):
    for r in range(x_ref.shape[0] // LN_GROUP_ROWS):
        rows = slice(r * LN_GROUP_ROWS, (r + 1) * LN_GROUP_ROWS)
        y = _dot(a1_ref[rows, :], w1_ref[...]) + _dot(a2_ref[rows, :], w2_ref[...])
        o_ref[rows, :] = _layer_norm(DN_ALPHA * x_ref[rows, :] + y, g_ref[...], b_ref[...])


def _proj_res_ln(a1, a2, col1, col2, w_bf, x2d, g, b, *, tm=1024):
    m, d = x2d.shape
    kh = w_bf.shape[0] // 2
    return pl.pallas_call(
        _proj_res_ln_kernel,
        out_shape=jax.ShapeDtypeStruct((m, d), F32),
        grid=(m // tm,),
        in_specs=[
            pl.BlockSpec((tm, kh), lambda i: (i, col1)),
            pl.BlockSpec((tm, kh), lambda i: (i, col2)),
            pl.BlockSpec((kh, d), lambda i: (0, 0), pipeline_mode=pl.Buffered(1)),
            pl.BlockSpec((kh, d), lambda i: (1, 0), pipeline_mode=pl.Buffered(1)),
            pl.BlockSpec((tm, d), lambda i: (i, 0)),
            pl.BlockSpec((1, d), lambda i: (0, 0)),
            pl.BlockSpec((1, d), lambda i: (0, 0)),
        ],
        out_specs=pl.BlockSpec((tm, d), lambda i: (i, 0)),
        compiler_params=_cparams(("parallel",)),
        name="proj_residual_ln",
    )(a1, a2, w_bf, w_bf, x2d, g, b)


def _matmul_kernel(a_ref, w_ref, o_ref):
    o_ref[...] = _dot(a_ref[...], w_ref[...]).astype(o_ref.dtype)


def _matmul(a_bf, w_bf, *, tm, tn):
    m, k = a_bf.shape
    n = w_bf.shape[1]
    return pl.pallas_call(
        _matmul_kernel,
        out_shape=jax.ShapeDtypeStruct((m, n), BF16),
        grid=(m // tm, n // tn),
        in_specs=[pl.BlockSpec((tm, k), lambda i, j: (i, 0)),
                  pl.BlockSpec((k, tn), lambda i, j: (0, j))],
        out_specs=pl.BlockSpec((tm, tn), lambda i, j: (i, j)),
        compiler_params=_cparams(("parallel", "arbitrary")),
        name="matmul",
    )(a_bf, w_bf)


XA_GROUP_ROWS = 256


def _xattn_kernel(x_ref, wq_ref, k_ref, v_ref, o_ref, *, hd):
    groups = x_ref.shape[0] // XA_GROUP_ROWS
    tasks = [(r, h) for r in range(groups) for h in range(XA_HEADS)]
    xb = {}

    def qproj(task):
        r, h = task
        if r not in xb:
            xb[r] = x_ref[r * XA_GROUP_ROWS:(r + 1) * XA_GROUP_ROWS, :].astype(BF16)
        return _dot(xb[r], wq_ref[:, h * hd:(h + 1) * hd]).astype(BF16)

    def pv(task, p, den):
        r, h = task
        rows = slice(r * XA_GROUP_ROWS, (r + 1) * XA_GROUP_ROWS)
        cols = slice(h * hd, (h + 1) * hd)
        o_ref[rows, cols] = (_dot(p, v_ref[:, cols]) / den).astype(o_ref.dtype)

    q_next = qproj(tasks[0])
    held = None
    for n, task in enumerate(tasks):
        q = q_next
        if n + 1 < len(tasks):
            q_next = qproj(tasks[n + 1])
        h = task[1]
        logits = _dot_nt(q, k_ref[:, h * hd:(h + 1) * hd]) * (hd ** -0.5)
        if held is not None:
            pv(*held)
        mx = jnp.max(logits, axis=-1, keepdims=True)
        p = jnp.exp(logits - mx)
        held = (task, p.astype(BF16), jnp.sum(p, axis=-1, keepdims=True))
    pv(*held)


def _xattn(x2d, wq_bf, kv, seq, mem_len, *, tm=1024):
    m, d = x2d.shape
    hd = d // XA_HEADS
    per_b = seq // tm
    return pl.pallas_call(
        functools.partial(_xattn_kernel, hd=hd),
        out_shape=jax.ShapeDtypeStruct((m, d), BF16),
        grid=(m // tm,),
        in_specs=[
            pl.BlockSpec((tm, d), lambda i: (i, 0)),
            pl.BlockSpec((d, d), lambda i: (0, 0), pipeline_mode=pl.Buffered(1)),
            pl.BlockSpec((mem_len, d), lambda i: (i // per_b, 0)),
            pl.BlockSpec((mem_len, d), lambda i: (i // per_b, 1)),
        ],
        out_specs=pl.BlockSpec((tm, d), lambda i: (i, 0)),
        compiler_params=_cparams(("parallel",)),
        name="memory_cross_attention",
    )(x2d, wq_bf, kv, kv)


FFN_HALO = 16
FFN_GROUP_ROWS = 256
FFN_CHUNK = 256


def _ffn_kernel(x_ref, xh_ref, wu_ref, wg_ref, cw_ref, cb_ref, wd_ref, g_ref, b_ref, o_ref,
                xb_ref, *, tm, per_b):
    i = pl.program_id(0)
    f = pl.program_id(1)
    nf = pl.num_programs(1)

    groups = tm // FFN_GROUP_ROWS

    def body(first, last):
        cw = cw_ref[...]
        cb = cb_ref[...]

        def up(r):
            r0 = r * FFN_GROUP_ROWS
            if first:
                xb_ref[FFN_HALO + r0:FFN_HALO + r0 + FFN_GROUP_ROWS, :] = (
                    x_ref[r0:r0 + FFN_GROUP_ROWS, :].astype(BF16))
            u = _dot(xb_ref[FFN_HALO + r0:FFN_HALO + r0 + FFN_GROUP_ROWS, :], wu_ref[...])
            ge = _dot(xb_ref[r0:r0 + FFN_GROUP_ROWS + FFN_HALO, :], wg_ref[...])
            return u, ge

        if first:
            halo = jnp.where(i % per_b == 0, 0.0, xh_ref[...])
            xb_ref[0:FFN_HALO, :] = halo.astype(BF16)
        nxt = up(0)
        for r in range(groups):
            u, ge = nxt
            if r + 1 < groups:
                nxt = up(r + 1)
            gc = cb
            for tap in range(CONV_W):
                lo = FFN_HALO - (CONV_W - 1) + tap
                gc = gc + ge[lo:lo + FFN_GROUP_ROWS, :] * cw[tap:tap + 1, :]
            hcur = (gc * jax.nn.sigmoid(gc) * u).astype(BF16)
            rows = slice(r * FFN_GROUP_ROWS, (r + 1) * FFN_GROUP_ROWS)
            acc = _dot(hcur, wd_ref[...])
            acc = (DN_ALPHA * x_ref[rows, :] if first else o_ref[rows, :]) + acc
            if last:
                acc = _layer_norm(acc, g_ref[...], b_ref[...])
            o_ref[rows, :] = acc

    pl.when(f == 0)(functools.partial(body, True, False))
    pl.when((f > 0) & (f < nf - 1))(functools.partial(body, False, False))
    pl.when(f == nf - 1)(functools.partial(body, False, True))


def _ffn(x2d, wup_bf, conv_w, conv_b, wdown_bf, g, b, seq, *, tm=1024, fc=FFN_CHUNK):
    m, d = x2d.shape
    nf = wdown_bf.shape[0] // fc
    assert nf >= 2, "the first and the last hidden chunk are separate code paths"
    per_b = seq // tm
    halo_per_tile = tm // FFN_HALO
    return pl.pallas_call(
        functools.partial(_ffn_kernel, tm=tm, per_b=per_b),
        out_shape=jax.ShapeDtypeStruct((m, d), F32),
        grid=(m // tm, nf),
        in_specs=[
            pl.BlockSpec((tm, d), lambda i, f: (i, 0)),
            pl.BlockSpec((FFN_HALO, d), lambda i, f: (jnp.maximum(i * halo_per_tile - 1, 0), 0)),
            pl.BlockSpec((d, fc), lambda i, f: (0, f)),
            pl.BlockSpec((d, fc), lambda i, f: (0, nf + f)),
            pl.BlockSpec((CONV_W, fc), lambda i, f: (0, f)),
            pl.BlockSpec((1, fc), lambda i, f: (0, f)),
            pl.BlockSpec((fc, d), lambda i, f: (f, 0)),
            pl.BlockSpec((1, d), lambda i, f: (0, 0)),
            pl.BlockSpec((1, d), lambda i, f: (0, 0)),
        ],
        out_specs=pl.BlockSpec((tm, d), lambda i, f: (i, 0)),
        scratch_shapes=[pltpu.VMEM((tm + FFN_HALO, d), BF16)],
        compiler_params=_cparams(("parallel", "arbitrary")),
        name="conv_ffn_ln",
    )(x2d, x2d, wup_bf, wup_bf, conv_w, conv_b, wdown_bf, g, b)


def _rope_tables(seq):
    half = RET_D // 2
    inv = 1.0 / (ROPE_BASE ** (jnp.arange(half, dtype=F32) / half))
    ang = jnp.arange(seq).astype(F32)[:, None] * inv[None, :]
    cos, sin = jnp.cos(ang), jnp.sin(ang)
    return jnp.concatenate([cos, cos], axis=-1), jnp.concatenate([-sin, sin], axis=-1)


def _retention_tables():
    c = RET_CHUNK
    log_gamma = jnp.log1p(-jnp.exp2(-5.0 - jnp.arange(RET_HEADS, dtype=F32)))
    idx = jnp.arange(c, dtype=F32)
    diff = idx[:, None] - idx[None, :]
    dmat = jnp.where(diff[None] >= 0,
                     jnp.exp(log_gamma[:, None, None] * jnp.maximum(diff, 0.0)[None]), 0.0)
    kdec = jnp.exp(log_gamma[:, None] * (c - 1 - idx)[None, :])
    qdec = jnp.exp(log_gamma[:, None] * (idx + 1.0)[None, :])
    cdec = jnp.exp(log_gamma * c)
    bc = lambda t: jnp.broadcast_to(t[:, :, None], (RET_HEADS, c, RET_D))
    return dmat, bc(kdec), bc(qdec), cdec


def _t5_bucket_table():
    l = SWA_BLOCK
    dist = jnp.maximum(jnp.arange(l)[:, None] + l - jnp.arange(2 * l)[None, :], 0)
    max_exact = REL_BUCKETS // 2
    nf = jnp.maximum(dist, 1).astype(F32)
    large = max_exact + (jnp.log(nf / max_exact) / math.log(REL_MAX_DIST / max_exact)
                         * (REL_BUCKETS - max_exact)).astype(jnp.int32)
    large = jnp.minimum(large, REL_BUCKETS - 1)
    return jnp.where(dist < max_exact, dist, large).astype(jnp.int32).T


def kernel(x, mem, w_in, ret_gn_g, swa_sinks, rel_bias, w_o, ln1_g, ln1_b, xa_wq, xa_wkv, xa_wo,
           ln2_g, ln2_b, ffn_w_up, ffn_conv_w, ffn_conv_b, ffn_w_down, ln3_g, ln3_b):
    batch, seq, d = x.shape
    mem_len = mem.shape[1]
    assert w_in.shape[0] == DEPTH and w_in.shape[2] == IN_W
    assert seq % 1024 == 0 and d % LANES == 0

    cosf, sinf = _rope_tables(seq)
    dmat, kdec, qdec, cdec = _retention_tables()
    bucket = _t5_bucket_table()
    bias = _bias_table(rel_bias, bucket)
    row = lambda t: t.reshape(1, -1)

    x2d = x.reshape(batch * seq, d)
    mem_bf = mem.reshape(batch * mem_len, d).astype(BF16)
    for l in range(DEPTH):
        proj = _in_proj(x2d, w_in[l], cosf, sinf, seq)
        o_r, (wo_bf, wkv_bf, wq_bf, xwo_bf) = _retention(
            proj, cdec, dmat, kdec, qdec, row(ret_gn_g[l]), batch, seq,
            [w_o[l], xa_wkv[l], xa_wq[l], xa_wo[l]])
        o_s, (wup_bf, wdown_bf) = _swa(proj, swa_sinks[l], bias, batch, seq,
                                       [ffn_w_up[l], ffn_w_down[l]])
        x2d = _proj_res_ln(o_r, o_s, 0, 0, wo_bf, x2d, row(ln1_g[l]), row(ln1_b[l]))

        kv = _matmul(mem_bf, wkv_bf, tm=batch * mem_len, tn=1024)
        xa = _xattn(x2d, wq_bf, kv, seq, mem_len)
        x2d = _proj_res_ln(xa, xa, 0, 1, xwo_bf, x2d, row(ln2_g[l]), row(ln2_b[l]))

        x2d = _ffn(x2d, wup_bf, ffn_conv_w[l], row(ffn_conv_b[l]),
                   wdown_bf, row(ln3_g[l]), row(ln3_b[l]), seq)
    return x2d.reshape(batch, seq, d)
```
